```python
import math
import jax, jax.numpy as jnp
from jax import lax
import numpy as np

D_MODEL = 2048
BATCH = 2
SEQ = 4096
DEPTH = 1

MEM_LEN = 256
HEAD_DIM = 128
DIFF_HEADS = 8
DIFF_HALF = HEAD_DIM // 2
DIFF_WIDTH = DIFF_HEADS * HEAD_DIM
MOBA_HEADS = 8
MOBA_WIDTH = MOBA_HEADS * HEAD_DIM
D_MIX = DIFF_WIDTH + MOBA_WIDTH
IN_COLS = 3 * DIFF_WIDTH + 3 * MOBA_WIDTH + D_MIX
MOBA_BLOCK = 256
MOBA_TOPK = 3
MOBA_Q_CHUNK = 32
DENSE_Q_BLOCK = 128
MEM_HEADS = 4
MEM_WIDTH = MEM_HEADS * HEAD_DIM
EPS = 1e-6
SPLITS = [DIFF_WIDTH, 2 * DIFF_WIDTH, 3 * DIFF_WIDTH,
          3 * DIFF_WIDTH + MOBA_WIDTH, 3 * DIFF_WIDTH + 2 * MOBA_WIDTH,
          3 * DIFF_WIDTH + 3 * MOBA_WIDTH]

kernel_name = "hybrid_diffattn_moba_gated_memxattn"


def rmsnorm(x, g):
    xf = x.astype(jnp.float32)
    r = lax.rsqrt(jnp.mean(xf * xf, axis=-1, keepdims=True) + EPS)
    return (xf * r).astype(x.dtype) * g


def alibi_slopes(n):
    return jnp.asarray(2.0 ** (-8.0 * np.arange(1, n + 1) / n), dtype=jnp.float32)


def diff_attention(q, k, v, lam, subln_g, lam_init, slopes):
    B, H, S = q.shape[:3]
    nqb = S // DENSE_Q_BLOCK
    scale = DIFF_HALF ** -0.5
    kpos = jnp.arange(S)
    qb = q.reshape(B, H, nqb, DENSE_Q_BLOCK, 2, DIFF_HALF).transpose(2, 0, 1, 3, 4, 5)

    def one_block(args):
        i, qi = args
        qpos = i * DENSE_Q_BLOCK + jnp.arange(DENSE_Q_BLOCK)
        dist = (qpos[:, None] - kpos[None, :]).astype(jnp.float32)
        bias = -slopes[:, None, None] * dist
        logits = jnp.einsum('bhqcd,bhkcd->bhcqk', qi, k).astype(jnp.float32) * scale
        logits = jnp.where(dist >= 0, logits + bias[None, :, None], -jnp.inf)
        p = jax.nn.softmax(logits, axis=-1)
        attn = p[:, :, 0] - lam.astype(jnp.float32) * p[:, :, 1]
        return jnp.einsum('bhqk,bhkd->bhqd', attn.astype(v.dtype), v)

    o = lax.map(one_block, (jnp.arange(nqb), qb))
    o = o.transpose(1, 2, 0, 3, 4).reshape(B, H, S, HEAD_DIM)
    return rmsnorm(o, subln_g) * (1.0 - lam_init)


def moba_attention(q, k, v, slopes):
    B, H, S, D = q.shape
    nb = -(-S // MOBA_BLOCK)
    pad = nb * MOBA_BLOCK - S
    scale = D ** -0.5
    tk = min(MOBA_TOPK, nb)
    kpad = jnp.pad(k, ((0, 0), (0, 0), (0, pad), (0, 0)))
    vpad = jnp.pad(v, ((0, 0), (0, 0), (0, pad), (0, 0)))
    kb = kpad.reshape(B, H, nb, MOBA_BLOCK, D)
    vb = vpad.reshape(B, H, nb, MOBA_BLOCK, D)
    kmean = jnp.mean(kb.astype(jnp.float32), axis=3)
    nqc = S // MOBA_Q_CHUNK
    qc = q.reshape(B, H, nqc, MOBA_Q_CHUNK, D).transpose(2, 0, 1, 3, 4)
    bi = jnp.arange(B)[:, None, None, None]
    hi = jnp.arange(H)[None, :, None, None]
    blk = jnp.arange(MOBA_BLOCK)

    def one_chunk(args):
        i, qi = args
        start = i * MOBA_Q_CHUNK
        cur = start // MOBA_BLOCK
        qpos = start + jnp.arange(MOBA_Q_CHUNK)
        gate = jnp.einsum('bhqd,bhnd->bhqn', qi.astype(jnp.float32), kmean)
        gate = jnp.where(jnp.arange(nb) < cur, gate, -jnp.inf)
        _, idx = lax.top_k(gate, tk)
        valid = idx < cur
        kg = kb[bi, hi, idx]
        vg = vb[bi, hi, idx]
        lg = jnp.einsum('bhqd,bhqjsd->bhqjs', qi, kg).astype(jnp.float32) * scale
        spos = idx[..., None] * MOBA_BLOCK + blk
        gdist = (qpos[None, None, :, None, None] - spos).astype(jnp.float32)
        lg = lg - slopes[None, :, None, None, None] * gdist
        lg = jnp.where(valid[..., None], lg, -jnp.inf).reshape(B, H, MOBA_Q_CHUNK, tk * MOBA_BLOCK)
        own_k = lax.dynamic_slice_in_dim(kpad, cur * MOBA_BLOCK, MOBA_BLOCK, axis=2)
        own_v = lax.dynamic_slice_in_dim(vpad, cur * MOBA_BLOCK, MOBA_BLOCK, axis=2)
        lo = jnp.einsum('bhqd,bhsd->bhqs', qi, own_k).astype(jnp.float32) * scale
        odist = (qpos[:, None] - (cur * MOBA_BLOCK + blk)[None, :]).astype(jnp.float32)
        lo = jnp.where(odist >= 0, lo - slopes[:, None, None] * odist, -jnp.inf)
        p = jax.nn.softmax(jnp.concatenate([lg, lo], axis=-1), axis=-1)
        pg = p[..., :tk * MOBA_BLOCK].reshape(B, H, MOBA_Q_CHUNK, tk, MOBA_BLOCK).astype(v.dtype)
        po = p[..., tk * MOBA_BLOCK:].astype(v.dtype)
        return (jnp.einsum('bhqjs,bhqjsd->bhqd', pg, vg)
                + jnp.einsum('bhqs,bhsd->bhqd', po, own_v))

    o = lax.map(one_chunk, (jnp.arange(nqc), qc))
    return o.transpose(1, 2, 0, 3, 4).reshape(B, H, S, D)


def setup_inputs(seed: int = 0) -> dict:
    key = jax.random.key(seed)
    ks = jax.random.split(key, 16)
    f32 = jnp.float32

    def w(k, shape, fan_in):
        return jax.random.normal(k, shape, f32) * fan_in ** -0.5

    def gain(k, shape):
        return 1.0 + 0.02 * jax.random.normal(k, shape, f32)

    return {
        "x": jax.random.normal(ks[0], (BATCH, SEQ, D_MODEL), f32),
        "mem": jax.random.normal(ks[1], (BATCH, MEM_LEN, D_MODEL), f32),
        "norm_mix_g": gain(ks[2], (DEPTH, D_MODEL)),
        "w_in": w(ks[3], (DEPTH, D_MODEL, IN_COLS), D_MODEL),
        "lambda_q1": 0.1 * jax.random.normal(ks[4], (DEPTH, DIFF_HALF), f32),
        "lambda_k1": 0.1 * jax.random.normal(ks[5], (DEPTH, DIFF_HALF), f32),
        "lambda_q2": 0.1 * jax.random.normal(ks[6], (DEPTH, DIFF_HALF), f32),
        "lambda_k2": 0.1 * jax.random.normal(ks[7], (DEPTH, DIFF_HALF), f32),
        "subln_g": gain(ks[8], (DEPTH, HEAD_DIM)),
        "w_out": w(ks[9], (DEPTH, D_MIX, D_MODEL), D_MIX),
        "norm_mem_g": gain(ks[10], (DEPTH, D_MODEL)),
        "mem_norm_g": gain(ks[11], (DEPTH, D_MODEL)),
        "w_q_mem": w(ks[12], (DEPTH, D_MODEL, MEM_WIDTH), D_MODEL),
        "w_kv_mem": w(ks[13], (DEPTH, D_MODEL, 2 * MEM_WIDTH), D_MODEL),
        "w_o_mem": w(ks[14], (DEPTH, MEM_WIDTH, D_MODEL), MEM_WIDTH),
        "final_g": gain(ks[15], (D_MODEL,)),
    }


def reference(x, mem, norm_mix_g, w_in, lambda_q1, lambda_k1, lambda_q2, lambda_k2,
              subln_g, w_out, norm_mem_g, mem_norm_g, w_q_mem, w_kv_mem, w_o_mem, final_g):
    B, S, _ = x.shape
    M = mem.shape[1]
    diff_slopes = alibi_slopes(DIFF_HEADS)
    moba_slopes = alibi_slopes(MOBA_HEADS)
    for l in range(DEPTH):
        h = rmsnorm(x, norm_mix_g[l])
        proj = jnp.einsum('bsd,de->bse', h, w_in[l])
        dq, dk, dv, mq, mk, mv, gate = jnp.split(proj, SPLITS, axis=-1)
        dq = dq.reshape(B, S, DIFF_HEADS, 2, DIFF_HALF).transpose(0, 2, 1, 3, 4)
        dk = dk.reshape(B, S, DIFF_HEADS, 2, DIFF_HALF).transpose(0, 2, 1, 3, 4)
        dv = dv.reshape(B, S, DIFF_HEADS, HEAD_DIM).transpose(0, 2, 1, 3)
        lam_init = 0.8 - 0.6 * math.exp(-0.3 * l)
        lam = (jnp.exp(jnp.sum(lambda_q1[l].astype(jnp.float32) * lambda_k1[l].astype(jnp.float32)))
               - jnp.exp(jnp.sum(lambda_q2[l].astype(jnp.float32) * lambda_k2[l].astype(jnp.float32)))
               + lam_init)
        d_out = diff_attention(dq, dk, dv, lam, subln_g[l], lam_init, diff_slopes)
        d_out = d_out.transpose(0, 2, 1, 3).reshape(B, S, DIFF_WIDTH)
        mq = mq.reshape(B, S, MOBA_HEADS, HEAD_DIM).transpose(0, 2, 1, 3)
        mk = mk.reshape(B, S, MOBA_HEADS, HEAD_DIM).transpose(0, 2, 1, 3)
        mv = mv.reshape(B, S, MOBA_HEADS, HEAD_DIM).transpose(0, 2, 1, 3)
        m_out = moba_attention(mq, mk, mv, moba_slopes)
        m_out = m_out.transpose(0, 2, 1, 3).reshape(B, S, MOBA_WIDTH)
        y = jnp.concatenate([d_out, m_out], axis=-1) * jax.nn.silu(gate)
        x = x + jnp.einsum('bse,ed->bsd', y, w_out[l])
        h2 = rmsnorm(x, norm_mem_g[l])
        mn = rmsnorm(mem, mem_norm_g[l])
        q = jnp.einsum('bsd,de->bse', h2, w_q_mem[l]).reshape(B, S, MEM_HEADS, HEAD_DIM)
        kv = jnp.einsum('bmd,de->bme', mn, w_kv_mem[l])
        km, vm = jnp.split(kv, 2, axis=-1)
        km = km.reshape(B, M, MEM_HEADS, HEAD_DIM)
        vm = vm.reshape(B, M, MEM_HEADS, HEAD_DIM)
        logits = jnp.einsum('bshd,bmhd->bhsm', q, km).astype(jnp.float32) * HEAD_DIM ** -0.5
        p = jax.nn.softmax(logits, axis=-1).astype(vm.dtype)
        o = jnp.einsum('bhsm,bmhd->bshd', p, vm).reshape(B, S, MEM_WIDTH)
        x = x + jnp.einsum('bse,ed->bsd', o, w_o_mem[l])
    return rmsnorm(x, final_g)
```

```python
import functools
import math

import jax
import jax.numpy as jnp
import numpy as np
from jax import lax
from jax.experimental import pallas as pl
from jax.experimental.pallas import tpu as pltpu

HEAD_DIM = 128
DIFF_HEADS = 8
DIFF_HALF = HEAD_DIM // 2
DIFF_WIDTH = DIFF_HEADS * HEAD_DIM
MOBA_HEADS = 8
MOBA_WIDTH = MOBA_HEADS * HEAD_DIM
MOBA_BLOCK = 256
MOBA_TOPK = 3
MEM_HEADS = 4
MEM_WIDTH = MEM_HEADS * HEAD_DIM
EPS = 1e-6

_DQ_BLK = 0
_DK_BLK = DIFF_HEADS
_DV_BLK = 2 * DIFF_HEADS
_MQ_BLK = 3 * DIFF_HEADS
_MK_BLK = 3 * DIFF_HEADS + MOBA_HEADS
_MV_BLK = 3 * DIFF_HEADS + 2 * MOBA_HEADS
_GATE_BLK = 3 * DIFF_HEADS + 3 * MOBA_HEADS

_V7X_VMEM_BYTES = 64 * 1024 * 1024
_VMEM_LIMIT = _V7X_VMEM_BYTES * 3 // 4

_BF16 = jnp.bfloat16
_F32 = jnp.float32
_NEG_INF = float("-inf")


def _dot_nt(a, b):
    return lax.dot_general(a, b, (((1,), (1,)), ((), ())), preferred_element_type=_F32)


def _dot(a, b):
    return jnp.dot(a, b, preferred_element_type=_F32)


def _silu(g):
    return g * jax.nn.sigmoid(g)


def _rms_matmul_kernel(x_ref, g_ref, w_ref, o_ref, h_scr):
    @pl.when(pl.program_id(1) == 0)
    def _():
        x = x_ref[...]
        ms = jnp.mean(x * x, axis=-1, keepdims=True)
        h_scr[...] = (x * lax.rsqrt(ms + EPS) * g_ref[...]).astype(_BF16)

    o_ref[...] = _dot(h_scr[...], w_ref[...]).astype(o_ref.dtype)


def rms_matmul(x, g, w, *, tm, tn, out_dtype=_BF16):
    m, k = x.shape
    n = w.shape[1]
    assert m % tm == 0 and n % tn == 0
    return pl.pallas_call(
        _rms_matmul_kernel,
        grid=(m // tm, n // tn),
        in_specs=[
            pl.BlockSpec((tm, k), lambda i, j: (i, 0)),
            pl.BlockSpec((1, k), lambda i, j: (0, 0)),
            pl.BlockSpec((k, tn), lambda i, j: (0, j)),
        ],
        out_specs=pl.BlockSpec((tm, tn), lambda i, j: (i, j)),
        out_shape=jax.ShapeDtypeStruct((m, n), out_dtype),
        scratch_shapes=[pltpu.VMEM((tm, k), _BF16)],
        compiler_params=pltpu.CompilerParams(
            dimension_semantics=("parallel", "arbitrary"),
            vmem_limit_bytes=_VMEM_LIMIT),
        name="rms_matmul",
    )(x, g.reshape(1, k), w)


def _matmul_residual_kernel(ya_ref, yb_ref, wa_ref, wb_ref, r_ref, o_ref):
    o_ref[...] = (r_ref[...] + _dot(ya_ref[...], wa_ref[...])
                  + _dot(yb_ref[...], wb_ref[...]))


def matmul_residual(ya, yb, w, res, *, tm, tn):
    m, ka = ya.shape
    kb = yb.shape[1]
    n = w.shape[1]
    assert m % tm == 0 and n % tn == 0 and ka == kb and w.shape[0] == ka + kb
    return pl.pallas_call(
        _matmul_residual_kernel,
        grid=(m // tm, n // tn),
        in_specs=[
            pl.BlockSpec((tm, ka), lambda i, j: (i, 0)),
            pl.BlockSpec((tm, kb), lambda i, j: (i, 0)),
            pl.BlockSpec((ka, tn), lambda i, j: (0, j)),
            pl.BlockSpec((kb, tn), lambda i, j: (1, j)),
            pl.BlockSpec((tm, tn), lambda i, j: (i, j)),
        ],
        out_specs=pl.BlockSpec((tm, tn), lambda i, j: (i, j)),
        out_shape=jax.ShapeDtypeStruct((m, n), _F32),
        compiler_params=pltpu.CompilerParams(
            dimension_semantics=("parallel", "arbitrary"),
            vmem_limit_bytes=_VMEM_LIMIT),
        name="matmul_residual",
    )(ya, yb, w, w, res)


def _online_softmax_step(z, rb, v, m, l, acc):
    m_new = jnp.maximum(m, jnp.max(z, axis=-1, keepdims=True) + rb)
    p = jnp.exp(z - (m_new - rb))
    alpha = jnp.exp(m - m_new)
    l_new = alpha * l + jnp.sum(p, axis=-1, keepdims=True)
    acc_new = alpha * acc + _dot(p.astype(_BF16), v)
    return m_new, l_new, acc_new


def _diff_attention_kernel(slopes_ref, lq1_ref, lk1_ref, lq2_ref, lk2_ref, subg_ref,
                           q_ref, k_ref, v_ref, gate_ref, o_ref, *, lam_init, tq):
    h = pl.program_id(1)
    i = pl.program_id(2)
    slope = slopes_ref[h]
    scale = DIFF_HALF ** -0.5

    q = q_ref[...]
    lane = lax.broadcasted_iota(jnp.int32, q.shape, 1)
    zero = jnp.zeros_like(q)
    qs = (jnp.where(lane < DIFF_HALF, q, zero) * scale,
          jnp.where(lane >= DIFF_HALF, q, zero) * scale)

    row = lax.broadcasted_iota(jnp.int32, (tq, 1), 0).astype(_F32)
    col = lax.broadcasted_iota(jnp.int32, (1, tq), 1).astype(_F32)
    col_bias = slope * col

    k_d = k_ref[pl.ds(i * tq, tq), :]
    v_d = v_ref[pl.ds(i * tq, tq), :]
    rb_d = -slope * row
    causal = (lax.broadcasted_iota(jnp.int32, (tq, tq), 1)
              <= lax.broadcasted_iota(jnp.int32, (tq, tq), 0))
    state = []
    for qc in qs:
        z = jnp.where(causal, _dot_nt(qc, k_d) + col_bias, _NEG_INF)
        m0 = jnp.max(z, axis=-1, keepdims=True) + rb_d
        p = jnp.exp(z - (m0 - rb_d))
        state += [m0, jnp.sum(p, axis=-1, keepdims=True), _dot(p.astype(_BF16), v_d)]

    def body(j, st):
        k_j = k_ref[pl.ds(j * tq, tq), :]
        v_j = v_ref[pl.ds(j * tq, tq), :]
        rb = rb_d - slope * ((i - j) * tq).astype(_F32)
        out = []
        for c, qc in enumerate(qs):
            z = _dot_nt(qc, k_j) + col_bias
            out += list(_online_softmax_step(z, rb, v_j, *st[3 * c:3 * c + 3]))
        return tuple(out)

    m1, l1, a1, m2, l2, a2 = lax.fori_loop(0, i, body, tuple(state))

    lam = (jnp.exp(jnp.sum(lq1_ref[...] * lk1_ref[...], axis=-1, keepdims=True))
           - jnp.exp(jnp.sum(lq2_ref[...] * lk2_ref[...], axis=-1, keepdims=True))
           + lam_init)
    o = a1 / l1 - lam * (a2 / l2)
    ms = jnp.mean(o * o, axis=-1, keepdims=True)
    o = o * lax.rsqrt(ms + EPS) * subg_ref[...] * (1.0 - lam_init)
    o_ref[...] = (o * _silu(gate_ref[...].astype(_F32))).astype(o_ref.dtype)


def diff_attention(proj, slopes, lq1, lk1, lq2, lk2, subg, *, lam_init, tq):
    b, s, _ = proj.shape
    assert s % tq == 0
    vec = lambda n: pl.BlockSpec((1, n), lambda bi, hi, i: (0, 0))
    return pl.pallas_call(
        functools.partial(_diff_attention_kernel, lam_init=lam_init, tq=tq),
        grid=(b, DIFF_HEADS, s // tq),
        in_specs=[
            pl.BlockSpec(memory_space=pltpu.SMEM),
            vec(DIFF_HALF), vec(DIFF_HALF), vec(DIFF_HALF), vec(DIFF_HALF), vec(HEAD_DIM),
            pl.BlockSpec((None, tq, HEAD_DIM), lambda bi, hi, i: (bi, i, _DQ_BLK + hi)),
            pl.BlockSpec((None, s, HEAD_DIM), lambda bi, hi, i: (bi, 0, _DK_BLK + hi)),
            pl.BlockSpec((None, s, HEAD_DIM), lambda bi, hi, i: (bi, 0, _DV_BLK + hi)),
            pl.BlockSpec((None, tq, HEAD_DIM), lambda bi, hi, i: (bi, i, _GATE_BLK + hi)),
        ],
        out_specs=pl.BlockSpec((None, tq, HEAD_DIM), lambda bi, hi, i: (bi, i, hi)),
        out_shape=jax.ShapeDtypeStruct((b, s, DIFF_WIDTH), _BF16),
        compiler_params=pltpu.CompilerParams(
            dimension_semantics=("parallel", "parallel", "arbitrary"),
            vmem_limit_bytes=_VMEM_LIMIT),
        name="diff_attention",
    )(slopes, lq1.reshape(1, -1), lk1.reshape(1, -1), lq2.reshape(1, -1), lk2.reshape(1, -1),
      subg.reshape(1, -1), proj, proj, proj, proj)


def _moba_attention_kernel(slopes_ref, q_ref, k_ref, v_ref, gate_ref, o_ref, kmean_scr,
                           *, n_blocks):
    h = pl.program_id(1)
    i = pl.program_id(2)
    slope = slopes_ref[h]
    scale = HEAD_DIM ** -0.5
    tq = MOBA_BLOCK

    @pl.when(i == 0)
    def _():
        for n in range(n_blocks):
            kb = k_ref[n * MOBA_BLOCK:(n + 1) * MOBA_BLOCK, :].astype(_F32)
            kmean_scr[n:n + 1, :] = jnp.sum(kb, axis=0, keepdims=True) * (1.0 / MOBA_BLOCK)

    q = q_ref[...]

    gate = lax.dot_general(q.astype(_F32), kmean_scr[...], (((1,), (1,)), ((), ())),
                           precision=lax.Precision.HIGHEST,
                           preferred_element_type=_F32)
    blk = lax.broadcasted_iota(jnp.int32, gate.shape, 1)
    past = blk < i
    gate = jnp.where(past, gate, _NEG_INF)
    rank = jnp.zeros(gate.shape, jnp.int32)
    for n in range(n_blocks):
        g_n = gate[:, n:n + 1]
        beats = (g_n > gate) | ((g_n == gate) & (n < blk))
        rank = rank + beats.astype(jnp.int32)
    selected = jnp.where(past & (rank < MOBA_TOPK), 1.0, 0.0)

    row = lax.broadcasted_iota(jnp.int32, (tq, 1), 0).astype(_F32)
    col = lax.broadcasted_iota(jnp.int32, (1, tq), 1).astype(_F32)
    col_bias = slope * col
    rb_d = -slope * row

    k_d = k_ref[pl.ds(i * tq, tq), :]
    v_d = v_ref[pl.ds(i * tq, tq), :]
    causal = (lax.broadcasted_iota(jnp.int32, (tq, tq), 1)
              <= lax.broadcasted_iota(jnp.int32, (tq, tq), 0))
    z = jnp.where(causal, _dot_nt(q, k_d) * scale + col_bias, _NEG_INF)
    m0 = jnp.max(z, axis=-1, keepdims=True) + rb_d
    p = jnp.exp(z - (m0 - rb_d))
    state = (m0, jnp.sum(p, axis=-1, keepdims=True), _dot(p.astype(_BF16), v_d))

    def body(j, st):
        k_j = k_ref[pl.ds(j * tq, tq), :]
        v_j = v_ref[pl.ds(j * tq, tq), :]
        rb = rb_d - slope * ((i - j) * tq).astype(_F32)
        sel_j = jnp.sum(jnp.where(blk == j, selected, 0.0), axis=-1, keepdims=True) > 0.5
        z = jnp.where(sel_j, _dot_nt(q, k_j) * scale + col_bias, _NEG_INF)
        return _online_softmax_step(z, rb, v_j, *st)

    m, l, acc = lax.fori_loop(0, i, body, state)
    o = acc / l
    o_ref[...] = (o * _silu(gate_ref[...].astype(_F32))).astype(o_ref.dtype)


def moba_attention(proj, slopes):
    b, s, _ = proj.shape
    assert s % MOBA_BLOCK == 0
    n_blocks = s // MOBA_BLOCK
    tq = MOBA_BLOCK
    return pl.pallas_call(
        functools.partial(_moba_attention_kernel, n_blocks=n_blocks),
        grid=(b, MOBA_HEADS, n_blocks),
        in_specs=[
            pl.BlockSpec(memory_space=pltpu.SMEM),
            pl.BlockSpec((None, tq, HEAD_DIM), lambda bi, hi, i: (bi, i, _MQ_BLK + hi)),
            pl.BlockSpec((None, s, HEAD_DIM), lambda bi, hi, i: (bi, 0, _MK_BLK + hi)),
            pl.BlockSpec((None, s, HEAD_DIM), lambda bi, hi, i: (bi, 0, _MV_BLK + hi)),
            pl.BlockSpec((None, tq, HEAD_DIM),
                         lambda bi, hi, i: (bi, i, _GATE_BLK + DIFF_HEADS + hi)),
        ],
        out_specs=pl.BlockSpec((None, tq, HEAD_DIM), lambda bi, hi, i: (bi, i, hi)),
        out_shape=jax.ShapeDtypeStruct((b, s, MOBA_WIDTH), _BF16),
        scratch_shapes=[pltpu.VMEM((n_blocks, HEAD_DIM), _F32)],
        compiler_params=pltpu.CompilerParams(
            dimension_semantics=("parallel", "parallel", "arbitrary"),
            vmem_limit_bytes=_VMEM_LIMIT),
        name="moba_attention",
    )(slopes, proj, proj, proj, proj)


def _mem_attention_kernel(q_ref, kv_ref, wo_ref, x_ref, fg_ref, o_ref, *, apply_final):
    scale = HEAD_DIM ** -0.5
    x = x_ref[...]
    for hh in range(MEM_HEADS):
        lo, hi = hh * HEAD_DIM, (hh + 1) * HEAD_DIM
        s = _dot_nt(q_ref[:, lo:hi], kv_ref[:, lo:hi]) * scale
        p = jnp.exp(s - jnp.max(s, axis=-1, keepdims=True))
        l = jnp.sum(p, axis=-1, keepdims=True)
        o_h = _dot(p.astype(_BF16), kv_ref[:, MEM_WIDTH + lo:MEM_WIDTH + hi]) / l
        x = x + _dot(o_h.astype(_BF16), wo_ref[lo:hi, :])
    if apply_final:
        ms = jnp.mean(x * x, axis=-1, keepdims=True)
        x = x * lax.rsqrt(ms + EPS) * fg_ref[...]
    o_ref[...] = x


def mem_attention(q, kv, wo, x, final_g, *, apply_final, tm):
    b, s, d = x.shape
    mlen = kv.shape[1]
    assert s % tm == 0
    return pl.pallas_call(
        functools.partial(_mem_attention_kernel, apply_final=apply_final),
        grid=(b, s // tm),
        in_specs=[
            pl.BlockSpec((None, tm, MEM_WIDTH), lambda bi, i: (bi, i, 0)),
            pl.BlockSpec((None, mlen, 2 * MEM_WIDTH), lambda bi, i: (bi, 0, 0)),
            pl.BlockSpec((MEM_WIDTH, d), lambda bi, i: (0, 0)),
            pl.BlockSpec((None, tm, d), lambda bi, i: (bi, i, 0)),
            pl.BlockSpec((1, d), lambda bi, i: (0, 0)),
        ],
        out_specs=pl.BlockSpec((None, tm, d), lambda bi, i: (bi, i, 0)),
        out_shape=jax.ShapeDtypeStruct((b, s, d), _F32),
        compiler_params=pltpu.CompilerParams(
            dimension_semantics=("parallel", "parallel"),
            vmem_limit_bytes=_VMEM_LIMIT),
        name="mem_attention",
    )(q, kv, wo, x, final_g.reshape(1, d))


def _alibi_slopes(n):
    return jnp.asarray(2.0 ** (-8.0 * np.arange(1, n + 1) / n), dtype=_F32)


def kernel(x, mem, norm_mix_g, w_in, lambda_q1, lambda_k1, lambda_q2, lambda_k2, subln_g,
           w_out, norm_mem_g, mem_norm_g, w_q_mem, w_kv_mem, w_o_mem, final_g):
    b, s, d = x.shape
    mlen = mem.shape[1]
    depth = w_in.shape[0]
    diff_slopes = _alibi_slopes(DIFF_HEADS)
    moba_slopes = _alibi_slopes(MOBA_HEADS)
    mem2 = mem.reshape(b * mlen, d)
    for l in range(depth):
        lam_init = 0.8 - 0.6 * math.exp(-0.3 * l)
        proj = rms_matmul(x.reshape(b * s, d), norm_mix_g[l], w_in[l].astype(_BF16),
                          tm=512, tn=1024).reshape(b, s, -1)
        y_d = diff_attention(proj, diff_slopes, lambda_q1[l], lambda_k1[l], lambda_q2[l],
                             lambda_k2[l], subln_g[l], lam_init=lam_init, tq=256)
        y_m = moba_attention(proj, moba_slopes)
        x1 = matmul_residual(y_d.reshape(b * s, -1), y_m.reshape(b * s, -1),
                             w_out[l].astype(_BF16), x.reshape(b * s, d), tm=512, tn=1024)
        kv = rms_matmul(mem2, mem_norm_g[l], w_kv_mem[l].astype(_BF16),
                        tm=b * mlen, tn=2 * MEM_WIDTH).reshape(b, mlen, 2 * MEM_WIDTH)
        q = rms_matmul(x1, norm_mem_g[l], w_q_mem[l].astype(_BF16),
                       tm=512, tn=MEM_WIDTH).reshape(b, s, MEM_WIDTH)
        x = mem_attention(q, kv, w_o_mem[l].astype(_BF16), x1.reshape(b, s, d), final_g,
                          apply_final=(l == depth - 1), tm=512)
    return x
```

```python
import functools
import math

import jax
import jax.numpy as jnp
import numpy as np
from jax import lax
from jax.experimental import pallas as pl
from jax.experimental.pallas import tpu as pltpu

HEAD_DIM = 128
DIFF_HEADS = 8
DIFF_HALF = HEAD_DIM // 2
DIFF_WIDTH = DIFF_HEADS * HEAD_DIM
MOBA_HEADS = 8
MOBA_WIDTH = MOBA_HEADS * HEAD_DIM
MOBA_BLOCK = 256
MOBA_TOPK = 3
MEM_HEADS = 4
MEM_WIDTH = MEM_HEADS * HEAD_DIM
EPS = 1e-6

_DQ_BLK = 0
_DK_BLK = DIFF_HEADS
_DV_BLK = 2 * DIFF_HEADS
_MQ_BLK = 3 * DIFF_HEADS
_MK_BLK = 3 * DIFF_HEADS + MOBA_HEADS
_MV_BLK = 3 * DIFF_HEADS + 2 * MOBA_HEADS
_GATE_BLK = 3 * DIFF_HEADS + 3 * MOBA_HEADS

_V7X_VMEM_BYTES = 64 * 1024 * 1024
_VMEM_LIMIT = _V7X_VMEM_BYTES * 3 // 4

_BF16 = jnp.bfloat16
_F32 = jnp.float32
_NEG_INF = float("-inf")


def _dot_nt(a, b):
    return lax.dot_general(a, b, (((1,), (1,)), ((), ())), preferred_element_type=_F32)


def _dot(a, b):
    return jnp.dot(a, b, preferred_element_type=_F32)


def _silu(g):
    return g * jax.nn.sigmoid(g)


def _rms_matmul_kernel(x_ref, g_ref, w_ref, o_ref, h_scr):
    @pl.when(pl.program_id(1) == 0)
    def _():
        x = x_ref[...]
        ms = jnp.mean(x * x, axis=-1, keepdims=True)
        h_scr[...] = (x * lax.rsqrt(ms + EPS) * g_ref[...]).astype(_BF16)

    o_ref[...] = _dot(h_scr[...], w_ref[...]).astype(o_ref.dtype)


def rms_matmul(x, g, w, *, tm, tn, out_dtype=_BF16):
    m, k = x.shape
    n = w.shape[1]
    assert m % tm == 0 and n % tn == 0
    return pl.pallas_call(
        _rms_matmul_kernel,
        grid=(m // tm, n // tn),
        in_specs=[
            pl.BlockSpec((tm, k), lambda i, j: (i, 0)),
            pl.BlockSpec((1, k), lambda i, j: (0, 0)),
            pl.BlockSpec((k, tn), lambda i, j: (0, j)),
        ],
        out_specs=pl.BlockSpec((tm, tn), lambda i, j: (i, j)),
        out_shape=jax.ShapeDtypeStruct((m, n), out_dtype),
        scratch_shapes=[pltpu.VMEM((tm, k), _BF16)],
        compiler_params=pltpu.CompilerParams(
            dimension_semantics=("parallel", "arbitrary"),
            vmem_limit_bytes=_VMEM_LIMIT),
        name="rms_matmul",
    )(x, g.reshape(1, k), w)


def _matmul_residual_kernel(ya_ref, yb_ref, wa_ref, wb_ref, r_ref, o_ref):
    o_ref[...] = (r_ref[...] + _dot(ya_ref[...], wa_ref[...])
                  + _dot(yb_ref[...], wb_ref[...]))


def matmul_residual(ya, yb, w, res, *, tm, tn):
    m, ka = ya.shape
    kb = yb.shape[1]
    n = w.shape[1]
    assert m % tm == 0 and n % tn == 0 and ka == kb and w.shape[0] == ka + kb
    return pl.pallas_call(
        _matmul_residual_kernel,
        grid=(m // tm, n // tn),
        in_specs=[
            pl.BlockSpec((tm, ka), lambda i, j: (i, 0)),
            pl.BlockSpec((tm, kb), lambda i, j: (i, 0)),
            pl.BlockSpec((ka, tn), lambda i, j: (0, j)),
            pl.BlockSpec((kb, tn), lambda i, j: (1, j)),
            pl.BlockSpec((tm, tn), lambda i, j: (i, j)),
        ],
        out_specs=pl.BlockSpec((tm, tn), lambda i, j: (i, j)),
        out_shape=jax.ShapeDtypeStruct((m, n), _F32),
        compiler_params=pltpu.CompilerParams(
            dimension_semantics=("parallel", "arbitrary"),
            vmem_limit_bytes=_VMEM_LIMIT),
        name="matmul_residual",
    )(ya, yb, w, w, res)


def _first_flash_step(z, rb, vt, acc_ref):
    m = jnp.max(z, axis=0, keepdims=True) + rb
    p = jnp.exp(z - (m - rb))
    acc_ref[...] = _dot(vt, p.astype(_BF16))
    return m, jnp.sum(p, axis=0, keepdims=True)


def _flash_step(z, rb, vt, m, l, acc_ref):
    m_new = jnp.maximum(m, jnp.max(z, axis=0, keepdims=True) + rb)
    p = jnp.exp(z - (m_new - rb))
    alpha = jnp.exp(m - m_new)
    acc_ref[...] = alpha * acc_ref[...] + _dot(vt, p.astype(_BF16))
    return m_new, alpha * l + jnp.sum(p, axis=0, keepdims=True)


def _pipelined_past_tiles(n_past, scores, update, sa_scr, sb_scr, state):
    scores(0, sa_scr)

    def pair(t, st):
        j0 = 2 * t
        j1 = j0 + 1
        scores(j1, sb_scr)
        st = update(j0, sa_scr, st)
        scores(jnp.minimum(j1 + 1, n_past), sa_scr)
        return update(j1, sb_scr, st, valid=j1 < n_past)

    return lax.fori_loop(0, (n_past + 1) // 2, pair, state)


def _head_lanes(g):
    return slice(g * HEAD_DIM, (g + 1) * HEAD_DIM)


def _head_group_spec(rows, tq, width, first_blk, hps):
    assert first_blk % hps == 0
    row_blk = (lambda i: i) if rows == tq else (lambda i: 0)
    return pl.BlockSpec((None, rows, width),
                        lambda bi, hi, i: (bi, row_blk(i), first_blk // hps + hi))


def _diff_attention_kernel(slopes_ref, lq1_ref, lk1_ref, lq2_ref, lk2_ref, subg_ref,
                           q_ref, k_ref, v_ref, gate_ref, o_ref,
                           vt_scr, acc_scr, sa_scr, sb_scr, *, lam_init, tq, heads_per_step):
    i = pl.program_id(2)
    scale = DIFF_HALF ** -0.5
    heads = range(heads_per_step)

    @pl.when(i == 0)
    def _():
        for g in heads:
            for n in range(vt_scr.shape[1]):
                vt_scr[g, n] = v_ref[n * tq:(n + 1) * tq, _head_lanes(g)].T

    lane = lax.broadcasted_iota(jnp.int32, (tq, HEAD_DIM), 1)
    key_idx = lax.broadcasted_iota(jnp.int32, (tq, tq), 0)
    qry_idx = lax.broadcasted_iota(jnp.int32, (tq, tq), 1)
    key_pos = key_idx.astype(_F32)
    qry_pos = lax.broadcasted_iota(jnp.int32, (1, tq), 1).astype(_F32)

    streams, state = [], []
    for g in heads:
        slope = slopes_ref[pl.program_id(1) * heads_per_step + g]
        q = q_ref[:, _head_lanes(g)]
        zero = jnp.zeros_like(q)
        k_d = k_ref[pl.ds(i * tq, tq), _head_lanes(g)]
        for c in range(2):
            half = (lane < DIFF_HALF) if c == 0 else (lane >= DIFF_HALF)
            qc = jnp.where(half, q, zero) * scale
            z = jnp.where(key_idx <= qry_idx, _dot_nt(k_d, qc) + slope * key_pos, _NEG_INF)
            state += list(_first_flash_step(z, -slope * qry_pos, vt_scr[g, i],
                                            acc_scr.at[g, c]))
            streams.append((g, c, slope, qc))

    def scores(j, s_buf):
        for n, (g, _, _, qc) in enumerate(streams):
            s_buf[n] = _dot_nt(k_ref[pl.ds(j * tq, tq), _head_lanes(g)], qc)

    def update(j, s_buf, st, valid=None):
        out = []
        for n, (g, c, slope, _) in enumerate(streams):
            rb = -slope * qry_pos - slope * ((i - j) * tq).astype(_F32)
            if valid is not None:
                rb = jnp.where(valid, rb, _NEG_INF)
            z = s_buf[n] + slope * key_pos
            out += list(_flash_step(z, rb, vt_scr[g, j], st[2 * n], st[2 * n + 1],
                                    acc_scr.at[g, c]))
        return tuple(out)

    st = _pipelined_past_tiles(i, scores, update, sa_scr, sb_scr, tuple(state))

    lam = (jnp.exp(jnp.sum(lq1_ref[...] * lk1_ref[...], axis=-1, keepdims=True))
           - jnp.exp(jnp.sum(lq2_ref[...] * lk2_ref[...], axis=-1, keepdims=True))
           + lam_init)
    for g in heads:
        l1, l2 = st[4 * g + 1], st[4 * g + 3]
        o_t = acc_scr[g, 0] / l1 - lam * (acc_scr[g, 1] / l2)
        ms = jnp.mean(o_t * o_t, axis=0, keepdims=True)
        o = (o_t * lax.rsqrt(ms + EPS)).T * subg_ref[...] * (1.0 - lam_init)
        gate = gate_ref[:, _head_lanes(g)].astype(_F32)
        o_ref[:, _head_lanes(g)] = (o * _silu(gate)).astype(o_ref.dtype)


def diff_attention(proj, slopes, lq1, lk1, lq2, lk2, subg, *, lam_init, tq, heads_per_step):
    b, s, _ = proj.shape
    hps = heads_per_step
    assert s % tq == 0 and DIFF_HEADS % hps == 0
    width = hps * HEAD_DIM
    vec = lambda n: pl.BlockSpec((1, n), lambda bi, hi, i: (0, 0))
    spec = functools.partial(_head_group_spec, tq=tq, width=width, hps=hps)
    scratch = [pltpu.VMEM((hps, s // tq, HEAD_DIM, tq), _BF16),
               pltpu.VMEM((hps, 2, HEAD_DIM, tq), _F32),
               pltpu.VMEM((2 * hps, tq, tq), _F32),
               pltpu.VMEM((2 * hps, tq, tq), _F32)]
    return pl.pallas_call(
        functools.partial(_diff_attention_kernel, lam_init=lam_init, tq=tq,
                          heads_per_step=hps),
        grid=(b, DIFF_HEADS // hps, s // tq),
        in_specs=[
            pl.BlockSpec(memory_space=pltpu.SMEM),
            vec(DIFF_HALF), vec(DIFF_HALF), vec(DIFF_HALF), vec(DIFF_HALF), vec(HEAD_DIM),
            spec(tq, first_blk=_DQ_BLK), spec(s, first_blk=_DK_BLK), spec(s, first_blk=_DV_BLK),
            spec(tq, first_blk=_GATE_BLK),
        ],
        out_specs=pl.BlockSpec((None, tq, width), lambda bi, hi, i: (bi, i, hi)),
        out_shape=jax.ShapeDtypeStruct((b, s, DIFF_WIDTH), _BF16),
        scratch_shapes=scratch,
        compiler_params=pltpu.CompilerParams(
            dimension_semantics=("parallel", "parallel", "arbitrary"),
            vmem_limit_bytes=_VMEM_LIMIT),
        name="diff_attention",
    )(slopes, lq1.reshape(1, -1), lk1.reshape(1, -1), lq2.reshape(1, -1), lk2.reshape(1, -1),
      subg.reshape(1, -1), proj, proj, proj, proj)


def _moba_attention_kernel(slopes_ref, q_ref, k_ref, v_ref, gate_ref, o_ref,
                           kmean_scr, vt_scr, sel_scr, acc_scr, sa_scr, sb_scr,
                           *, n_blocks, heads_per_step):
    i = pl.program_id(2)
    scale = HEAD_DIM ** -0.5
    tq = MOBA_BLOCK
    heads = range(heads_per_step)

    @pl.when(i == 0)
    def _():
        for g in heads:
            for n in range(n_blocks):
                rows = slice(n * MOBA_BLOCK, (n + 1) * MOBA_BLOCK)
                vt_scr[g, n] = v_ref[rows, _head_lanes(g)].T
                kb = k_ref[rows, _head_lanes(g)].astype(_F32)
                kmean_scr[g, n:n + 1, :] = (jnp.sum(kb, axis=0, keepdims=True)
                                            * (1.0 / MOBA_BLOCK))

    key_idx = lax.broadcasted_iota(jnp.int32, (tq, tq), 0)
    qry_idx = lax.broadcasted_iota(jnp.int32, (tq, tq), 1)
    key_pos = key_idx.astype(_F32)
    qry_pos = lax.broadcasted_iota(jnp.int32, (1, tq), 1).astype(_F32)
    blk = lax.broadcasted_iota(jnp.int32, (n_blocks, tq), 0)
    past = blk < i

    slopes, qs, state = [], [], []
    for g in heads:
        slope = slopes_ref[pl.program_id(1) * heads_per_step + g]
        q = q_ref[:, _head_lanes(g)]
        gate = lax.dot_general(kmean_scr[g], q.astype(_F32), (((1,), (1,)), ((), ())),
                               precision=lax.Precision.HIGHEST,
                               preferred_element_type=_F32)
        gate = jnp.where(past, gate, _NEG_INF)
        rank = jnp.zeros(gate.shape, jnp.int32)
        for n in range(n_blocks):
            g_n = gate[n:n + 1, :]
            beats = (g_n > gate) | ((g_n == gate) & (n < blk))
            rank = rank + beats.astype(jnp.int32)
        sel_scr[g] = jnp.where(past & (rank < MOBA_TOPK), 0.0, _NEG_INF)

        k_d = k_ref[pl.ds(i * tq, tq), _head_lanes(g)]
        z = jnp.where(key_idx <= qry_idx,
                      _dot_nt(k_d, q) * scale + slope * key_pos, _NEG_INF)
        state += list(_first_flash_step(z, -slope * qry_pos, vt_scr[g, i], acc_scr.at[g]))
        slopes.append(slope)
        qs.append(q)

    def scores(j, s_buf):
        for g in heads:
            s_buf[g] = _dot_nt(k_ref[pl.ds(j * tq, tq), _head_lanes(g)], qs[g])

    def update(j, s_buf, st, valid=None):
        out = []
        for g in heads:
            slope = slopes[g]
            rb = (-slope * qry_pos - slope * ((i - j) * tq).astype(_F32)
                  + sel_scr[g, pl.ds(j, 1), :])
            z = s_buf[g] * scale + slope * key_pos
            out += list(_flash_step(z, rb, vt_scr[g, j], st[2 * g], st[2 * g + 1],
                                    acc_scr.at[g]))
        return tuple(out)

    st = _pipelined_past_tiles(i, scores, update, sa_scr, sb_scr, tuple(state))
    for g in heads:
        o = (acc_scr[g] / st[2 * g + 1]).T
        gate = gate_ref[:, _head_lanes(g)].astype(_F32)
        o_ref[:, _head_lanes(g)] = (o * _silu(gate)).astype(o_ref.dtype)


def moba_attention(proj, slopes, *, heads_per_step):
    b, s, _ = proj.shape
    hps = heads_per_step
    assert s % MOBA_BLOCK == 0 and MOBA_HEADS % hps == 0
    n_blocks = s // MOBA_BLOCK
    tq = MOBA_BLOCK
    width = hps * HEAD_DIM
    spec = functools.partial(_head_group_spec, tq=tq, width=width, hps=hps)
    return pl.pallas_call(
        functools.partial(_moba_attention_kernel, n_blocks=n_blocks, heads_per_step=hps),
        grid=(b, MOBA_HEADS // hps, n_blocks),
        in_specs=[
            pl.BlockSpec(memory_space=pltpu.SMEM),
            spec(tq, first_blk=_MQ_BLK), spec(s, first_blk=_MK_BLK), spec(s, first_blk=_MV_BLK),
            spec(tq, first_blk=_GATE_BLK + DIFF_HEADS),
        ],
        out_specs=pl.BlockSpec((None, tq, width), lambda bi, hi, i: (bi, i, hi)),
        out_shape=jax.ShapeDtypeStruct((b, s, MOBA_WIDTH), _BF16),
        scratch_shapes=[pltpu.VMEM((hps, n_blocks, HEAD_DIM), _F32),
                        pltpu.VMEM((hps, n_blocks, HEAD_DIM, tq), _BF16),
                        pltpu.VMEM((hps, n_blocks, tq), _F32),
                        pltpu.VMEM((hps, HEAD_DIM, tq), _F32),
                        pltpu.VMEM((hps, tq, tq), _F32),
                        pltpu.VMEM((hps, tq, tq), _F32)],
        compiler_params=pltpu.CompilerParams(
            dimension_semantics=("parallel", "parallel", "arbitrary"),
            vmem_limit_bytes=_VMEM_LIMIT),
        name="moba_attention",
    )(slopes, proj, proj, proj, proj)


def _mem_attention_kernel(q_ref, kv_ref, wo_ref, x_ref, fg_ref, o_ref, *, apply_final):
    scale = HEAD_DIM ** -0.5
    x = x_ref[...]
    for hh in range(MEM_HEADS):
        lo, hi = hh * HEAD_DIM, (hh + 1) * HEAD_DIM
        s = _dot_nt(q_ref[:, lo:hi], kv_ref[:, lo:hi]) * scale
        p = jnp.exp(s - jnp.max(s, axis=-1, keepdims=True))
        l = jnp.sum(p, axis=-1, keepdims=True)
        o_h = _dot(p.astype(_BF16), kv_ref[:, MEM_WIDTH + lo:MEM_WIDTH + hi]) / l
        x = x + _dot(o_h.astype(_BF16), wo_ref[lo:hi, :])
    if apply_final:
        ms = jnp.mean(x * x, axis=-1, keepdims=True)
        x = x * lax.rsqrt(ms + EPS) * fg_ref[...]
    o_ref[...] = x


def mem_attention(q, kv, wo, x, final_g, *, apply_final, tm):
    b, s, d = x.shape
    mlen = kv.shape[1]
    assert s % tm == 0
    return pl.pallas_call(
        functools.partial(_mem_attention_kernel, apply_final=apply_final),
        grid=(b, s // tm),
        in_specs=[
            pl.BlockSpec((None, tm, MEM_WIDTH), lambda bi, i: (bi, i, 0)),
            pl.BlockSpec((None, mlen, 2 * MEM_WIDTH), lambda bi, i: (bi, 0, 0)),
            pl.BlockSpec((MEM_WIDTH, d), lambda bi, i: (0, 0)),
            pl.BlockSpec((None, tm, d), lambda bi, i: (bi, i, 0)),
            pl.BlockSpec((1, d), lambda bi, i: (0, 0)),
        ],
        out_specs=pl.BlockSpec((None, tm, d), lambda bi, i: (bi, i, 0)),
        out_shape=jax.ShapeDtypeStruct((b, s, d), _F32),
        compiler_params=pltpu.CompilerParams(
            dimension_semantics=("parallel", "parallel"),
            vmem_limit_bytes=_VMEM_LIMIT),
        name="mem_attention",
    )(q, kv, wo, x, final_g.reshape(1, d))


def _alibi_slopes(n):
    return jnp.asarray(2.0 ** (-8.0 * np.arange(1, n + 1) / n), dtype=_F32)


_ATTN_HEADS_PER_STEP = 2


def kernel(x, mem, norm_mix_g, w_in, lambda_q1, lambda_k1, lambda_q2, lambda_k2, subln_g,
           w_out, norm_mem_g, mem_norm_g, w_q_mem, w_kv_mem, w_o_mem, final_g):
    b, s, d = x.shape
    mlen = mem.shape[1]
    depth = w_in.shape[0]
    diff_slopes = _alibi_slopes(DIFF_HEADS)
    moba_slopes = _alibi_slopes(MOBA_HEADS)
    mem2 = mem.reshape(b * mlen, d)
    for l in range(depth):
        lam_init = 0.8 - 0.6 * math.exp(-0.3 * l)
        proj = rms_matmul(x.reshape(b * s, d), norm_mix_g[l], w_in[l].astype(_BF16),
                          tm=512, tn=1024).reshape(b, s, -1)
        y_d = diff_attention(proj, diff_slopes, lambda_q1[l], lambda_k1[l], lambda_q2[l],
                             lambda_k2[l], subln_g[l], lam_init=lam_init, tq=256,
                             heads_per_step=_ATTN_HEADS_PER_STEP)
        y_m = moba_attention(proj, moba_slopes, heads_per_step=_ATTN_HEADS_PER_STEP)
        x1 = matmul_residual(y_d.reshape(b * s, -1), y_m.reshape(b * s, -1),
                             w_out[l].astype(_BF16), x.reshape(b * s, d), tm=512, tn=1024)
        kv = rms_matmul(mem2, mem_norm_g[l], w_kv_mem[l].astype(_BF16),
                        tm=b * mlen, tn=2 * MEM_WIDTH).reshape(b, mlen, 2 * MEM_WIDTH)
        q = rms_matmul(x1, norm_mem_g[l], w_q_mem[l].astype(_BF16),
                       tm=512, tn=MEM_WIDTH).reshape(b, s, MEM_WIDTH)
        x = mem_attention(q, kv, w_o_mem[l].astype(_BF16), x1.reshape(b, s, d), final_g,
                          apply_final=(l == depth - 1), tm=512)
    return x
```

```python
import functools
import math

import jax
import jax.numpy as jnp
import numpy as np
from jax import lax
from jax.experimental import pallas as pl
from jax.experimental.pallas import tpu as pltpu

HEAD_DIM = 128
DIFF_HEADS = 8
DIFF_HALF = HEAD_DIM // 2
DIFF_WIDTH = DIFF_HEADS * HEAD_DIM
MOBA_HEADS = 8
MOBA_WIDTH = MOBA_HEADS * HEAD_DIM
MOBA_BLOCK = 256
MOBA_TOPK = 3
MEM_HEADS = 4
MEM_WIDTH = MEM_HEADS * HEAD_DIM
EPS = 1e-6

_DQ_BLK = 0
_DK_BLK = DIFF_HEADS
_DV_BLK = 2 * DIFF_HEADS
_MQ_BLK = 3 * DIFF_HEADS
_MK_BLK = 3 * DIFF_HEADS + MOBA_HEADS
_MV_BLK = 3 * DIFF_HEADS + 2 * MOBA_HEADS
_GATE_BLK = 3 * DIFF_HEADS + 3 * MOBA_HEADS

_V7X_VMEM_BYTES = 64 * 1024 * 1024
_VMEM_LIMIT = _V7X_VMEM_BYTES * 3 // 4

_BF16 = jnp.bfloat16
_F32 = jnp.float32
_NEG_INF = float("-inf")


def _dot_nt(a, b):
    return lax.dot_general(a, b, (((1,), (1,)), ((), ())), preferred_element_type=_F32)


def _dot(a, b):
    return jnp.dot(a, b, preferred_element_type=_F32)


def _silu(g):
    return g * jax.nn.sigmoid(g)


def _rms_matmul_kernel(x_ref, g_ref, w_ref, o_ref, h_scr):
    @pl.when(pl.program_id(1) == 0)
    def _():
        x = x_ref[...]
        ms = jnp.mean(x * x, axis=-1, keepdims=True)
        h_scr[...] = (x * lax.rsqrt(ms + EPS) * g_ref[...]).astype(_BF16)

    o_ref[...] = _dot(h_scr[...], w_ref[...]).astype(o_ref.dtype)


def rms_matmul(x, g, w, *, tm, tn, out_dtype=_BF16):
    m, k = x.shape
    n = w.shape[1]
    assert m % tm == 0 and n % tn == 0
    return pl.pallas_call(
        _rms_matmul_kernel,
        grid=(m // tm, n // tn),
        in_specs=[
            pl.BlockSpec((tm, k), lambda i, j: (i, 0)),
            pl.BlockSpec((1, k), lambda i, j: (0, 0)),
            pl.BlockSpec((k, tn), lambda i, j: (0, j)),
        ],
        out_specs=pl.BlockSpec((tm, tn), lambda i, j: (i, j)),
        out_shape=jax.ShapeDtypeStruct((m, n), out_dtype),
        scratch_shapes=[pltpu.VMEM((tm, k), _BF16)],
        compiler_params=pltpu.CompilerParams(
            dimension_semantics=("parallel", "arbitrary"),
            vmem_limit_bytes=_VMEM_LIMIT),
        name="rms_matmul",
    )(x, g.reshape(1, k), w)


def _matmul_residual_kernel(ya_ref, yb_ref, wa_ref, wb_ref, r_ref, o_ref):
    o_ref[...] = (r_ref[...] + _dot(ya_ref[...], wa_ref[...])
                  + _dot(yb_ref[...], wb_ref[...]))


def matmul_residual(ya, yb, w, res, *, tm, tn):
    m, ka = ya.shape
    kb = yb.shape[1]
    n = w.shape[1]
    assert m % tm == 0 and n % tn == 0 and ka == kb and w.shape[0] == ka + kb
    return pl.pallas_call(
        _matmul_residual_kernel,
        grid=(m // tm, n // tn),
        in_specs=[
            pl.BlockSpec((tm, ka), lambda i, j: (i, 0)),
            pl.BlockSpec((tm, kb), lambda i, j: (i, 0)),
            pl.BlockSpec((ka, tn), lambda i, j: (0, j)),
            pl.BlockSpec((kb, tn), lambda i, j: (1, j)),
            pl.BlockSpec((tm, tn), lambda i, j: (i, j)),
        ],
        out_specs=pl.BlockSpec((tm, tn), lambda i, j: (i, j)),
        out_shape=jax.ShapeDtypeStruct((m, n), _F32),
        compiler_params=pltpu.CompilerParams(
            dimension_semantics=("parallel", "arbitrary"),
            vmem_limit_bytes=_VMEM_LIMIT),
        name="matmul_residual",
    )(ya, yb, w, w, res)


def _first_flash_step(z, rb, vt, acc_ref):
    m = jnp.max(z, axis=0, keepdims=True) + rb
    p = jnp.exp(z - (m - rb))
    acc_ref[...] = _dot(vt, p.astype(_BF16))
    return m, jnp.sum(p, axis=0, keepdims=True)


def _flash_step(z, rb, vt, m, l, acc_ref):
    m_new = jnp.maximum(m, jnp.max(z, axis=0, keepdims=True) + rb)
    p = jnp.exp(z - (m_new - rb))
    alpha = jnp.exp(m - m_new)
    acc_ref[...] = alpha * acc_ref[...] + _dot(vt, p.astype(_BF16))
    return m_new, alpha * l + jnp.sum(p, axis=0, keepdims=True)


def _pipelined_flash(n_past, scores, first_update, update, sa_scr, sb_scr):
    scores(n_past, sb_scr)
    scores(0, sa_scr)
    state = first_update(sb_scr)

    def pair(t, st):
        j0 = 2 * t
        j1 = j0 + 1
        scores(j1, sb_scr)
        st = update(j0, sa_scr, st)
        scores(jnp.minimum(j1 + 1, n_past), sa_scr)
        return update(j1, sb_scr, st, valid=j1 < n_past)

    return lax.fori_loop(0, (n_past + 1) // 2, pair, state)


def _head_lanes(g):
    return slice(g * HEAD_DIM, (g + 1) * HEAD_DIM)


def _head_group_spec(rows, tq, width, first_blk, hps):
    assert first_blk % hps == 0
    row_blk = (lambda i: i) if rows == tq else (lambda i: 0)
    return pl.BlockSpec((None, rows, width),
                        lambda bi, hi, i: (bi, row_blk(i), first_blk // hps + hi))


def _diff_attention_kernel(slopes_ref, lq1_ref, lk1_ref, lq2_ref, lk2_ref, subg_ref,
                           q_ref, k_ref, v_ref, gate_ref, o_ref,
                           vt_scr, acc_scr, sa_scr, sb_scr, *, lam_init, tq, heads_per_step):
    i = pl.program_id(2)
    scale = DIFF_HALF ** -0.5
    heads = range(heads_per_step)

    @pl.when(i == 0)
    def _():
        for g in heads:
            for n in range(vt_scr.shape[1]):
                vt_scr[g, n] = v_ref[n * tq:(n + 1) * tq, _head_lanes(g)].T

    lane = lax.broadcasted_iota(jnp.int32, (tq, HEAD_DIM), 1)
    key_idx = lax.broadcasted_iota(jnp.int32, (tq, tq), 0)
    qry_idx = lax.broadcasted_iota(jnp.int32, (tq, tq), 1)
    key_pos = key_idx.astype(_F32)
    qry_pos = lax.broadcasted_iota(jnp.int32, (1, tq), 1).astype(_F32)

    streams = []
    for g in heads:
        slope = slopes_ref[pl.program_id(1) * heads_per_step + g]
        q = q_ref[:, _head_lanes(g)]
        zero = jnp.zeros_like(q)
        for c in range(2):
            half = (lane < DIFF_HALF) if c == 0 else (lane >= DIFF_HALF)
            streams.append((g, c, slope, jnp.where(half, q, zero) * scale))

    def scores(j, s_buf):
        for n, (g, _, _, qc) in enumerate(streams):
            s_buf[n] = _dot_nt(k_ref[pl.ds(j * tq, tq), _head_lanes(g)], qc)

    def first_update(s_buf):
        out = []
        for n, (g, c, slope, _) in enumerate(streams):
            z = jnp.where(key_idx <= qry_idx, s_buf[n] + slope * key_pos, _NEG_INF)
            out += list(_first_flash_step(z, -slope * qry_pos, vt_scr[g, i],
                                          acc_scr.at[g, c]))
        return tuple(out)

    def update(j, s_buf, st, valid=None):
        out = []
        for n, (g, c, slope, _) in enumerate(streams):
            rb = -slope * qry_pos - slope * ((i - j) * tq).astype(_F32)
            if valid is not None:
                rb = jnp.where(valid, rb, _NEG_INF)
            z = s_buf[n] + slope * key_pos
            out += list(_flash_step(z, rb, vt_scr[g, j], st[2 * n], st[2 * n + 1],
                                    acc_scr.at[g, c]))
        return tuple(out)

    st = _pipelined_flash(i, scores, first_update, update, sa_scr, sb_scr)

    lam =(jnp.exp(jnp.sum(lq1_ref[...] * lk1_ref[...], axis=-1, keepdims=True))
           - jnp.exp(jnp.sum(lq2_ref[...] * lk2_ref[...], axis=-1, keepdims=True))
           + lam_init)
    for g in heads:
        l1, l2 = st[4 * g + 1], st[4 * g + 3]
        o_t = acc_scr[g, 0] / l1 - lam * (acc_scr[g, 1] / l2)
        ms = jnp.mean(o_t * o_t, axis=0, keepdims=True)
        o = (o_t * lax.rsqrt(ms + EPS)).T * subg_ref[...] * (1.0 - lam_init)
        gate = gate_ref[:, _head_lanes(g)].astype(_F32)
        o_ref[:, _head_lanes(g)] = (o * _silu(gate)).astype(o_ref.dtype)


def diff_attention(proj, slopes, lq1, lk1, lq2, lk2, subg, *, lam_init, tq, heads_per_step):
    b, s, _ = proj.shape
    hps = heads_per_step
    assert s % tq == 0 and DIFF_HEADS % hps == 0
    width = hps * HEAD_DIM
    vec = lambda n: pl.BlockSpec((1, n), lambda bi, hi, i: (0, 0))
    spec = functools.partial(_head_group_spec, tq=tq, width=width, hps=hps)
    scratch = [pltpu.VMEM((hps, s // tq, HEAD_DIM, tq), _BF16),
               pltpu.VMEM((hps, 2, HEAD_DIM, tq), _F32),
               pltpu.VMEM((2 * hps, tq, tq), _F32),
               pltpu.VMEM((2 * hps, tq, tq), _F32)]
    return pl.pallas_call(
        functools.partial(_diff_attention_kernel, lam_init=lam_init, tq=tq,
                          heads_per_step=hps),
        grid=(b, DIFF_HEADS // hps, s // tq),
        in_specs=[
            pl.BlockSpec(memory_space=pltpu.SMEM),
            vec(DIFF_HALF), vec(DIFF_HALF), vec(DIFF_HALF), vec(DIFF_HALF), vec(HEAD_DIM),
            spec(tq, first_blk=_DQ_BLK), spec(s, first_blk=_DK_BLK), spec(s, first_blk=_DV_BLK),
            spec(tq, first_blk=_GATE_BLK),
        ],
        out_specs=pl.BlockSpec((None, tq, width), lambda bi, hi, i: (bi, i, hi)),
        out_shape=jax.ShapeDtypeStruct((b, s, DIFF_WIDTH), _BF16),
        scratch_shapes=scratch,
        compiler_params=pltpu.CompilerParams(
            dimension_semantics=("parallel", "parallel", "arbitrary"),
            vmem_limit_bytes=_VMEM_LIMIT),
        name="diff_attention",
    )(slopes, lq1.reshape(1, -1), lk1.reshape(1, -1), lq2.reshape(1, -1), lk2.reshape(1, -1),
      subg.reshape(1, -1), proj, proj, proj, proj)


def _moba_attention_kernel(slopes_ref, q_ref, k_ref, v_ref, gate_ref, o_ref,
                           kmean_scr, vt_scr, sel_scr, acc_scr, sa_scr, sb_scr,
                           *, n_blocks, heads_per_step):
    i = pl.program_id(2)
    scale = HEAD_DIM ** -0.5
    tq = MOBA_BLOCK
    heads = range(heads_per_step)

    @pl.when(i == 0)
    def _():
        for g in heads:
            for n in range(n_blocks):
                rows = slice(n * MOBA_BLOCK, (n + 1) * MOBA_BLOCK)
                vt_scr[g, n] = v_ref[rows, _head_lanes(g)].T
                kb = k_ref[rows, _head_lanes(g)].astype(_F32)
                kmean_scr[g, n:n + 1, :] = (jnp.sum(kb, axis=0, keepdims=True)
                                            * (1.0 / MOBA_BLOCK))

    key_idx = lax.broadcasted_iota(jnp.int32, (tq, tq), 0)
    qry_idx = lax.broadcasted_iota(jnp.int32, (tq, tq), 1)
    key_pos = key_idx.astype(_F32)
    qry_pos = lax.broadcasted_iota(jnp.int32, (1, tq), 1).astype(_F32)
    blk = lax.broadcasted_iota(jnp.int32, (n_blocks, tq), 0)
    past = blk < i

    slopes = [slopes_ref[pl.program_id(1) * heads_per_step + g] for g in heads]
    qs = [q_ref[:, _head_lanes(g)] for g in heads]

    def scores(j, s_buf):
        for g in heads:
            s_buf[g] = _dot_nt(k_ref[pl.ds(j * tq, tq), _head_lanes(g)], qs[g])

    def select_blocks(g):
        gate = lax.dot_general(kmean_scr[g], qs[g].astype(_F32), (((1,), (1,)), ((), ())),
                               precision=lax.Precision.HIGHEST,
                               preferred_element_type=_F32)
        gate = jnp.where(past, gate, _NEG_INF)
        rank = jnp.zeros(gate.shape, jnp.int32)
        for n in range(n_blocks):
            g_n = gate[n:n + 1, :]
            beats = (g_n > gate) | ((g_n == gate) & (n < blk))
            rank = rank + beats.astype(jnp.int32)
        sel_scr[g] = jnp.where(past & (rank < MOBA_TOPK), 0.0, _NEG_INF)

    def first_update(s_buf):
        out = []
        for g in heads:
            select_blocks(g)
            z = jnp.where(key_idx <= qry_idx, s_buf[g] * scale + slopes[g] * key_pos, _NEG_INF)
            out += list(_first_flash_step(z, -slopes[g] * qry_pos, vt_scr[g, i],
                                          acc_scr.at[g]))
        return tuple(out)

    def update(j, s_buf, st, valid=None):
        out = []
        for g in heads:
            slope = slopes[g]
            rb = (-slope * qry_pos - slope * ((i - j) * tq).astype(_F32)
                  + sel_scr[g, pl.ds(j, 1), :])
            z = s_buf[g] * scale + slope * key_pos
            out += list(_flash_step(z, rb, vt_scr[g, j], st[2 * g], st[2 * g + 1],
                                    acc_scr.at[g]))
        return tuple(out)

    st = _pipelined_flash(i, scores, first_update, update, sa_scr, sb_scr)
    for g in heads:
        o = (acc_scr[g] / st[2 * g + 1]).T
        gate = gate_ref[:, _head_lanes(g)].astype(_F32)
        o_ref[:, _head_lanes(g)] = (o * _silu(gate)).astype(o_ref.dtype)


def moba_attention(proj, slopes, *, heads_per_step):
    b, s, _ = proj.shape
    hps = heads_per_step
    assert s % MOBA_BLOCK == 0 and MOBA_HEADS % hps == 0
    n_blocks = s // MOBA_BLOCK
    tq = MOBA_BLOCK
    width = hps * HEAD_DIM
    spec = functools.partial(_head_group_spec, tq=tq, width=width, hps=hps)
    return pl.pallas_call(
        functools.partial(_moba_attention_kernel, n_blocks=n_blocks, heads_per_step=hps),
        grid=(b, MOBA_HEADS // hps, n_blocks),
        in_specs=[
            pl.BlockSpec(memory_space=pltpu.SMEM),
            spec(tq, first_blk=_MQ_BLK), spec(s, first_blk=_MK_BLK), spec(s, first_blk=_MV_BLK),
            spec(tq, first_blk=_GATE_BLK + DIFF_HEADS),
        ],
        out_specs=pl.BlockSpec((None, tq, width), lambda bi, hi, i: (bi, i, hi)),
        out_shape=jax.ShapeDtypeStruct((b, s, MOBA_WIDTH), _BF16),
        scratch_shapes=[pltpu.VMEM((hps, n_blocks, HEAD_DIM), _F32),
                        pltpu.VMEM((hps, n_blocks, HEAD_DIM, tq), _BF16),
                        pltpu.VMEM((hps, n_blocks, tq), _F32),
                        pltpu.VMEM((hps, HEAD_DIM, tq), _F32),
                        pltpu.VMEM((hps, tq, tq), _F32),
                        pltpu.VMEM((hps, tq, tq), _F32)],
        compiler_params=pltpu.CompilerParams(
            dimension_semantics=("parallel", "parallel", "arbitrary"),
            vmem_limit_bytes=_VMEM_LIMIT),
        name="moba_attention",
    )(slopes, proj, proj, proj, proj)


def _mem_attention_kernel(q_ref, kv_ref, wo_ref, x_ref, fg_ref, o_ref, *, apply_final):
    scale = HEAD_DIM ** -0.5
    x = x_ref[...]
    for hh in range(MEM_HEADS):
        lo, hi = hh * HEAD_DIM, (hh + 1) * HEAD_DIM
        s = _dot_nt(q_ref[:, lo:hi], kv_ref[:, lo:hi]) * scale
        p = jnp.exp(s - jnp.max(s, axis=-1, keepdims=True))
        l = jnp.sum(p, axis=-1, keepdims=True)
        o_h = _dot(p.astype(_BF16), kv_ref[:, MEM_WIDTH + lo:MEM_WIDTH + hi]) / l
        x = x + _dot(o_h.astype(_BF16), wo_ref[lo:hi, :])
    if apply_final:
        ms = jnp.mean(x * x, axis=-1, keepdims=True)
        x = x * lax.rsqrt(ms + EPS) * fg_ref[...]
    o_ref[...] = x


def mem_attention(q, kv, wo, x, final_g, *, apply_final, tm):
    b, s, d = x.shape
    mlen = kv.shape[1]
    assert s % tm == 0
    return pl.pallas_call(
        functools.partial(_mem_attention_kernel, apply_final=apply_final),
        grid=(b, s // tm),
        in_specs=[
            pl.BlockSpec((None, tm, MEM_WIDTH), lambda bi, i: (bi, i, 0)),
            pl.BlockSpec((None, mlen, 2 * MEM_WIDTH), lambda bi, i: (bi, 0, 0)),
            pl.BlockSpec((MEM_WIDTH, d), lambda bi, i: (0, 0)),
            pl.BlockSpec((None, tm, d), lambda bi, i: (bi, i, 0)),
            pl.BlockSpec((1, d), lambda bi, i: (0, 0)),
        ],
        out_specs=pl.BlockSpec((None, tm, d), lambda bi, i: (bi, i, 0)),
        out_shape=jax.ShapeDtypeStruct((b, s, d), _F32),
        compiler_params=pltpu.CompilerParams(
            dimension_semantics=("parallel", "parallel"),
            vmem_limit_bytes=_VMEM_LIMIT),
        name="mem_attention",
    )(q, kv, wo, x, final_g.reshape(1, d))


def _alibi_slopes(n):
    return jnp.asarray(2.0 ** (-8.0 * np.arange(1, n + 1) / n), dtype=_F32)


_ATTN_HEADS_PER_STEP = 2


def kernel(x, mem, norm_mix_g, w_in, lambda_q1, lambda_k1, lambda_q2, lambda_k2, subln_g,
           w_out, norm_mem_g, mem_norm_g, w_q_mem, w_kv_mem, w_o_mem, final_g):
    b, s, d = x.shape
    mlen = mem.shape[1]
    depth = w_in.shape[0]
    diff_slopes = _alibi_slopes(DIFF_HEADS)
    moba_slopes = _alibi_slopes(MOBA_HEADS)
    mem2 = mem.reshape(b * mlen, d)
    for l in range(depth):
        lam_init = 0.8 - 0.6 * math.exp(-0.3 * l)
        proj = rms_matmul(x.reshape(b * s, d), norm_mix_g[l], w_in[l].astype(_BF16),
                          tm=512, tn=1024).reshape(b, s, -1)
        y_d = diff_attention(proj, diff_slopes, lambda_q1[l], lambda_k1[l], lambda_q2[l],
                             lambda_k2[l], subln_g[l], lam_init=lam_init, tq=256,
                             heads_per_step=_ATTN_HEADS_PER_STEP)
        y_m = moba_attention(proj, moba_slopes, heads_per_step=_ATTN_HEADS_PER_STEP)
        x1 = matmul_residual(y_d.reshape(b * s, -1), y_m.reshape(b * s, -1),
                             w_out[l].astype(_BF16), x.reshape(b * s, d), tm=512, tn=1024)
        kv = rms_matmul(mem2, mem_norm_g[l], w_kv_mem[l].astype(_BF16),
                        tm=b * mlen, tn=2 * MEM_WIDTH).reshape(b, mlen, 2 * MEM_WIDTH)
        q = rms_matmul(x1, norm_mem_g[l], w_q_mem[l].astype(_BF16),
                       tm=512, tn=MEM_WIDTH).reshape(b, s, MEM_WIDTH)
        x = mem_attention(q, kv, w_o_mem[l].astype(_BF16), x1.reshape(b, s, d), final_g,
                          apply_final=(l == depth - 1), tm=512)
    return x
```

```python
import functools
import math

import jax
import jax.numpy as jnp
import numpy as np
from jax import lax
from jax.experimental import pallas as pl
from jax.experimental.pallas import tpu as pltpu

HEAD_DIM = 128
DIFF_HEADS = 8
DIFF_HALF = HEAD_DIM // 2
DIFF_WIDTH = DIFF_HEADS * HEAD_DIM
MOBA_HEADS = 8
MOBA_WIDTH = MOBA_HEADS * HEAD_DIM
MOBA_BLOCK = 256
MOBA_TOPK = 3
MEM_HEADS = 4
MEM_WIDTH = MEM_HEADS * HEAD_DIM
EPS = 1e-6

_DQ_BLK = 0
_DK_BLK = DIFF_HEADS
_DV_BLK = 2 * DIFF_HEADS
_MQ_BLK = 3 * DIFF_HEADS
_MK_BLK = 3 * DIFF_HEADS + MOBA_HEADS
_MV_BLK = 3 * DIFF_HEADS + 2 * MOBA_HEADS
_GATE_BLK = 3 * DIFF_HEADS + 3 * MOBA_HEADS

_V7X_VMEM_BYTES = 64 * 1024 * 1024
_VMEM_LIMIT = _V7X_VMEM_BYTES * 3 // 4

_BF16 = jnp.bfloat16
_F32 = jnp.float32
_NEG_INF = float("-inf")


def _dot_nt(a, b):
    return lax.dot_general(a, b, (((1,), (1,)), ((), ())), preferred_element_type=_F32)


def _dot(a, b):
    return jnp.dot(a, b, preferred_element_type=_F32)


def _silu(g):
    return g * jax.nn.sigmoid(g)


def _rms_matmul_kernel(x_ref, g_ref, w_ref, *rest, has_col_scale):
    cs_ref, o_ref, h_scr = rest if has_col_scale else (None,) + rest

    @pl.when(pl.program_id(1) == 0)
    def _():
        x = x_ref[...]
        ms = jnp.mean(x * x, axis=-1, keepdims=True)
        h_scr[...] = (x * lax.rsqrt(ms + EPS) * g_ref[...]).astype(_BF16)

    out = _dot(h_scr[...], w_ref[...])
    if has_col_scale:
        out = out * cs_ref[...]
    o_ref[...] = out.astype(o_ref.dtype)


def rms_matmul(x, g, w, col_scale=None, *, tm, tn, out_dtype=_BF16):
    m, k = x.shape
    n = w.shape[1]
    assert m % tm == 0 and n % tn == 0
    has_cs = col_scale is not None
    cs_specs = [pl.BlockSpec((1, tn), lambda i, j: (0, j))] if has_cs else []
    cs_args = [col_scale.reshape(1, n)] if has_cs else []
    return pl.pallas_call(
        functools.partial(_rms_matmul_kernel, has_col_scale=has_cs),
        grid=(m // tm, n // tn),
        in_specs=[
            pl.BlockSpec((tm, k), lambda i, j: (i, 0)),
            pl.BlockSpec((1, k), lambda i, j: (0, 0)),
            pl.BlockSpec((k, tn), lambda i, j: (0, j)),
        ] + cs_specs,
        out_specs=pl.BlockSpec((tm, tn), lambda i, j: (i, j)),
        out_shape=jax.ShapeDtypeStruct((m, n), out_dtype),
        scratch_shapes=[pltpu.VMEM((tm, k), _BF16)],
        compiler_params=pltpu.CompilerParams(
            dimension_semantics=("parallel", "arbitrary"),
            vmem_limit_bytes=_VMEM_LIMIT),
        name="rms_matmul",
    )(x, g.reshape(1, k), w, *cs_args)


def _matmul_residual_kernel(ya_ref, yb_ref, wa_ref, wb_ref, r_ref, o_ref):
    o_ref[...] = (r_ref[...] + _dot(ya_ref[...], wa_ref[...])
                  + _dot(yb_ref[...], wb_ref[...]))


def matmul_residual(ya, yb, w, res, *, tm, tn):
    m, ka = ya.shape
    kb = yb.shape[1]
    n = w.shape[1]
    assert m % tm == 0 and n % tn == 0 and ka == kb and w.shape[0] == ka + kb
    return pl.pallas_call(
        _matmul_residual_kernel,
        grid=(m // tm, n // tn),
        in_specs=[
            pl.BlockSpec((tm, ka), lambda i, j: (i, 0)),
            pl.BlockSpec((tm, kb), lambda i, j: (i, 0)),
            pl.BlockSpec((ka, tn), lambda i, j: (0, j)),
            pl.BlockSpec((kb, tn), lambda i, j: (1, j)),
            pl.BlockSpec((tm, tn), lambda i, j: (i, j)),
        ],
        out_specs=pl.BlockSpec((tm, tn), lambda i, j: (i, j)),
        out_shape=jax.ShapeDtypeStruct((m, n), _F32),
        compiler_params=pltpu.CompilerParams(
            dimension_semantics=("parallel", "arbitrary"),
            vmem_limit_bytes=_VMEM_LIMIT),
        name="matmul_residual",
    )(ya, yb, w, w, res)


_LOG2E = math.log2(math.e)
_ONES_ROWS = 16
_VT_ROWS = HEAD_DIM + _ONES_ROWS


def _first_flash_step(z, rb, vt, acc_ref):
    m = jnp.max(z, axis=0, keepdims=True) + rb
    p = jnp.exp2(z - (m - rb))
    acc_ref[...] = _dot(vt, p.astype(_BF16))
    return m


def _flash_step(z, rb, vt, m, acc_ref):
    m_new = jnp.maximum(m, jnp.max(z, axis=0, keepdims=True) + rb)
    p = jnp.exp2(z - (m_new - rb))
    acc_ref[...] = jnp.exp2(m - m_new) * acc_ref[...] + _dot(vt, p.astype(_BF16))
    return m_new


def _fill_values_transposed(v_ref, vt_scr, g, tk):
    for n in range(vt_scr.shape[1]):
        vt_scr[g, n, :HEAD_DIM, :] = v_ref[n * tk:(n + 1) * tk, _head_lanes(g)].T
        vt_scr[g, n, HEAD_DIM:, :] = jnp.ones((_ONES_ROWS, tk), _BF16)


def _normalised(acc):
    return acc[:HEAD_DIM] / acc[HEAD_DIM:HEAD_DIM + 1]


def _pipelined_flash(n_past, scores, first_update, update, sa_scr, sb_scr):
    scores(n_past, sb_scr)
    scores(0, sa_scr)
    state = first_update(sb_scr)

    def pair(t, st):
        j0 = 2 * t
        j1 = j0 + 1
        scores(j1, sb_scr)
        st = update(j0, sa_scr, st)
        scores(jnp.minimum(j1 + 1, n_past), sa_scr)
        return update(j1, sb_scr, st, valid=j1 < n_past)

    return lax.fori_loop(0, (n_past + 1) // 2, pair, state)


def _head_lanes(g):
    return slice(g * HEAD_DIM, (g + 1) * HEAD_DIM)


def _head_group_spec(rows, tq, width, first_blk, hps):
    assert first_blk % hps == 0
    row_blk = (lambda i: i) if rows == tq else (lambda i: 0)
    return pl.BlockSpec((None, rows, width),
                        lambda bi, hi, i: (bi, row_blk(i), first_blk // hps + hi))


def _diff_attention_kernel(slopes_ref, lq1_ref, lk1_ref, lq2_ref, lk2_ref, subg_ref,
                           q_ref, k_ref, v_ref, gate_ref, o_ref,
                           vt_scr, acc_scr, sa_scr, sb_scr, *, lam_init, tq, heads_per_step):
    i = pl.program_id(2)
    heads = range(heads_per_step)

    @pl.when(i == 0)
    def _():
        for g in heads:
            _fill_values_transposed(v_ref, vt_scr, g, tq)

    lane = lax.broadcasted_iota(jnp.int32, (tq, HEAD_DIM), 1)
    key_idx = lax.broadcasted_iota(jnp.int32, (tq, tq), 0)
    qry_idx = lax.broadcasted_iota(jnp.int32, (tq, tq), 1)
    key_pos = key_idx.astype(_F32)
    qry_pos = lax.broadcasted_iota(jnp.int32, (1, tq), 1).astype(_F32)

    streams = []
    for g in heads:
        slope = slopes_ref[pl.program_id(1) * heads_per_step + g] * _LOG2E
        q = q_ref[:, _head_lanes(g)]
        zero = jnp.zeros_like(q)
        for c in range(2):
            half = (lane < DIFF_HALF) if c == 0 else (lane >= DIFF_HALF)
            streams.append((g, c, slope, jnp.where(half, q, zero)))

    def scores(j, s_buf):
        for n, (g, _, _, qc) in enumerate(streams):
            s_buf[n] = _dot_nt(k_ref[pl.ds(j * tq, tq), _head_lanes(g)], qc)

    def first_update(s_buf):
        out = []
        for n, (g, c, slope, _) in enumerate(streams):
            z = jnp.where(key_idx <= qry_idx, s_buf[n] + slope * key_pos, _NEG_INF)
            out.append(_first_flash_step(z, -slope * qry_pos, vt_scr[g, i],
                                         acc_scr.at[g, c]))
        return tuple(out)

    def update(j, s_buf, st, valid=None):
        out = []
        for n, (g, c, slope, _) in enumerate(streams):
            rb = -slope * qry_pos - slope * ((i - j) * tq).astype(_F32)
            if valid is not None:
                rb = jnp.where(valid, rb, _NEG_INF)
            z = s_buf[n] + slope * key_pos
            out.append(_flash_step(z, rb, vt_scr[g, j], st[n], acc_scr.at[g, c]))
        return tuple(out)

    _pipelined_flash(i, scores, first_update, update, sa_scr, sb_scr)

    lam = (jnp.exp(jnp.sum(lq1_ref[...] * lk1_ref[...], axis=-1, keepdims=True))
           - jnp.exp(jnp.sum(lq2_ref[...] * lk2_ref[...], axis=-1, keepdims=True))
           + lam_init)
    for g in heads:
        o_t = _normalised(acc_scr[g, 0]) - lam * _normalised(acc_scr[g, 1])
        ms = jnp.mean(o_t * o_t, axis=0, keepdims=True)
        o = (o_t * lax.rsqrt(ms + EPS)).T * subg_ref[...] * (1.0 - lam_init)
        gate = gate_ref[:, _head_lanes(g)].astype(_F32)
        o_ref[:, _head_lanes(g)] = (o * _silu(gate)).astype(o_ref.dtype)


def diff_attention(proj, slopes, lq1, lk1, lq2, lk2, subg, *, lam_init, tq, heads_per_step):
    b, s, _ = proj.shape
    hps = heads_per_step
    assert s % tq == 0 and DIFF_HEADS % hps == 0
    width = hps * HEAD_DIM
    vec = lambda n: pl.BlockSpec((1, n), lambda bi, hi, i: (0, 0))
    spec = functools.partial(_head_group_spec, tq=tq, width=width, hps=hps)
    scratch = [pltpu.VMEM((hps, s // tq, _VT_ROWS, tq), _BF16),
               pltpu.VMEM((hps, 2, _VT_ROWS, tq), _F32),
               pltpu.VMEM((2 * hps, tq, tq), _F32),
               pltpu.VMEM((2 * hps, tq, tq), _F32)]
    return pl.pallas_call(
        functools.partial(_diff_attention_kernel, lam_init=lam_init, tq=tq,
                          heads_per_step=hps),
        grid=(b, DIFF_HEADS // hps, s // tq),
        in_specs=[
            pl.BlockSpec(memory_space=pltpu.SMEM),
            vec(DIFF_HALF), vec(DIFF_HALF), vec(DIFF_HALF), vec(DIFF_HALF), vec(HEAD_DIM),
            spec(tq, first_blk=_DQ_BLK), spec(s, first_blk=_DK_BLK), spec(s, first_blk=_DV_BLK),
            spec(tq, first_blk=_GATE_BLK),
        ],
        out_specs=pl.BlockSpec((None, tq, width), lambda bi, hi, i: (bi, i, hi)),
        out_shape=jax.ShapeDtypeStruct((b, s, DIFF_WIDTH), _BF16),
        scratch_shapes=scratch,
        compiler_params=pltpu.CompilerParams(
            dimension_semantics=("parallel", "parallel", "arbitrary"),
            vmem_limit_bytes=_VMEM_LIMIT),
        name="diff_attention",
    )(slopes, lq1.reshape(1, -1), lk1.reshape(1, -1), lq2.reshape(1, -1), lk2.reshape(1, -1),
      subg.reshape(1, -1), proj, proj, proj, proj)


def _moba_attention_kernel(slopes_ref, q_ref, k_ref, v_ref, gate_ref, o_ref,
                           kmean_scr, vt_scr, sel_scr, acc_scr, sa_scr, sb_scr,
                           *, n_blocks, heads_per_step):
    i = pl.program_id(2)
    tq = MOBA_BLOCK
    heads = range(heads_per_step)

    @pl.when(i == 0)
    def _():
        for g in heads:
            _fill_values_transposed(v_ref, vt_scr, g, tq)
            for n in range(n_blocks):
                rows = slice(n * MOBA_BLOCK, (n + 1) * MOBA_BLOCK)
                kb = k_ref[rows, _head_lanes(g)].astype(_F32)
                kmean_scr[g, n:n + 1, :] = (jnp.sum(kb, axis=0, keepdims=True)
                                            * (1.0 / MOBA_BLOCK))

    key_idx = lax.broadcasted_iota(jnp.int32, (tq, tq), 0)
    qry_idx = lax.broadcasted_iota(jnp.int32, (tq, tq), 1)
    key_pos = key_idx.astype(_F32)
    qry_pos = lax.broadcasted_iota(jnp.int32, (1, tq), 1).astype(_F32)
    blk = lax.broadcasted_iota(jnp.int32, (n_blocks, tq), 0)
    past = blk < i

    slopes = [slopes_ref[pl.program_id(1) * heads_per_step + g] * _LOG2E for g in heads]
    qs = [q_ref[:, _head_lanes(g)] for g in heads]

    def scores(j, s_buf):
        for g in heads:
            s_buf[g] = _dot_nt(k_ref[pl.ds(j * tq, tq), _head_lanes(g)], qs[g])

    def select_blocks(g):
        gate = lax.dot_general(kmean_scr[g], qs[g].astype(_F32), (((1,), (1,)), ((), ())),
                               precision=lax.Precision.HIGHEST,
                               preferred_element_type=_F32)
        gate = jnp.where(past, gate, _NEG_INF)
        rank = jnp.zeros(gate.shape, jnp.int32)
        for n in range(n_blocks):
            g_n = gate[n:n + 1, :]
            beats = (g_n > gate) | ((g_n == gate) & (n < blk))
            rank = rank + beats.astype(jnp.int32)
        sel_scr[g] = jnp.where(past & (rank < MOBA_TOPK), 0.0, _NEG_INF)

    def first_update(s_buf):
        out = []
        for g in heads:
            select_blocks(g)
            z = jnp.where(key_idx <= qry_idx, s_buf[g] + slopes[g] * key_pos, _NEG_INF)
            out.append(_first_flash_step(z, -slopes[g] * qry_pos, vt_scr[g, i], acc_scr.at[g]))
        return tuple(out)

    def update(j, s_buf, st, valid=None):
        out = []
        for g in heads:
            slope = slopes[g]
            rb = (-slope * qry_pos - slope * ((i - j) * tq).astype(_F32)
                  + sel_scr[g, pl.ds(j, 1), :])
            z = s_buf[g] + slope * key_pos
            out.append(_flash_step(z, rb, vt_scr[g, j], st[g], acc_scr.at[g]))
        return tuple(out)

    _pipelined_flash(i, scores, first_update, update, sa_scr, sb_scr)
    for g in heads:
        o = _normalised(acc_scr[g]).T
        gate = gate_ref[:, _head_lanes(g)].astype(_F32)
        o_ref[:, _head_lanes(g)] = (o * _silu(gate)).astype(o_ref.dtype)


def moba_attention(proj, slopes, *, heads_per_step):
    b, s, _ = proj.shape
    hps = heads_per_step
    assert s % MOBA_BLOCK == 0 and MOBA_HEADS % hps == 0
    n_blocks = s // MOBA_BLOCK
    tq = MOBA_BLOCK
    width = hps * HEAD_DIM
    spec = functools.partial(_head_group_spec, tq=tq, width=width, hps=hps)
    return pl.pallas_call(
        functools.partial(_moba_attention_kernel, n_blocks=n_blocks, heads_per_step=hps),
        grid=(b, MOBA_HEADS // hps, n_blocks),
        in_specs=[
            pl.BlockSpec(memory_space=pltpu.SMEM),
            spec(tq, first_blk=_MQ_BLK), spec(s, first_blk=_MK_BLK), spec(s, first_blk=_MV_BLK),
            spec(tq, first_blk=_GATE_BLK + DIFF_HEADS),
        ],
        out_specs=pl.BlockSpec((None, tq, width), lambda bi, hi, i: (bi, i, hi)),
        out_shape=jax.ShapeDtypeStruct((b, s, MOBA_WIDTH), _BF16),
        scratch_shapes=[pltpu.VMEM((hps, n_blocks, HEAD_DIM), _F32),
                        pltpu.VMEM((hps, n_blocks, _VT_ROWS, tq), _BF16),
                        pltpu.VMEM((hps, n_blocks, tq), _F32),
                        pltpu.VMEM((hps, _VT_ROWS, tq), _F32),
                        pltpu.VMEM((hps, tq, tq), _F32),
                        pltpu.VMEM((hps, tq, tq), _F32)],
        compiler_params=pltpu.CompilerParams(
            dimension_semantics=("parallel", "parallel", "arbitrary"),
            vmem_limit_bytes=_VMEM_LIMIT),
        name="moba_attention",
    )(slopes, proj, proj, proj, proj)


def _mem_attention_kernel(q_ref, kv_ref, wo_ref, x_ref, fg_ref, o_ref, *, apply_final):
    scale = HEAD_DIM ** -0.5
    x = x_ref[...]
    for hh in range(MEM_HEADS):
        lo, hi = hh * HEAD_DIM, (hh + 1) * HEAD_DIM
        s = _dot_nt(q_ref[:, lo:hi], kv_ref[:, lo:hi]) * scale
        p = jnp.exp(s - jnp.max(s, axis=-1, keepdims=True))
        l = jnp.sum(p, axis=-1, keepdims=True)
        o_h = _dot(p.astype(_BF16), kv_ref[:, MEM_WIDTH + lo:MEM_WIDTH + hi]) / l
        x = x + _dot(o_h.astype(_BF16), wo_ref[lo:hi, :])
    if apply_final:
        ms = jnp.mean(x * x, axis=-1, keepdims=True)
        x = x * lax.rsqrt(ms + EPS) * fg_ref[...]
    o_ref[...] = x


def mem_attention(q, kv, wo, x, final_g, *, apply_final, tm):
    b, s, d = x.shape
    mlen = kv.shape[1]
    assert s % tm == 0
    return pl.pallas_call(
        functools.partial(_mem_attention_kernel, apply_final=apply_final),
        grid=(b, s // tm),
        in_specs=[
            pl.BlockSpec((None, tm, MEM_WIDTH), lambda bi, i: (bi, i, 0)),
            pl.BlockSpec((None, mlen, 2 * MEM_WIDTH), lambda bi, i: (bi, 0, 0)),
            pl.BlockSpec((MEM_WIDTH, d), lambda bi, i: (0, 0)),
            pl.BlockSpec((None, tm, d), lambda bi, i: (bi, i, 0)),
            pl.BlockSpec((1, d), lambda bi, i: (0, 0)),
        ],
        out_specs=pl.BlockSpec((None, tm, d), lambda bi, i: (bi, i, 0)),
        out_shape=jax.ShapeDtypeStruct((b, s, d), _F32),
        compiler_params=pltpu.CompilerParams(
            dimension_semantics=("parallel", "parallel"),
            vmem_limit_bytes=_VMEM_LIMIT),
        name="mem_attention",
    )(q, kv, wo, x, final_g.reshape(1, d))


def _in_proj_col_scale(n_cols):
    cs = np.ones((n_cols,), np.float32)
    cs[_DQ_BLK * HEAD_DIM:_DK_BLK * HEAD_DIM] = DIFF_HALF ** -0.5 * _LOG2E
    cs[_MQ_BLK * HEAD_DIM:_MK_BLK * HEAD_DIM] = HEAD_DIM ** -0.5 * _LOG2E
    return jnp.asarray(cs)


def _alibi_slopes(n):
    return jnp.asarray(2.0 ** (-8.0 * np.arange(1, n + 1) / n), dtype=_F32)


_ATTN_HEADS_PER_STEP = 2


def kernel(x, mem, norm_mix_g, w_in, lambda_q1, lambda_k1, lambda_q2, lambda_k2, subln_g,
           w_out, norm_mem_g, mem_norm_g, w_q_mem, w_kv_mem, w_o_mem, final_g):
    b, s, d = x.shape
    mlen = mem.shape[1]
    depth = w_in.shape[0]
    diff_slopes = _alibi_slopes(DIFF_HEADS)
    moba_slopes = _alibi_slopes(MOBA_HEADS)
    mem2 = mem.reshape(b * mlen, d)
    for l in range(depth):
        lam_init = 0.8 - 0.6 * math.exp(-0.3 * l)
        proj = rms_matmul(x.reshape(b * s, d), norm_mix_g[l], w_in[l].astype(_BF16),
                          _in_proj_col_scale(w_in.shape[-1]), tm=512, tn=1024).reshape(b, s, -1)
        y_d = diff_attention(proj, diff_slopes, lambda_q1[l], lambda_k1[l], lambda_q2[l],
                             lambda_k2[l], subln_g[l], lam_init=lam_init, tq=256,
                             heads_per_step=2 * _ATTN_HEADS_PER_STEP)
        y_m = moba_attention(proj, moba_slopes, heads_per_step=2 * _ATTN_HEADS_PER_STEP)
        x1 = matmul_residual(y_d.reshape(b * s, -1), y_m.reshape(b * s, -1),
                             w_out[l].astype(_BF16), x.reshape(b * s, d), tm=512, tn=1024)
        kv = rms_matmul(mem2, mem_norm_g[l], w_kv_mem[l].astype(_BF16),
                        tm=b * mlen, tn=2 * MEM_WIDTH).reshape(b, mlen, 2 * MEM_WIDTH)
        q = rms_matmul(x1, norm_mem_g[l], w_q_mem[l].astype(_BF16),
                       tm=512, tn=MEM_WIDTH).reshape(b, s, MEM_WIDTH)
        x = mem_attention(q, kv, w_o_mem[l].astype(_BF16), x1.reshape(b, s, d), final_g,
                          apply_final=(l == depth - 1), tm=512)
    return x
```

```python
import functools
import math

import jax
import jax.numpy as jnp
import numpy as np
from jax import lax
from jax.experimental import pallas as pl
from jax.experimental.pallas import tpu as pltpu

HEAD_DIM = 128
DIFF_HEADS = 8
DIFF_HALF = HEAD_DIM // 2
DIFF_WIDTH = DIFF_HEADS * HEAD_DIM
MOBA_HEADS = 8
MOBA_WIDTH = MOBA_HEADS * HEAD_DIM
MOBA_BLOCK = 256
MOBA_TOPK = 3
MEM_HEADS = 4
MEM_WIDTH = MEM_HEADS * HEAD_DIM
EPS = 1e-6

_DQ_BLK = 0
_DK_BLK = DIFF_HEADS
_DV_BLK = 2 * DIFF_HEADS
_MQ_BLK = 3 * DIFF_HEADS
_MK_BLK = 3 * DIFF_HEADS + MOBA_HEADS
_MV_BLK = 3 * DIFF_HEADS + 2 * MOBA_HEADS
_GATE_BLK = 3 * DIFF_HEADS + 3 * MOBA_HEADS

_V7X_VMEM_BYTES = 64 * 1024 * 1024
_VMEM_LIMIT = _V7X_VMEM_BYTES * 3 // 4

_BF16 = jnp.bfloat16
_F32 = jnp.float32
_NEG_INF = float("-inf")


def _dot_nt(a, b):
    return lax.dot_general(a, b, (((1,), (1,)), ((), ())), preferred_element_type=_F32)


def _dot(a, b):
    return jnp.dot(a, b, preferred_element_type=_F32)


def _silu(g):
    return g * jax.nn.sigmoid(g)


def _rms_matmul_kernel(x_ref, g_ref, w_ref, *rest, has_col_scale):
    cs_ref, o_ref, h_scr = rest if has_col_scale else (None,) + rest

    @pl.when(pl.program_id(1) == 0)
    def _():
        x = x_ref[...]
        ms = jnp.mean(x * x, axis=-1, keepdims=True)
        h_scr[...] = (x * lax.rsqrt(ms + EPS) * g_ref[...]).astype(_BF16)

    out = _dot(h_scr[...], w_ref[...])
    if has_col_scale:
        out = out * cs_ref[...]
    o_ref[...] = out.astype(o_ref.dtype)


def rms_matmul(x, g, w, col_scale=None, *, tm, tn, out_dtype=_BF16):
    m, k = x.shape
    n = w.shape[1]
    assert m % tm == 0 and n % tn == 0
    has_cs = col_scale is not None
    cs_specs = [pl.BlockSpec((1, tn), lambda i, j: (0, j))] if has_cs else []
    cs_args = [col_scale.reshape(1, n)] if has_cs else []
    return pl.pallas_call(
        functools.partial(_rms_matmul_kernel, has_col_scale=has_cs),
        grid=(m // tm, n // tn),
        in_specs=[
            pl.BlockSpec((tm, k), lambda i, j: (i, 0)),
            pl.BlockSpec((1, k), lambda i, j: (0, 0)),
            pl.BlockSpec((k, tn), lambda i, j: (0, j)),
        ] + cs_specs,
        out_specs=pl.BlockSpec((tm, tn), lambda i, j: (i, j)),
        out_shape=jax.ShapeDtypeStruct((m, n), out_dtype),
        scratch_shapes=[pltpu.VMEM((tm, k), _BF16)],
        compiler_params=pltpu.CompilerParams(
            dimension_semantics=("parallel", "arbitrary"),
            vmem_limit_bytes=_VMEM_LIMIT),
        name="rms_matmul",
    )(x, g.reshape(1, k), w, *cs_args)


def _matmul_residual_kernel(ya_ref, yb_ref, wa_ref, wb_ref, r_ref, o_ref):
    o_ref[...] = (r_ref[...] + _dot(ya_ref[...], wa_ref[...])
                  + _dot(yb_ref[...], wb_ref[...]))


def matmul_residual(ya, yb, w, res, *, tm, tn):
    m, ka = ya.shape
    kb = yb.shape[1]
    n = w.shape[1]
    assert m % tm == 0 and n % tn == 0 and ka == kb and w.shape[0] == ka + kb
    return pl.pallas_call(
        _matmul_residual_kernel,
        grid=(m // tm, n // tn),
        in_specs=[
            pl.BlockSpec((tm, ka), lambda i, j: (i, 0)),
            pl.BlockSpec((tm, kb), lambda i, j: (i, 0)),
            pl.BlockSpec((ka, tn), lambda i, j: (0, j)),
            pl.BlockSpec((kb, tn), lambda i, j: (1, j)),
            pl.BlockSpec((tm, tn), lambda i, j: (i, j)),
        ],
        out_specs=pl.BlockSpec((tm, tn), lambda i, j: (i, j)),
        out_shape=jax.ShapeDtypeStruct((m, n), _F32),
        compiler_params=pltpu.CompilerParams(
            dimension_semantics=("parallel", "arbitrary"),
            vmem_limit_bytes=_VMEM_LIMIT),
        name="matmul_residual",
    )(ya, yb, w, w, res)


_LOG2E = math.log2(math.e)
_ONES_ROWS = 16
_VT_ROWS = HEAD_DIM + _ONES_ROWS


def _first_flash_step(z, rb, vt, acc_ref):
    m = jnp.max(z, axis=0, keepdims=True) + rb
    p = jnp.exp2(z - (m - rb))
    acc_ref[...] = _dot(vt, p.astype(_BF16))
    return m


def _flash_step(z_ref, rb, vt, m, acc_ref):
    m_new = jnp.maximum(m, jnp.max(z_ref[...], axis=0, keepdims=True) + rb)
    p = jnp.exp2(z_ref[...] - (m_new - rb))
    acc_ref[...] = jnp.exp2(m - m_new) * acc_ref[...] + _dot(vt, p.astype(_BF16))
    return m_new


_BIAS_LANES = 3


def _bias_key_lanes(lane, lane0):
    pos = lax.broadcasted_iota(jnp.int32, lane.shape, 0).astype(_F32)
    return jnp.where((lane >= lane0) & (lane < lane0 + _BIAS_LANES), pos, 0.0)


def _bias_query_lanes(slope, lane, lane0):
    rest = jnp.full(lane.shape, slope, _F32)
    out = jnp.zeros(lane.shape, _F32)
    for t in range(_BIAS_LANES):
        piece = rest.astype(_BF16).astype(_F32)
        out = jnp.where(lane == lane0 + t, piece, out)
        rest = rest - piece
    return out


def _fill_values_transposed(v_ref, vt_scr, g, tk):
    for n in range(vt_scr.shape[1]):
        vt_scr[g, n, :HEAD_DIM, :] = v_ref[n * tk:(n + 1) * tk, _head_lanes(g)].T
        vt_scr[g, n, HEAD_DIM:, :] = jnp.ones((_ONES_ROWS, tk), _BF16)


def _normalised(acc):
    return acc[:HEAD_DIM] / acc[HEAD_DIM:HEAD_DIM + 1]


def _pipelined_flash(n_past, scores, first_update, update, sa_scr, sb_scr):
    scores(n_past, sb_scr)
    scores(0, sa_scr)
    state = first_update(sb_scr)

    def pair(t, st):
        j0 = 2 * t
        j1 = j0 + 1
        scores(j1, sb_scr)
        st = update(j0, sa_scr, st)
        scores(jnp.minimum(j1 + 1, n_past), sa_scr)
        return update(j1, sb_scr, st, valid=j1 < n_past)

    return lax.fori_loop(0, (n_past + 1) // 2, pair, state)


def _head_lanes(g):
    return slice(g * HEAD_DIM, (g + 1) * HEAD_DIM)


def _head_group_spec(rows, tq, width, first_blk, hps):
    assert first_blk % hps == 0
    row_blk = (lambda i: i) if rows == tq else (lambda i: 0)
    return pl.BlockSpec((None, rows, width),
                        lambda bi, hi, i: (bi, row_blk(i), first_blk // hps + hi))


def _diff_attention_kernel(slopes_ref, lq1_ref, lk1_ref, lq2_ref, lk2_ref, subg_ref,
                           q_ref, k_ref, v_ref, gate_ref, o_ref,
                           vt_scr, ka_scr, acc_scr, sa_scr, sb_scr,
                           *, lam_init, tq, heads_per_step):
    i = pl.program_id(2)
    heads = range(heads_per_step)
    lane = lax.broadcasted_iota(jnp.int32, (tq, HEAD_DIM), 1)
    own_half = (lane < DIFF_HALF, lane >= DIFF_HALF)
    bias_lane0 = (DIFF_HALF, 0)

    @pl.when(i == 0)
    def _():
        for g in heads:
            _fill_values_transposed(v_ref, vt_scr, g, tq)
            for n in range(ka_scr.shape[2]):
                k = k_ref[n * tq:(n + 1) * tq, _head_lanes(g)].astype(_F32)
                for c in range(2):
                    ka_scr[g, c, n] = jnp.where(
                        own_half[c], k, _bias_key_lanes(lane, bias_lane0[c])).astype(_BF16)

    key_idx = lax.broadcasted_iota(jnp.int32, (tq, tq), 0)
    qry_idx = lax.broadcasted_iota(jnp.int32, (tq, tq), 1)
    qry_pos = lax.broadcasted_iota(jnp.int32, (1, tq), 1).astype(_F32)

    streams = []
    for g in heads:
        slope = slopes_ref[pl.program_id(1) * heads_per_step + g] * _LOG2E
        q = q_ref[:, _head_lanes(g)].astype(_F32)
        for c in range(2):
            qc = jnp.where(own_half[c], q, _bias_query_lanes(slope, lane, bias_lane0[c]))
            streams.append((g, c, slope, qc.astype(_BF16)))

    def scores(j, s_buf):
        for n, (g, c, _, qc) in enumerate(streams):
            s_buf[n] = _dot_nt(ka_scr[g, c, j], qc)

    def first_update(s_buf):
        out = []
        for n, (g, c, slope, _) in enumerate(streams):
            z = jnp.where(key_idx <= qry_idx, s_buf[n], _NEG_INF)
            out.append(_first_flash_step(z, -slope * qry_pos, vt_scr[g, i],
                                         acc_scr.at[g, c]))
        return tuple(out)

    def update(j, s_buf, st, valid=None):
        out = []
        for n, (g, c, slope, _) in enumerate(streams):
            rb = -slope * qry_pos - slope * ((i - j) * tq).astype(_F32)
            if valid is not None:
                rb = jnp.where(valid, rb, _NEG_INF)
            out.append(_flash_step(s_buf.at[n], rb, vt_scr[g, j], st[n], acc_scr.at[g, c]))
        return tuple(out)

    _pipelined_flash(i, scores, first_update, update, sa_scr, sb_scr)

    lam = (jnp.exp(jnp.sum(lq1_ref[...] * lk1_ref[...], axis=-1, keepdims=True))
           - jnp.exp(jnp.sum(lq2_ref[...] * lk2_ref[...], axis=-1, keepdims=True))
           + lam_init)
    for g in heads:
        o_t = _normalised(acc_scr[g, 0]) - lam * _normalised(acc_scr[g, 1])
        ms = jnp.mean(o_t * o_t, axis=0, keepdims=True)
        o = (o_t * lax.rsqrt(ms + EPS)).T * subg_ref[...] * (1.0 - lam_init)
        gate = gate_ref[:, _head_lanes(g)].astype(_F32)
        o_ref[:, _head_lanes(g)] = (o * _silu(gate)).astype(o_ref.dtype)


def diff_attention(proj, slopes, lq1, lk1, lq2, lk2, subg, *, lam_init, tq, heads_per_step):
    b, s, _ = proj.shape
    hps = heads_per_step
    assert s % tq == 0 and DIFF_HEADS % hps == 0
    width = hps * HEAD_DIM
    vec = lambda n: pl.BlockSpec((1, n), lambda bi, hi, i: (0, 0))
    spec = functools.partial(_head_group_spec, tq=tq, width=width, hps=hps)
    scratch = [pltpu.VMEM((hps, s // tq, _VT_ROWS, tq), _BF16),
               pltpu.VMEM((hps, 2, s // tq, tq, HEAD_DIM), _BF16),
               pltpu.VMEM((hps, 2, _VT_ROWS, tq), _F32),
               pltpu.VMEM((2 * hps, tq, tq), _F32),
               pltpu.VMEM((2 * hps, tq, tq), _F32)]
    return pl.pallas_call(
        functools.partial(_diff_attention_kernel, lam_init=lam_init, tq=tq,
                          heads_per_step=hps),
        grid=(b, DIFF_HEADS // hps, s // tq),
        in_specs=[
            pl.BlockSpec(memory_space=pltpu.SMEM),
            vec(DIFF_HALF), vec(DIFF_HALF), vec(DIFF_HALF), vec(DIFF_HALF), vec(HEAD_DIM),
            spec(tq, first_blk=_DQ_BLK), spec(s, first_blk=_DK_BLK), spec(s, first_blk=_DV_BLK),
            spec(tq, first_blk=_GATE_BLK),
        ],
        out_specs=pl.BlockSpec((None, tq, width), lambda bi, hi, i: (bi, i, hi)),
        out_shape=jax.ShapeDtypeStruct((b, s, DIFF_WIDTH), _BF16),
        scratch_shapes=scratch,
        compiler_params=pltpu.CompilerParams(
            dimension_semantics=("parallel", "parallel", "arbitrary"),
            vmem_limit_bytes=_VMEM_LIMIT),
        name="diff_attention",
    )(slopes, lq1.reshape(1, -1), lk1.reshape(1, -1), lq2.reshape(1, -1), lk2.reshape(1, -1),
      subg.reshape(1, -1), proj, proj, proj, proj)


def _moba_attention_kernel(slopes_ref, q_ref, k_ref, v_ref, gate_ref, o_ref,
                           kmean_scr, vt_scr, ka_scr, sel_scr, acc_scr, sa_scr, sb_scr,
                           *, n_blocks, heads_per_step):
    i = pl.program_id(2)
    tq = MOBA_BLOCK
    heads = range(heads_per_step)

    lane = lax.broadcasted_iota(jnp.int32, (tq, HEAD_DIM), 1)

    @pl.when(i == 0)
    def _():
        for g in heads:
            _fill_values_transposed(v_ref, vt_scr, g, tq)
            for n in range(n_blocks):
                k = k_ref[n * MOBA_BLOCK:(n + 1) * MOBA_BLOCK, _head_lanes(g)]
                kmean_scr[g, n:n + 1, :] = (jnp.sum(k.astype(_F32), axis=0, keepdims=True)
                                            * (1.0 / MOBA_BLOCK))
                ka_scr[g, n, :, :HEAD_DIM] = k
                ka_scr[g, n, :, HEAD_DIM:] = _bias_key_lanes(lane, 0).astype(_BF16)

    key_idx = lax.broadcasted_iota(jnp.int32, (tq, tq), 0)
    qry_idx = lax.broadcasted_iota(jnp.int32, (tq, tq), 1)
    qry_pos = lax.broadcasted_iota(jnp.int32, (1, tq), 1).astype(_F32)
    blk = lax.broadcasted_iota(jnp.int32, (n_blocks, tq), 0)
    past = blk < i

    slopes = [slopes_ref[pl.program_id(1) * heads_per_step + g] * _LOG2E for g in heads]
    qs = [q_ref[:, _head_lanes(g)] for g in heads]
    qas = [jnp.concatenate([qs[g], _bias_query_lanes(slopes[g], lane, 0).astype(_BF16)], axis=1)
           for g in heads]

    def scores(j, s_buf):
        for g in heads:
            s_buf[g] = _dot_nt(ka_scr[g, j], qas[g])

    def select_blocks(g):
        gate = lax.dot_general(kmean_scr[g], qs[g].astype(_F32), (((1,), (1,)), ((), ())),
                               precision=lax.Precision.HIGHEST,
                               preferred_element_type=_F32)
        gate = jnp.where(past, gate, _NEG_INF)
        rank = jnp.zeros(gate.shape, jnp.int32)
        for n in range(n_blocks):
            g_n = gate[n:n + 1, :]
            beats = (g_n > gate) | ((g_n == gate) & (n < blk))
            rank = rank + beats.astype(jnp.int32)
        sel_scr[g] = jnp.where(past & (rank < MOBA_TOPK), 0.0, _NEG_INF)

    def first_update(s_buf):
        out = []
        for g in heads:
            select_blocks(g)
            z = jnp.where(key_idx <= qry_idx, s_buf[g], _NEG_INF)
            out.append(_first_flash_step(z, -slopes[g] * qry_pos, vt_scr[g, i], acc_scr.at[g]))
        return tuple(out)

    def update(j, s_buf, st, valid=None):
        out = []
        for g in heads:
            slope = slopes[g]
            rb = (-slope * qry_pos - slope * ((i - j) * tq).astype(_F32)
                  + sel_scr[g, pl.ds(j, 1), :])
            out.append(_flash_step(s_buf.at[g], rb, vt_scr[g, j], st[g], acc_scr.at[g]))
        return tuple(out)

    _pipelined_flash(i, scores, first_update, update, sa_scr, sb_scr)
    for g in heads:
        o = _normalised(acc_scr[g]).T
        gate = gate_ref[:, _head_lanes(g)].astype(_F32)
        o_ref[:, _head_lanes(g)] = (o * _silu(gate)).astype(o_ref.dtype)


def moba_attention(proj, slopes, *, heads_per_step):
    b, s, _ = proj.shape
    hps = heads_per_step
    assert s % MOBA_BLOCK == 0 and MOBA_HEADS % hps == 0
    n_blocks = s // MOBA_BLOCK
    tq = MOBA_BLOCK
    width = hps * HEAD_DIM
    spec = functools.partial(_head_group_spec, tq=tq, width=width, hps=hps)
    return pl.pallas_call(
        functools.partial(_moba_attention_kernel, n_blocks=n_blocks, heads_per_step=hps),
        grid=(b, MOBA_HEADS // hps, n_blocks),
        in_specs=[
            pl.BlockSpec(memory_space=pltpu.SMEM),
            spec(tq, first_blk=_MQ_BLK), spec(s, first_blk=_MK_BLK), spec(s, first_blk=_MV_BLK),
            spec(tq, first_blk=_GATE_BLK + DIFF_HEADS),
        ],
        out_specs=pl.BlockSpec((None, tq, width), lambda bi, hi, i: (bi, i, hi)),
        out_shape=jax.ShapeDtypeStruct((b, s, MOBA_WIDTH), _BF16),
        scratch_shapes=[pltpu.VMEM((hps, n_blocks, HEAD_DIM), _F32),
                        pltpu.VMEM((hps, n_blocks, _VT_ROWS, tq), _BF16),
                        pltpu.VMEM((hps, n_blocks, tq, 2 * HEAD_DIM), _BF16),
                        pltpu.VMEM((hps, n_blocks, tq), _F32),
                        pltpu.VMEM((hps, _VT_ROWS, tq), _F32),
                        pltpu.VMEM((hps, tq, tq), _F32),
                        pltpu.VMEM((hps, tq, tq), _F32)],
        compiler_params=pltpu.CompilerParams(
            dimension_semantics=("parallel", "parallel", "arbitrary"),
            vmem_limit_bytes=_VMEM_LIMIT),
        name="moba_attention",
    )(slopes, proj, proj, proj, proj)


def _mem_attention_kernel(q_ref, kv_ref, wo_ref, x_ref, fg_ref, o_ref, *, apply_final):
    scale = HEAD_DIM ** -0.5
    x = x_ref[...]
    for hh in range(MEM_HEADS):
        lo, hi = hh * HEAD_DIM, (hh + 1) * HEAD_DIM
        s = _dot_nt(q_ref[:, lo:hi], kv_ref[:, lo:hi]) * scale
        p = jnp.exp(s - jnp.max(s, axis=-1, keepdims=True))
        l = jnp.sum(p, axis=-1, keepdims=True)
        o_h = _dot(p.astype(_BF16), kv_ref[:, MEM_WIDTH + lo:MEM_WIDTH + hi]) / l
        x = x + _dot(o_h.astype(_BF16), wo_ref[lo:hi, :])
    if apply_final:
        ms = jnp.mean(x * x, axis=-1, keepdims=True)
        x = x * lax.rsqrt(ms + EPS) * fg_ref[...]
    o_ref[...] = x


def mem_attention(q, kv, wo, x, final_g, *, apply_final, tm):
    b, s, d = x.shape
    mlen = kv.shape[1]
    assert s % tm == 0
    return pl.pallas_call(
        functools.partial(_mem_attention_kernel, apply_final=apply_final),
        grid=(b, s // tm),
        in_specs=[
            pl.BlockSpec((None, tm, MEM_WIDTH), lambda bi, i: (bi, i, 0)),
            pl.BlockSpec((None, mlen, 2 * MEM_WIDTH), lambda bi, i: (bi, 0, 0)),
            pl.BlockSpec((MEM_WIDTH, d), lambda bi, i: (0, 0)),
            pl.BlockSpec((None, tm, d), lambda bi, i: (bi, i, 0)),
            pl.BlockSpec((1, d), lambda bi, i: (0, 0)),
        ],
        out_specs=pl.BlockSpec((None, tm, d), lambda bi, i: (bi, i, 0)),
        out_shape=jax.ShapeDtypeStruct((b, s, d), _F32),
        compiler_params=pltpu.CompilerParams(
            dimension_semantics=("parallel", "parallel"),
            vmem_limit_bytes=_VMEM_LIMIT),
        name="mem_attention",
    )(q, kv, wo, x, final_g.reshape(1, d))


def _in_proj_col_scale(n_cols):
    cs = np.ones((n_cols,), np.float32)
    cs[_DQ_BLK * HEAD_DIM:_DK_BLK * HEAD_DIM] = DIFF_HALF ** -0.5 * _LOG2E
    cs[_MQ_BLK * HEAD_DIM:_MK_BLK * HEAD_DIM] = HEAD_DIM ** -0.5 * _LOG2E
    return jnp.asarray(cs)


def _alibi_slopes(n):
    return jnp.asarray(2.0 ** (-8.0 * np.arange(1, n + 1) / n), dtype=_F32)


_ATTN_HEADS_PER_STEP = 2


def kernel(x, mem, norm_mix_g, w_in, lambda_q1, lambda_k1, lambda_q2, lambda_k2, subln_g,
           w_out, norm_mem_g, mem_norm_g, w_q_mem, w_kv_mem, w_o_mem, final_g):
    b, s, d = x.shape
    mlen = mem.shape[1]
    depth = w_in.shape[0]
    diff_slopes = _alibi_slopes(DIFF_HEADS)
    moba_slopes = _alibi_slopes(MOBA_HEADS)
    mem2 = mem.reshape(b * mlen, d)
    for l in range(depth):
        lam_init = 0.8 - 0.6 * math.exp(-0.3 * l)
        proj = rms_matmul(x.reshape(b * s, d), norm_mix_g[l], w_in[l].astype(_BF16),
                          _in_proj_col_scale(w_in.shape[-1]), tm=512, tn=1024).reshape(b, s, -1)
        y_d = diff_attention(proj, diff_slopes, lambda_q1[l], lambda_k1[l], lambda_q2[l],
                             lambda_k2[l], subln_g[l], lam_init=lam_init, tq=256,
                             heads_per_step=2 * _ATTN_HEADS_PER_STEP)
        y_m = moba_attention(proj, moba_slopes, heads_per_step=2 * _ATTN_HEADS_PER_STEP)
        x1 = matmul_residual(y_d.reshape(b * s, -1), y_m.reshape(b * s, -1),
                             w_out[l].astype(_BF16), x.reshape(b * s, d), tm=512, tn=1024)
        kv = rms_matmul(mem2, mem_norm_g[l], w_kv_mem[l].astype(_BF16),
                        tm=b * mlen, tn=2 * MEM_WIDTH).reshape(b, mlen, 2 * MEM_WIDTH)
        q = rms_matmul(x1, norm_mem_g[l], w_q_mem[l].astype(_BF16),
                       tm=512, tn=MEM_WIDTH).reshape(b, s, MEM_WIDTH)
        x = mem_attention(q, kv, w_o_mem[l].astype(_BF16), x1.reshape(b, s, d), final_g,
                          apply_final=(l == depth - 1), tm=512)
    return x
```

```python
import functools
import math

import jax
import jax.numpy as jnp
import numpy as np
from jax import lax
from jax.experimental import pallas as pl
from jax.experimental.pallas import tpu as pltpu

HEAD_DIM = 128
DIFF_HEADS = 8
DIFF_HALF = HEAD_DIM // 2
DIFF_WIDTH = DIFF_HEADS * HEAD_DIM
MOBA_HEADS = 8
MOBA_WIDTH = MOBA_HEADS * HEAD_DIM
MOBA_BLOCK = 256
MOBA_TOPK = 3
MEM_HEADS = 4
MEM_WIDTH = MEM_HEADS * HEAD_DIM
EPS = 1e-6

_DQ_BLK = 0
_DK_BLK = DIFF_HEADS
_DV_BLK = 2 * DIFF_HEADS
_MQ_BLK = 3 * DIFF_HEADS
_MK_BLK = 3 * DIFF_HEADS + MOBA_HEADS
_MV_BLK = 3 * DIFF_HEADS + 2 * MOBA_HEADS
_GATE_BLK = 3 * DIFF_HEADS + 3 * MOBA_HEADS

_V7X_VMEM_BYTES = 64 * 1024 * 1024
_VMEM_LIMIT = _V7X_VMEM_BYTES * 3 // 4

_BF16 = jnp.bfloat16
_F32 = jnp.float32
_NEG_INF = float("-inf")


def _dot_nt(a, b):
    return lax.dot_general(a, b, (((1,), (1,)), ((), ())), preferred_element_type=_F32)


def _dot(a, b):
    return jnp.dot(a, b, preferred_element_type=_F32)


def _silu(g):
    return g * jax.nn.sigmoid(g)


def _rms_matmul_kernel(x_ref, g_ref, w_ref, *rest, has_col_scale):
    cs_ref, o_ref, h_scr = rest if has_col_scale else (None,) + rest

    @pl.when(pl.program_id(1) == 0)
    def _():
        x = x_ref[...]
        ms = jnp.mean(x * x, axis=-1, keepdims=True)
        h_scr[...] = (x * lax.rsqrt(ms + EPS) * g_ref[...]).astype(_BF16)

    out = _dot(h_scr[...], w_ref[...])
    if has_col_scale:
        out = out * cs_ref[...]
    o_ref[...] = out.astype(o_ref.dtype)


def rms_matmul(x, g, w, col_scale=None, *, tm, tn, out_dtype=_BF16):
    m, k = x.shape
    n = w.shape[1]
    assert m % tm == 0 and n % tn == 0
    has_cs = col_scale is not None
    cs_specs = [pl.BlockSpec((1, tn), lambda i, j: (0, j))] if has_cs else []
    cs_args = [col_scale.reshape(1, n)] if has_cs else []
    return pl.pallas_call(
        functools.partial(_rms_matmul_kernel, has_col_scale=has_cs),
        grid=(m // tm, n // tn),
        in_specs=[
            pl.BlockSpec((tm, k), lambda i, j: (i, 0)),
            pl.BlockSpec((1, k), lambda i, j: (0, 0)),
            pl.BlockSpec((k, tn), lambda i, j: (0, j)),
        ] + cs_specs,
        out_specs=pl.BlockSpec((tm, tn), lambda i, j: (i, j)),
        out_shape=jax.ShapeDtypeStruct((m, n), out_dtype),
        scratch_shapes=[pltpu.VMEM((tm, k), _BF16)],
        compiler_params=pltpu.CompilerParams(
            dimension_semantics=("parallel", "arbitrary"),
            vmem_limit_bytes=_VMEM_LIMIT),
        name="rms_matmul",
    )(x, g.reshape(1, k), w, *cs_args)


_LOG2E = math.log2(math.e)
_ONES_ROWS = 16
_VT_ROWS = HEAD_DIM + _ONES_ROWS


def _first_flash_step(z, rb, vt, acc_ref):
    m = jnp.max(z, axis=0, keepdims=True) + rb
    p = jnp.exp2(z - (m - rb))
    acc_ref[...] = _dot(vt, p.astype(_BF16))
    return m


def _flash_step(z_ref, rb, vt, m, acc_ref):
    m_new = jnp.maximum(m, jnp.max(z_ref[...], axis=0, keepdims=True) + rb)
    p = jnp.exp2(z_ref[...] - (m_new - rb))
    acc_ref[...] = jnp.exp2(m - m_new) * acc_ref[...] + _dot(vt, p.astype(_BF16))
    return m_new


_BIAS_LANES = 3


def _bias_key_lanes(lane, lane0):
    pos = lax.broadcasted_iota(jnp.int32, lane.shape, 0).astype(_F32)
    return jnp.where((lane >= lane0) & (lane < lane0 + _BIAS_LANES), pos, 0.0)


def _bias_query_lanes(slope, lane, lane0):
    rest = jnp.full(lane.shape, slope, _F32)
    out = jnp.zeros(lane.shape, _F32)
    for t in range(_BIAS_LANES):
        piece = rest.astype(_BF16).astype(_F32)
        out = jnp.where(lane == lane0 + t, piece, out)
        rest = rest - piece
    return out


def _fill_values_transposed(v_ref, vt_scr, g, tk):
    for n in range(vt_scr.shape[1]):
        vt_scr[g, n, :HEAD_DIM, :] = v_ref[n * tk:(n + 1) * tk, _head_lanes(g)].T
        vt_scr[g, n, HEAD_DIM:, :] = jnp.ones((_ONES_ROWS, tk), _BF16)


def _normalised(acc):
    return acc[:HEAD_DIM] / acc[HEAD_DIM:HEAD_DIM + 1]


def _pipelined_flash(n_past, scores, first_update, update, sa_scr, sb_scr):
    scores(n_past, sb_scr)
    scores(0, sa_scr)
    state = first_update(sb_scr)

    def pair(t, st):
        j0 = 2 * t
        j1 = j0 + 1
        scores(j1, sb_scr)
        st = update(j0, sa_scr, st)
        scores(jnp.minimum(j1 + 1, n_past), sa_scr)
        return update(j1, sb_scr, st, valid=j1 < n_past)

    return lax.fori_loop(0, (n_past + 1) // 2, pair, state)


def _head_lanes(g):
    return slice(g * HEAD_DIM, (g + 1) * HEAD_DIM)


def _head_group_spec(rows, tq, width, first_blk, hps):
    assert first_blk % hps == 0
    row_blk = (lambda i: i) if rows == tq else (lambda i: 0)
    return pl.BlockSpec((None, rows, width),
                        lambda bi, hi, i: (bi, row_blk(i), first_blk // hps + hi))


def _diff_attention_kernel(slopes_ref, lq1_ref, lk1_ref, lq2_ref, lk2_ref, subg_ref,
                           q_ref, k_ref, v_ref, gate_ref, o_ref,
                           vt_scr, ka_scr, acc_scr, sa_scr, sb_scr,
                           *, lam_init, tq, heads_per_step):
    i = pl.program_id(2)
    heads = range(heads_per_step)
    lane = lax.broadcasted_iota(jnp.int32, (tq, HEAD_DIM), 1)
    own_half = (lane < DIFF_HALF, lane >= DIFF_HALF)
    bias_lane0 = (DIFF_HALF, 0)

    @pl.when(i == 0)
    def _():
        for g in heads:
            _fill_values_transposed(v_ref, vt_scr, g, tq)
            for n in range(ka_scr.shape[2]):
                k = k_ref[n * tq:(n + 1) * tq, _head_lanes(g)].astype(_F32)
                for c in range(2):
                    ka_scr[g, c, n] = jnp.where(
                        own_half[c], k, _bias_key_lanes(lane, bias_lane0[c])).astype(_BF16)

    key_idx = lax.broadcasted_iota(jnp.int32, (tq, tq), 0)
    qry_idx = lax.broadcasted_iota(jnp.int32, (tq, tq), 1)
    qry_pos = lax.broadcasted_iota(jnp.int32, (1, tq), 1).astype(_F32)

    streams = []
    for g in heads:
        slope = slopes_ref[pl.program_id(1) * heads_per_step + g] * _LOG2E
        q = q_ref[:, _head_lanes(g)].astype(_F32)
        for c in range(2):
            qc = jnp.where(own_half[c], q, _bias_query_lanes(slope, lane, bias_lane0[c]))
            streams.append((g, c, slope, qc.astype(_BF16)))

    def scores(j, s_buf):
        for n, (g, c, _, qc) in enumerate(streams):
            s_buf[n] = _dot_nt(ka_scr[g, c, j], qc)

    def first_update(s_buf):
        out = []
        for n, (g, c, slope, _) in enumerate(streams):
            z = jnp.where(key_idx <= qry_idx, s_buf[n], _NEG_INF)
            out.append(_first_flash_step(z, -slope * qry_pos, vt_scr[g, i],
                                         acc_scr.at[g, c]))
        return tuple(out)

    def update(j, s_buf, st, valid=None):
        out = []
        for n, (g, c, slope, _) in enumerate(streams):
            rb = -slope * qry_pos - slope * ((i - j) * tq).astype(_F32)
            if valid is not None:
                rb = jnp.where(valid, rb, _NEG_INF)
            out.append(_flash_step(s_buf.at[n], rb, vt_scr[g, j], st[n], acc_scr.at[g, c]))
        return tuple(out)

    _pipelined_flash(i, scores, first_update, update, sa_scr, sb_scr)

    lam = (jnp.exp(jnp.sum(lq1_ref[...] * lk1_ref[...], axis=-1, keepdims=True))
           - jnp.exp(jnp.sum(lq2_ref[...] * lk2_ref[...], axis=-1, keepdims=True))
           + lam_init)
    for g in heads:
        o_t = _normalised(acc_scr[g, 0]) - lam * _normalised(acc_scr[g, 1])
        ms = jnp.mean(o_t * o_t, axis=0, keepdims=True)
        o = (o_t * lax.rsqrt(ms + EPS)).T * subg_ref[...] * (1.0 - lam_init)
        gate = gate_ref[:, _head_lanes(g)].astype(_F32)
        o_ref[:, _head_lanes(g)] = (o * _silu(gate)).astype(o_ref.dtype)


def diff_attention(proj, slopes, lq1, lk1, lq2, lk2, subg, *, lam_init, tq, heads_per_step):
    b, s, _ = proj.shape
    hps = heads_per_step
    assert s % tq == 0 and DIFF_HEADS % hps == 0
    width = hps * HEAD_DIM
    vec = lambda n: pl.BlockSpec((1, n), lambda bi, hi, i: (0, 0))
    spec = functools.partial(_head_group_spec, tq=tq, width=width, hps=hps)
    scratch = [pltpu.VMEM((hps, s // tq, _VT_ROWS, tq), _BF16),
               pltpu.VMEM((hps, 2, s // tq, tq, HEAD_DIM), _BF16),
               pltpu.VMEM((hps, 2, _VT_ROWS, tq), _F32),
               pltpu.VMEM((2 * hps, tq, tq), _F32),
               pltpu.VMEM((2 * hps, tq, tq), _F32)]
    return pl.pallas_call(
        functools.partial(_diff_attention_kernel, lam_init=lam_init, tq=tq,
                          heads_per_step=hps),
        grid=(b, DIFF_HEADS // hps, s // tq),
        in_specs=[
            pl.BlockSpec(memory_space=pltpu.SMEM),
            vec(DIFF_HALF), vec(DIFF_HALF), vec(DIFF_HALF), vec(DIFF_HALF), vec(HEAD_DIM),
            spec(tq, first_blk=_DQ_BLK), spec(s, first_blk=_DK_BLK), spec(s, first_blk=_DV_BLK),
            spec(tq, first_blk=_GATE_BLK),
        ],
        out_specs=pl.BlockSpec((None, tq, width), lambda bi, hi, i: (bi, i, hi)),
        out_shape=jax.ShapeDtypeStruct((b, s, DIFF_WIDTH), _BF16),
        scratch_shapes=scratch,
        compiler_params=pltpu.CompilerParams(
            dimension_semantics=("parallel", "parallel", "arbitrary"),
            vmem_limit_bytes=_VMEM_LIMIT),
        name="diff_attention",
    )(slopes, lq1.reshape(1, -1), lk1.reshape(1, -1), lq2.reshape(1, -1), lk2.reshape(1, -1),
      subg.reshape(1, -1), proj, proj, proj, proj)


def _moba_attention_kernel(slopes_ref, q_ref, k_ref, v_ref, gate_ref, o_ref,
                           kmean_scr, vt_scr, ka_scr, sel_scr, acc_scr, sa_scr, sb_scr,
                           *, n_blocks, heads_per_step):
    i = pl.program_id(2)
    tq = MOBA_BLOCK
    heads = range(heads_per_step)

    lane = lax.broadcasted_iota(jnp.int32, (tq, HEAD_DIM), 1)

    @pl.when(i == 0)
    def _():
        for g in heads:
            _fill_values_transposed(v_ref, vt_scr, g, tq)
            for n in range(n_blocks):
                k = k_ref[n * MOBA_BLOCK:(n + 1) * MOBA_BLOCK, _head_lanes(g)]
                kmean_scr[g, n:n + 1, :] = (jnp.sum(k.astype(_F32), axis=0, keepdims=True)
                                            * (1.0 / MOBA_BLOCK))
                ka_scr[g, n, :, :HEAD_DIM] = k
                ka_scr[g, n, :, HEAD_DIM:] = _bias_key_lanes(lane, 0).astype(_BF16)

    key_idx = lax.broadcasted_iota(jnp.int32, (tq, tq), 0)
    qry_idx = lax.broadcasted_iota(jnp.int32, (tq, tq), 1)
    qry_pos = lax.broadcasted_iota(jnp.int32, (1, tq), 1).astype(_F32)
    blk = lax.broadcasted_iota(jnp.int32, (n_blocks, tq), 0)
    past = blk < i

    slopes = [slopes_ref[pl.program_id(1) * heads_per_step + g] * _LOG2E for g in heads]
    qs = [q_ref[:, _head_lanes(g)] for g in heads]
    qas = [jnp.concatenate([qs[g], _bias_query_lanes(slopes[g], lane, 0).astype(_BF16)], axis=1)
           for g in heads]

    def scores(j, s_buf):
        for g in heads:
            s_buf[g] = _dot_nt(ka_scr[g, j], qas[g])

    def select_blocks(g):
        gate = lax.dot_general(kmean_scr[g], qs[g].astype(_F32), (((1,), (1,)), ((), ())),
                               precision=lax.Precision.HIGHEST,
                               preferred_element_type=_F32)
        gate = jnp.where(past, gate, _NEG_INF)
        rank = jnp.zeros(gate.shape, jnp.int32)
        for n in range(n_blocks):
            g_n = gate[n:n + 1, :]
            beats = (g_n > gate) | ((g_n == gate) & (n < blk))
            rank = rank + beats.astype(jnp.int32)
        sel_scr[g] = jnp.where(past & (rank < MOBA_TOPK), 0.0, _NEG_INF)

    def first_update(s_buf):
        out = []
        for g in heads:
            select_blocks(g)
            z = jnp.where(key_idx <= qry_idx, s_buf[g], _NEG_INF)
            out.append(_first_flash_step(z, -slopes[g] * qry_pos, vt_scr[g, i], acc_scr.at[g]))
        return tuple(out)

    def update(j, s_buf, st, valid=None):
        out = []
        for g in heads:
            slope = slopes[g]
            rb = (-slope * qry_pos - slope * ((i - j) * tq).astype(_F32)
                  + sel_scr[g, pl.ds(j, 1), :])
            out.append(_flash_step(s_buf.at[g], rb, vt_scr[g, j], st[g], acc_scr.at[g]))
        return tuple(out)

    _pipelined_flash(i, scores, first_update, update, sa_scr, sb_scr)
    for g in heads:
        o = _normalised(acc_scr[g]).T
        gate = gate_ref[:, _head_lanes(g)].astype(_F32)
        o_ref[:, _head_lanes(g)] = (o * _silu(gate)).astype(o_ref.dtype)


def moba_attention(proj, slopes, *, heads_per_step):
    b, s, _ = proj.shape
    hps = heads_per_step
    assert s % MOBA_BLOCK == 0 and MOBA_HEADS % hps == 0
    n_blocks = s // MOBA_BLOCK
    tq = MOBA_BLOCK
    width = hps * HEAD_DIM
    spec = functools.partial(_head_group_spec, tq=tq, width=width, hps=hps)
    return pl.pallas_call(
        functools.partial(_moba_attention_kernel, n_blocks=n_blocks, heads_per_step=hps),
        grid=(b, MOBA_HEADS // hps, n_blocks),
        in_specs=[
            pl.BlockSpec(memory_space=pltpu.SMEM),
            spec(tq, first_blk=_MQ_BLK), spec(s, first_blk=_MK_BLK), spec(s, first_blk=_MV_BLK),
            spec(tq, first_blk=_GATE_BLK + DIFF_HEADS),
        ],
        out_specs=pl.BlockSpec((None, tq, width), lambda bi, hi, i: (bi, i, hi)),
        out_shape=jax.ShapeDtypeStruct((b, s, MOBA_WIDTH), _BF16),
        scratch_shapes=[pltpu.VMEM((hps, n_blocks, HEAD_DIM), _F32),
                        pltpu.VMEM((hps, n_blocks, _VT_ROWS, tq), _BF16),
                        pltpu.VMEM((hps, n_blocks, tq, 2 * HEAD_DIM), _BF16),
                        pltpu.VMEM((hps, n_blocks, tq), _F32),
                        pltpu.VMEM((hps, _VT_ROWS, tq), _F32),
                        pltpu.VMEM((hps, tq, tq), _F32),
                        pltpu.VMEM((hps, tq, tq), _F32)],
        compiler_params=pltpu.CompilerParams(
            dimension_semantics=("parallel", "parallel", "arbitrary"),
            vmem_limit_bytes=_VMEM_LIMIT),
        name="moba_attention",
    )(slopes, proj, proj, proj, proj)


def _layer_tail_kernel(yd_ref, ym_ref, x_ref, wa_ref, wb_ref, g_ref, wq_ref, kv_ref, wo_ref,
                       fg_ref, o_ref, q_scr, o_scr, *, apply_final):
    x = x_ref[...] + _dot(yd_ref[...], wa_ref[...]) + _dot(ym_ref[...], wb_ref[...])
    ms = jnp.mean(x * x, axis=-1, keepdims=True)
    h = (x * lax.rsqrt(ms + EPS) * g_ref[...]).astype(_BF16)
    q_scr[...] = (_dot(h, wq_ref[...]) * (HEAD_DIM ** -0.5 * _LOG2E)).astype(_BF16)
    for hh in range(MEM_HEADS):
        lo, hi = hh * HEAD_DIM, (hh + 1) * HEAD_DIM
        s = _dot_nt(q_scr[:, lo:hi], kv_ref[:, lo:hi])
        p = jnp.exp2(s - jnp.max(s, axis=-1, keepdims=True))
        l = jnp.sum(p, axis=-1, keepdims=True)
        o_h = _dot(p.astype(_BF16), kv_ref[:, MEM_WIDTH + lo:MEM_WIDTH + hi]) / l
        o_scr[:, lo:hi] = o_h.astype(_BF16)
    x = x + _dot(o_scr[...], wo_ref[...])
    if apply_final:
        ms = jnp.mean(x * x, axis=-1, keepdims=True)
        x = x * lax.rsqrt(ms + EPS) * fg_ref[...]
    o_ref[...] = x


def layer_tail(yd, ym, x, w_out, g, wq, kv, wo, final_g, *, apply_final, tm):
    b, s, d = x.shape
    mlen = kv.shape[1]
    ka, kb = yd.shape[-1], ym.shape[-1]
    assert s % tm == 0 and ka == kb and w_out.shape[0] == ka + kb
    rows = lambda width: pl.BlockSpec((None, tm, width), lambda bi, i: (bi, i, 0))
    whole = lambda shape: pl.BlockSpec(shape, lambda bi, i: (0,) * len(shape))
    return pl.pallas_call(
        functools.partial(_layer_tail_kernel, apply_final=apply_final),
        grid=(b, s // tm),
        in_specs=[
            rows(ka), rows(kb), rows(d),
            pl.BlockSpec((ka, d), lambda bi, i: (0, 0)),
            pl.BlockSpec((kb, d), lambda bi, i: (1, 0)),
            whole((1, d)), whole((d, MEM_WIDTH)),
            pl.BlockSpec((None, mlen, 2 * MEM_WIDTH), lambda bi, i: (bi, 0, 0)),
            whole((MEM_WIDTH, d)), whole((1, d)),
        ],
        out_specs=rows(d),
        out_shape=jax.ShapeDtypeStruct((b, s, d), _F32),
        scratch_shapes=[pltpu.VMEM((tm, MEM_WIDTH), _BF16),
                        pltpu.VMEM((tm, MEM_WIDTH), _BF16)],
        compiler_params=pltpu.CompilerParams(
            dimension_semantics=("parallel", "parallel"),
            vmem_limit_bytes=_VMEM_LIMIT),
        name="layer_tail",
    )(yd, ym, x, w_out, w_out, g.reshape(1, d), wq, kv, wo, final_g.reshape(1, d))


def _in_proj_col_scale(n_cols):
    cs = np.ones((n_cols,), np.float32)
    cs[_DQ_BLK * HEAD_DIM:_DK_BLK * HEAD_DIM] = DIFF_HALF ** -0.5 * _LOG2E
    cs[_MQ_BLK * HEAD_DIM:_MK_BLK * HEAD_DIM] = HEAD_DIM ** -0.5 * _LOG2E
    return jnp.asarray(cs)


def _alibi_slopes(n):
    return jnp.asarray(2.0 ** (-8.0 * np.arange(1, n + 1) / n), dtype=_F32)


_ATTN_HEADS_PER_STEP = 2


def kernel(x, mem, norm_mix_g, w_in, lambda_q1, lambda_k1, lambda_q2, lambda_k2, subln_g,
           w_out, norm_mem_g, mem_norm_g, w_q_mem, w_kv_mem, w_o_mem, final_g):
    b, s, d = x.shape
    mlen = mem.shape[1]
    depth = w_in.shape[0]
    diff_slopes = _alibi_slopes(DIFF_HEADS)
    moba_slopes = _alibi_slopes(MOBA_HEADS)
    mem2 = mem.reshape(b * mlen, d)
    for l in range(depth):
        lam_init = 0.8 - 0.6 * math.exp(-0.3 * l)
        proj = rms_matmul(x.reshape(b * s, d), norm_mix_g[l], w_in[l].astype(_BF16),
                          _in_proj_col_scale(w_in.shape[-1]), tm=1024, tn=1024).reshape(b, s, -1)
        y_d = diff_attention(proj, diff_slopes, lambda_q1[l], lambda_k1[l], lambda_q2[l],
                             lambda_k2[l], subln_g[l], lam_init=lam_init, tq=256,
                             heads_per_step=2 * _ATTN_HEADS_PER_STEP)
        y_m = moba_attention(proj, moba_slopes, heads_per_step=2 * _ATTN_HEADS_PER_STEP)
        kv = rms_matmul(mem2, mem_norm_g[l], w_kv_mem[l].astype(_BF16),
                        tm=b * mlen, tn=2 * MEM_WIDTH).reshape(b, mlen, 2 * MEM_WIDTH)
        x = layer_tail(y_d, y_m, x, w_out[l].astype(_BF16), norm_mem_g[l],
                       w_q_mem[l].astype(_BF16), kv, w_o_mem[l].astype(_BF16), final_g,
                       apply_final=(l == depth - 1), tm=256)
    return x
```

```python
import functools
import math

import jax
import jax.numpy as jnp
import numpy as np
from jax import lax
from jax.experimental import pallas as pl
from jax.experimental.pallas import tpu as pltpu

HEAD_DIM = 128
DIFF_HEADS = 8
DIFF_HALF = HEAD_DIM // 2
DIFF_WIDTH = DIFF_HEADS * HEAD_DIM
MOBA_HEADS = 8
MOBA_WIDTH = MOBA_HEADS * HEAD_DIM
MOBA_BLOCK = 256
MOBA_TOPK = 3
MEM_HEADS = 4
MEM_WIDTH = MEM_HEADS * HEAD_DIM
EPS = 1e-6

_DQ_BLK = 0
_DK_BLK = DIFF_HEADS
_DV_BLK = 2 * DIFF_HEADS
_MQ_BLK = 3 * DIFF_HEADS
_MK_BLK = 3 * DIFF_HEADS + MOBA_HEADS
_MV_BLK = 3 * DIFF_HEADS + 2 * MOBA_HEADS
_GATE_BLK = 3 * DIFF_HEADS + 3 * MOBA_HEADS

_V7X_VMEM_BYTES = 64 * 1024 * 1024
_VMEM_LIMIT = _V7X_VMEM_BYTES * 3 // 4

_BF16 = jnp.bfloat16
_F32 = jnp.float32
_NEG_INF = float("-inf")


def _dot_nt(a, b):
    return lax.dot_general(a, b, (((1,), (1,)), ((), ())), preferred_element_type=_F32)


def _dot(a, b):
    return jnp.dot(a, b, preferred_element_type=_F32)


def _silu(g):
    return g * jax.nn.sigmoid(g)


def _rms_matmul_kernel(x_ref, g_ref, w_ref, *rest, has_col_scale):
    cs_ref, o_ref, h_scr = rest if has_col_scale else (None,) + rest

    @pl.when(pl.program_id(1) == 0)
    def _():
        x = x_ref[...]
        ms = jnp.mean(x * x, axis=-1, keepdims=True)
        h_scr[...] = (x * lax.rsqrt(ms + EPS) * g_ref[...]).astype(_BF16)

    out = _dot(h_scr[...], w_ref[...])
    if has_col_scale:
        out = out * cs_ref[...]
    o_ref[...] = out.astype(o_ref.dtype)


def rms_matmul(x, g, w, col_scale=None, *, tm, tn, out_dtype=_BF16):
    m, k = x.shape
    n = w.shape[1]
    assert m % tm == 0 and n % tn == 0
    has_cs = col_scale is not None
    cs_specs = [pl.BlockSpec((1, tn), lambda i, j: (0, j))] if has_cs else []
    cs_args = [col_scale.reshape(1, n)] if has_cs else []
    return pl.pallas_call(
        functools.partial(_rms_matmul_kernel, has_col_scale=has_cs),
        grid=(m // tm, n // tn),
        in_specs=[
            pl.BlockSpec((tm, k), lambda i, j: (i, 0)),
            pl.BlockSpec((1, k), lambda i, j: (0, 0)),
            pl.BlockSpec((k, tn), lambda i, j: (0, j)),
        ] + cs_specs,
        out_specs=pl.BlockSpec((tm, tn), lambda i, j: (i, j)),
        out_shape=jax.ShapeDtypeStruct((m, n), out_dtype),
        scratch_shapes=[pltpu.VMEM((tm, k), _BF16)],
        compiler_params=pltpu.CompilerParams(
            dimension_semantics=("parallel", "arbitrary"),
            vmem_limit_bytes=_VMEM_LIMIT),
        name="rms_matmul",
    )(x, g.reshape(1, k), w, *cs_args)


_LOG2E = math.log2(math.e)
_ONES_ROWS = 16
_VT_ROWS = HEAD_DIM + _ONES_ROWS


def _first_flash_step(z, rb, vt, acc_ref):
    m = jnp.max(z, axis=0, keepdims=True) + rb
    p = jnp.exp2(z - (m - rb))
    acc_ref[...] = _dot(vt, p.astype(_BF16))
    return m


def _flash_step(z_ref, z_max, rb, vt, m, acc_ref):
    m_new = jnp.maximum(m, z_max + rb)
    p = jnp.exp2(z_ref[...] - (m_new - rb))
    acc_ref[...] = jnp.exp2(m - m_new) * acc_ref[...] + _dot(vt, p.astype(_BF16))
    return m_new


_BIAS_LANES = 3


def _bias_key_lanes(lane, lane0):
    pos = lax.broadcasted_iota(jnp.int32, lane.shape, 0).astype(_F32)
    return jnp.where((lane >= lane0) & (lane < lane0 + _BIAS_LANES), pos, 0.0)


def _bias_query_lanes(slope, lane, lane0):
    rest = jnp.full(lane.shape, slope, _F32)
    out = jnp.zeros(lane.shape, _F32)
    for t in range(_BIAS_LANES):
        piece = rest.astype(_BF16).astype(_F32)
        out = jnp.where(lane == lane0 + t, piece, out)
        rest = rest - piece
    return out


def _fill_values_transposed(v_ref, vt_scr, g, tk):
    for n in range(vt_scr.shape[1]):
        vt_scr[g, n, :HEAD_DIM, :] = v_ref[n * tk:(n + 1) * tk, _head_lanes(g)].T
        vt_scr[g, n, HEAD_DIM:, :] = jnp.ones((_ONES_ROWS, tk), _BF16)


def _normalised(acc):
    return acc[:HEAD_DIM] / acc[HEAD_DIM:HEAD_DIM + 1]


def _pipelined_flash(n_streams, n_past, scores, first_update, update, sa_scr, sb_scr):
    streams = range(n_streams)
    for n in streams:
        scores(n, n_past, sb_scr)
    max_a = tuple(scores(n, 0, sa_scr) for n in streams)
    running = tuple(first_update(n, sb_scr) for n in streams)

    def pair(t, carry):
        running, max_a = (list(c) for c in carry)
        j0 = 2 * t
        j1 = j0 + 1
        j2 = jnp.minimum(j1 + 1, n_past)
        max_b = [None] * n_streams
        for n in streams:
            max_b[n] = scores(n, j1, sb_scr)
            running[n] = update(n, j0, sa_scr, max_a[n], running[n])
        for n in streams:
            max_a[n] = scores(n, j2, sa_scr)
            running[n] = update(n, j1, sb_scr, max_b[n], running[n], valid=j1 < n_past)
        return tuple(running), tuple(max_a)

    lax.fori_loop(0, (n_past + 1) // 2, pair, (running, max_a))


def _head_lanes(g):
    return slice(g * HEAD_DIM, (g + 1) * HEAD_DIM)


def _head_group_spec(rows, tq, width, first_blk, hps):
    assert first_blk % hps == 0
    index_map = lambda bi, hi, i: (bi, i if rows == tq else 0, first_blk // hps + hi)
    if rows == tq:
        return pl.BlockSpec((None, rows, width), index_map)
    return pl.BlockSpec((None, rows, width), index_map, pipeline_mode=pl.Buffered(1))


def _diff_attention_kernel(slopes_ref, lq1_ref, lk1_ref, lq2_ref, lk2_ref, subg_ref,
                           q_ref, k_ref, v_ref, gate_ref, o_ref,
                           vt_scr, ka_scr, acc_scr, sa_scr, sb_scr,
                           *, lam_init, tq, heads_per_step):
    i = pl.program_id(2)
    heads = range(heads_per_step)
    lane = lax.broadcasted_iota(jnp.int32, (tq, HEAD_DIM), 1)
    own_half = (lane < DIFF_HALF, lane >= DIFF_HALF)
    bias_lane0 = (DIFF_HALF, 0)

    @pl.when(i == 0)
    def _():
        for g in heads:
            _fill_values_transposed(v_ref, vt_scr, g, tq)
            for n in range(ka_scr.shape[2]):
                k = k_ref[n * tq:(n + 1) * tq, _head_lanes(g)].astype(_F32)
                for c in range(2):
                    ka_scr[g, c, n] = jnp.where(
                        own_half[c], k, _bias_key_lanes(lane, bias_lane0[c])).astype(_BF16)

    key_idx = lax.broadcasted_iota(jnp.int32, (tq, tq), 0)
    qry_idx = lax.broadcasted_iota(jnp.int32, (tq, tq), 1)
    qry_pos = lax.broadcasted_iota(jnp.int32, (1, tq), 1).astype(_F32)

    streams = []
    for g in heads:
        slope = slopes_ref[pl.program_id(1) * heads_per_step + g] * _LOG2E
        q = q_ref[:, _head_lanes(g)].astype(_F32)
        for c in range(2):
            qc = jnp.where(own_half[c], q, _bias_query_lanes(slope, lane, bias_lane0[c]))
            streams.append((g, c, slope, qc.astype(_BF16)))

    def scores(n, j, s_buf):
        g, c, _, qc = streams[n]
        s = _dot_nt(ka_scr[g, c, j], qc)
        s_buf[n] = s
        return jnp.max(s, axis=0, keepdims=True)

    def first_update(n, s_buf):
        g, c, slope, _ = streams[n]
        z = jnp.where(key_idx <= qry_idx, s_buf[n], _NEG_INF)
        return _first_flash_step(z, -slope * qry_pos, vt_scr[g, i], acc_scr.at[g, c])

    def update(n, j, s_buf, tile_max, running_max, valid=None):
        g, c, slope, _ = streams[n]
        rb = -slope * qry_pos - slope * ((i - j) * tq).astype(_F32)
        if valid is not None:
            rb = jnp.where(valid, rb, _NEG_INF)
        return _flash_step(s_buf.at[n], tile_max, rb, vt_scr[g, j], running_max,
                           acc_scr.at[g, c])

    _pipelined_flash(len(streams), i, scores, first_update, update, sa_scr, sb_scr)

    lam = (jnp.exp(jnp.sum(lq1_ref[...] * lk1_ref[...], axis=-1, keepdims=True))
           - jnp.exp(jnp.sum(lq2_ref[...] * lk2_ref[...], axis=-1, keepdims=True))
           + lam_init)
    for g in heads:
        o_t = _normalised(acc_scr[g, 0]) - lam * _normalised(acc_scr[g, 1])
        ms = jnp.mean(o_t * o_t, axis=0, keepdims=True)
        o = (o_t * lax.rsqrt(ms + EPS)).T * subg_ref[...] * (1.0 - lam_init)
        gate = gate_ref[:, _head_lanes(g)].astype(_F32)
        o_ref[:, _head_lanes(g)] = (o * _silu(gate)).astype(o_ref.dtype)


def diff_attention(proj, slopes, lq1, lk1, lq2, lk2, subg, *, lam_init, tq, heads_per_step):
    b, s, _ = proj.shape
    hps = heads_per_step
    assert s % tq == 0 and DIFF_HEADS % hps == 0
    width = hps * HEAD_DIM
    vec = lambda n: pl.BlockSpec((1, n), lambda bi, hi, i: (0, 0))
    spec = functools.partial(_head_group_spec, tq=tq, width=width, hps=hps)
    scratch = [pltpu.VMEM((hps, s // tq, _VT_ROWS, tq), _BF16),
               pltpu.VMEM((hps, 2, s // tq, tq, HEAD_DIM), _BF16),
               pltpu.VMEM((hps, 2, _VT_ROWS, tq), _F32),
               pltpu.VMEM((2 * hps, tq, tq), _F32),
               pltpu.VMEM((2 * hps, tq, tq), _F32)]
    return pl.pallas_call(
        functools.partial(_diff_attention_kernel, lam_init=lam_init, tq=tq,
                          heads_per_step=hps),
        grid=(b, DIFF_HEADS // hps, s // tq),
        in_specs=[
            pl.BlockSpec(memory_space=pltpu.SMEM),
            vec(DIFF_HALF), vec(DIFF_HALF), vec(DIFF_HALF), vec(DIFF_HALF), vec(HEAD_DIM),
            spec(tq, first_blk=_DQ_BLK), spec(s, first_blk=_DK_BLK), spec(s, first_blk=_DV_BLK),
            spec(tq, first_blk=_GATE_BLK),
        ],
        out_specs=pl.BlockSpec((None, tq, width), lambda bi, hi, i: (bi, i, hi)),
        out_shape=jax.ShapeDtypeStruct((b, s, DIFF_WIDTH), _BF16),
        scratch_shapes=scratch,
        compiler_params=pltpu.CompilerParams(
            dimension_semantics=("parallel", "parallel", "arbitrary"),
            vmem_limit_bytes=_VMEM_LIMIT),
        name="diff_attention",
    )(slopes, lq1.reshape(1, -1), lk1.reshape(1, -1), lq2.reshape(1, -1), lk2.reshape(1, -1),
      subg.reshape(1, -1), proj, proj, proj, proj)


def _moba_attention_kernel(slopes_ref, q_ref, k_ref, v_ref, gate_ref, o_ref,
                           kmean_scr, vt_scr, sel_scr, acc_scr, sa_scr, sb_scr,
                           *, n_blocks, heads_per_step):
    i = pl.program_id(2)
    tq = MOBA_BLOCK
    heads = range(heads_per_step)

    lane = lax.broadcasted_iota(jnp.int32, (tq, HEAD_DIM), 1)

    @pl.when(i == 0)
    def _():
        for g in heads:
            _fill_values_transposed(v_ref, vt_scr, g, tq)
            for n in range(n_blocks):
                k = k_ref[n * MOBA_BLOCK:(n + 1) * MOBA_BLOCK, _head_lanes(g)]
                kmean_scr[g, n:n + 1, :] = (jnp.sum(k.astype(_F32), axis=0, keepdims=True)
                                            * (1.0 / MOBA_BLOCK))

    key_idx = lax.broadcasted_iota(jnp.int32, (tq, tq), 0)
    qry_idx = lax.broadcasted_iota(jnp.int32, (tq, tq), 1)
    qry_pos = lax.broadcasted_iota(jnp.int32, (1, tq), 1).astype(_F32)
    blk = lax.broadcasted_iota(jnp.int32, (n_blocks, tq), 0)
    past = blk < i

    slopes = [slopes_ref[pl.program_id(1) * heads_per_step + g] * _LOG2E for g in heads]
    qs = [q_ref[:, _head_lanes(g)] for g in heads]
    qas = [jnp.concatenate([qs[g], _bias_query_lanes(slopes[g], lane, 0).astype(_BF16)], axis=1)
           for g in heads]
    key_pos_lanes = _bias_key_lanes(lane, 0).astype(_BF16)

    def scores(g, j, s_buf):
        ka = jnp.concatenate([k_ref[pl.ds(j * tq, tq), _head_lanes(g)], key_pos_lanes], axis=1)
        s = _dot_nt(ka, qas[g])
        s_buf[g] = s
        return jnp.max(s, axis=0, keepdims=True)

    def select_blocks(g):
        gate = lax.dot_general(kmean_scr[g], qs[g].astype(_F32), (((1,), (1,)), ((), ())),
                               precision=lax.Precision.HIGHEST,
                               preferred_element_type=_F32)
        gate = jnp.where(past, gate, _NEG_INF)
        rank = jnp.zeros(gate.shape, jnp.int32)
        for n in range(n_blocks):
            g_n = gate[n:n + 1, :]
            beats = (g_n > gate) | ((g_n == gate) & (n < blk))
            rank = rank + beats.astype(jnp.int32)
        sel_scr[g] = jnp.where(past & (rank < MOBA_TOPK), 0.0, _NEG_INF)

    def first_update(g, s_buf):
        select_blocks(g)
        z = jnp.where(key_idx <= qry_idx, s_buf[g], _NEG_INF)
        return _first_flash_step(z, -slopes[g] * qry_pos, vt_scr[g, i], acc_scr.at[g])

    def update(g, j, s_buf, tile_max, running_max, valid=None):
        rb = (-slopes[g] * qry_pos - slopes[g] * ((i - j) * tq).astype(_F32)
              + sel_scr[g, pl.ds(j, 1), :])
        return _flash_step(s_buf.at[g], tile_max, rb, vt_scr[g, j], running_max, acc_scr.at[g])

    _pipelined_flash(heads_per_step, i, scores, first_update, update, sa_scr, sb_scr)
    for g in heads:
        o = _normalised(acc_scr[g]).T
        gate = gate_ref[:, _head_lanes(g)].astype(_F32)
        o_ref[:, _head_lanes(g)] = (o * _silu(gate)).astype(o_ref.dtype)


def moba_attention(proj, slopes, *, heads_per_step):
    b, s, _ = proj.shape
    hps = heads_per_step
    assert s % MOBA_BLOCK == 0 and MOBA_HEADS % hps == 0
    n_blocks = s // MOBA_BLOCK
    tq = MOBA_BLOCK
    width = hps * HEAD_DIM
    spec = functools.partial(_head_group_spec, tq=tq, width=width, hps=hps)
    return pl.pallas_call(
        functools.partial(_moba_attention_kernel, n_blocks=n_blocks, heads_per_step=hps),
        grid=(b, MOBA_HEADS // hps, n_blocks),
        in_specs=[
            pl.BlockSpec(memory_space=pltpu.SMEM),
            spec(tq, first_blk=_MQ_BLK), spec(s, first_blk=_MK_BLK), spec(s, first_blk=_MV_BLK),
            spec(tq, first_blk=_GATE_BLK + DIFF_HEADS),
        ],
        out_specs=pl.BlockSpec((None, tq, width), lambda bi, hi, i: (bi, i, hi)),
        out_shape=jax.ShapeDtypeStruct((b, s, MOBA_WIDTH), _BF16),
        scratch_shapes=[pltpu.VMEM((hps, n_blocks, HEAD_DIM), _F32),
                        pltpu.VMEM((hps, n_blocks, _VT_ROWS, tq), _BF16),
                        pltpu.VMEM((hps, n_blocks, tq), _F32),
                        pltpu.VMEM((hps, _VT_ROWS, tq), _F32),
                        pltpu.VMEM((hps, tq, tq), _F32),
                        pltpu.VMEM((hps, tq, tq), _F32)],
        compiler_params=pltpu.CompilerParams(
            dimension_semantics=("parallel", "parallel", "arbitrary"),
            vmem_limit_bytes=_VMEM_LIMIT),
        name="moba_attention",
    )(slopes, proj, proj, proj, proj)


def _layer_tail_kernel(yd_ref, ym_ref, x_ref, wa_ref, wb_ref, g_ref, wq_ref, kv_ref, wo_ref,
                       fg_ref, o_ref, q_scr, o_scr, *, apply_final):
    x = x_ref[...] + _dot(yd_ref[...], wa_ref[...]) + _dot(ym_ref[...], wb_ref[...])
    ms = jnp.mean(x * x, axis=-1, keepdims=True)
    h = (x * lax.rsqrt(ms + EPS) * g_ref[...]).astype(_BF16)
    q_scr[...] = (_dot(h, wq_ref[...]) * (HEAD_DIM ** -0.5 * _LOG2E)).astype(_BF16)
    for hh in range(MEM_HEADS):
        lo, hi = hh * HEAD_DIM, (hh + 1) * HEAD_DIM
        s = _dot_nt(q_scr[:, lo:hi], kv_ref[:, lo:hi])
        p = jnp.exp2(s - jnp.max(s, axis=-1, keepdims=True))
        l = jnp.sum(p, axis=-1, keepdims=True)
        o_h = _dot(p.astype(_BF16), kv_ref[:, MEM_WIDTH + lo:MEM_WIDTH + hi]) / l
        o_scr[:, lo:hi] = o_h.astype(_BF16)
    x = x + _dot(o_scr[...], wo_ref[...])
    if apply_final:
        ms = jnp.mean(x * x, axis=-1, keepdims=True)
        x = x * lax.rsqrt(ms + EPS) * fg_ref[...]
    o_ref[...] = x


def layer_tail(yd, ym, x, w_out, g, wq, kv, wo, final_g, *, apply_final, tm):
    b, s, d = x.shape
    mlen = kv.shape[1]
    ka, kb = yd.shape[-1], ym.shape[-1]
    assert s % tm == 0 and ka == kb and w_out.shape[0] == ka + kb
    rows = lambda width: pl.BlockSpec((None, tm, width), lambda bi, i: (bi, i, 0))
    whole = lambda shape: pl.BlockSpec(shape, lambda bi, i: (0,) * len(shape))
    return pl.pallas_call(
        functools.partial(_layer_tail_kernel, apply_final=apply_final),
        grid=(b, s // tm),
        in_specs=[
            rows(ka), rows(kb), rows(d),
            pl.BlockSpec((ka, d), lambda bi, i: (0, 0)),
            pl.BlockSpec((kb, d), lambda bi, i: (1, 0)),
            whole((1, d)), whole((d, MEM_WIDTH)),
            pl.BlockSpec((None, mlen, 2 * MEM_WIDTH), lambda bi, i: (bi, 0, 0)),
            whole((MEM_WIDTH, d)), whole((1, d)),
        ],
        out_specs=rows(d),
        out_shape=jax.ShapeDtypeStruct((b, s, d), _F32),
        scratch_shapes=[pltpu.VMEM((tm, MEM_WIDTH), _BF16),
                        pltpu.VMEM((tm, MEM_WIDTH), _BF16)],
        compiler_params=pltpu.CompilerParams(
            dimension_semantics=("parallel", "parallel"),
            vmem_limit_bytes=_VMEM_LIMIT),
        name="layer_tail",
    )(yd, ym, x, w_out, w_out, g.reshape(1, d), wq, kv, wo, final_g.reshape(1, d))


def _in_proj_col_scale(n_cols):
    cs = np.ones((n_cols,), np.float32)
    cs[_DQ_BLK * HEAD_DIM:_DK_BLK * HEAD_DIM] = DIFF_HALF ** -0.5 * _LOG2E
    cs[_MQ_BLK * HEAD_DIM:_MK_BLK * HEAD_DIM] = HEAD_DIM ** -0.5 * _LOG2E
    return jnp.asarray(cs)


def _alibi_slopes(n):
    return jnp.asarray(2.0 ** (-8.0 * np.arange(1, n + 1) / n), dtype=_F32)


_IN_PROJ_TILE = (1024, 1024)
_ATTN_Q_TILE = 256
_DIFF_HEADS_PER_STEP = 4
_MOBA_HEADS_PER_STEP = 8
_TAIL_ROW_TILE = 256


def kernel(x, mem, norm_mix_g, w_in, lambda_q1, lambda_k1, lambda_q2, lambda_k2, subln_g,
           w_out, norm_mem_g, mem_norm_g, w_q_mem, w_kv_mem, w_o_mem, final_g):
    b, s, d = x.shape
    mlen = mem.shape[1]
    depth = w_in.shape[0]
    diff_slopes = _alibi_slopes(DIFF_HEADS)
    moba_slopes = _alibi_slopes(MOBA_HEADS)
    mem2 = mem.reshape(b * mlen, d)
    for l in range(depth):
        lam_init = 0.8 - 0.6 * math.exp(-0.3 * l)
        proj = rms_matmul(x.reshape(b * s, d), norm_mix_g[l], w_in[l].astype(_BF16),
                          _in_proj_col_scale(w_in.shape[-1]),
                          tm=_IN_PROJ_TILE[0], tn=_IN_PROJ_TILE[1]).reshape(b, s, -1)
        y_d = diff_attention(proj, diff_slopes, lambda_q1[l], lambda_k1[l], lambda_q2[l],
                             lambda_k2[l], subln_g[l], lam_init=lam_init, tq=_ATTN_Q_TILE,
                             heads_per_step=_DIFF_HEADS_PER_STEP)
        y_m = moba_attention(proj, moba_slopes, heads_per_step=_MOBA_HEADS_PER_STEP)
        kv = rms_matmul(mem2, mem_norm_g[l], w_kv_mem[l].astype(_BF16),
                        tm=b * mlen, tn=2 * MEM_WIDTH).reshape(b, mlen, 2 * MEM_WIDTH)
        x = layer_tail(y_d, y_m, x, w_out[l].astype(_BF16), norm_mem_g[l],
                       w_q_mem[l].astype(_BF16), kv, w_o_mem[l].astype(_BF16), final_g,
                       apply_final=(l == depth - 1), tm=_TAIL_ROW_TILE)
    return x
```

```python
import functools
import math

import jax
import jax.numpy as jnp
import numpy as np
from jax import lax
from jax.experimental import pallas as pl
from jax.experimental.pallas import tpu as pltpu

HEAD_DIM = 128
DIFF_HEADS = 8
DIFF_HALF = HEAD_DIM // 2
DIFF_WIDTH = DIFF_HEADS * HEAD_DIM
MOBA_HEADS = 8
MOBA_WIDTH = MOBA_HEADS * HEAD_DIM
MOBA_BLOCK = 256
MOBA_TOPK = 3
MEM_HEADS = 4
MEM_WIDTH = MEM_HEADS * HEAD_DIM
EPS = 1e-6

_DQ_BLK = 0
_DK_BLK = DIFF_HEADS
_DV_BLK = 2 * DIFF_HEADS
_MQ_BLK = 3 * DIFF_HEADS
_MK_BLK = 3 * DIFF_HEADS + MOBA_HEADS
_MV_BLK = 3 * DIFF_HEADS + 2 * MOBA_HEADS
_GATE_BLK = 3 * DIFF_HEADS + 3 * MOBA_HEADS

_V7X_VMEM_BYTES = 64 * 1024 * 1024
_VMEM_LIMIT = _V7X_VMEM_BYTES * 3 // 4

_BF16 = jnp.bfloat16
_F32 = jnp.float32
_NEG_INF = float("-inf")


def _dot_nt(a, b):
    return lax.dot_general(a, b, (((1,), (1,)), ((), ())), preferred_element_type=_F32)


def _dot(a, b):
    return jnp.dot(a, b, preferred_element_type=_F32)


def _silu(g):
    return g * jax.nn.sigmoid(g)


def _rms_matmul_kernel(x_ref, g_ref, w_ref, *rest, has_col_scale):
    cs_ref, o_ref, h_scr = rest if has_col_scale else (None,) + rest

    @pl.when(pl.program_id(1) == 0)
    def _():
        x = x_ref[...]
        ms = jnp.mean(x * x, axis=-1, keepdims=True)
        h_scr[...] = (x * lax.rsqrt(ms + EPS) * g_ref[...]).astype(_BF16)

    out = _dot(h_scr[...], w_ref[...])
    if has_col_scale:
        out = out * cs_ref[...]
    o_ref[...] = out.astype(o_ref.dtype)


def rms_matmul(x, g, w, col_scale=None, *, tm, tn, out_dtype=_BF16):
    m, k = x.shape
    n = w.shape[1]
    assert m % tm == 0 and n % tn == 0
    has_cs = col_scale is not None
    cs_specs = [pl.BlockSpec((1, tn), lambda i, j: (0, j))] if has_cs else []
    cs_args = [col_scale.reshape(1, n)] if has_cs else []
    return pl.pallas_call(
        functools.partial(_rms_matmul_kernel, has_col_scale=has_cs),
        grid=(m // tm, n // tn),
        in_specs=[
            pl.BlockSpec((tm, k), lambda i, j: (i, 0)),
            pl.BlockSpec((1, k), lambda i, j: (0, 0)),
            pl.BlockSpec((k, tn), lambda i, j: (0, j)),
        ] + cs_specs,
        out_specs=pl.BlockSpec((tm, tn), lambda i, j: (i, j)),
        out_shape=jax.ShapeDtypeStruct((m, n), out_dtype),
        scratch_shapes=[pltpu.VMEM((tm, k), _BF16)],
        compiler_params=pltpu.CompilerParams(
            dimension_semantics=("parallel", "arbitrary"),
            vmem_limit_bytes=_VMEM_LIMIT),
        name="rms_matmul",
    )(x, g.reshape(1, k), w, *cs_args)


_LOG2E = math.log2(math.e)
_ONES_ROWS = 16
_VT_ROWS = HEAD_DIM + _ONES_ROWS


_RUNNING_MAX_INIT = float(np.finfo(np.float32).min)


def _flash_step(z, z_max, rb, vt, m, acc_ref):
    m_new = jnp.maximum(m, z_max + rb)
    p = jnp.exp2(z[...] - (m_new - rb))
    acc_ref[...] = jnp.exp2(m - m_new) * acc_ref[...] + _dot(vt, p.astype(_BF16))
    return m_new


def _causal_flash_step(s, causal, rb, vt, m, acc_ref):
    z = jnp.where(causal, s, _NEG_INF)
    return _flash_step(z, jnp.max(z, axis=0, keepdims=True), rb, vt, m, acc_ref)


_BIAS_LANES = 3


def _bias_key_lanes(lane, lane0):
    pos = lax.broadcasted_iota(jnp.int32, lane.shape, 0).astype(_F32)
    return jnp.where((lane >= lane0) & (lane < lane0 + _BIAS_LANES), pos, 0.0)


def _bias_query_lanes(slope, lane, lane0):
    rest = jnp.full(lane.shape, slope, _F32)
    out = jnp.zeros(lane.shape, _F32)
    for t in range(_BIAS_LANES):
        piece = rest.astype(_BF16).astype(_F32)
        out = jnp.where(lane == lane0 + t, piece, out)
        rest = rest - piece
    return out


def _fill_values_transposed(v_ref, vt_scr, g, tk):
    for n in range(vt_scr.shape[1]):
        vt_scr[g, n, :HEAD_DIM, :] = v_ref[n * tk:(n + 1) * tk, _head_lanes(g)].T
        vt_scr[g, n, HEAD_DIM:, :] = jnp.ones((_ONES_ROWS, tk), _BF16)


def _normalised(acc):
    return acc[:HEAD_DIM] / acc[HEAD_DIM:HEAD_DIM + 1]


def _pipelined_flash(n_streams, n_past, tq, scores, update, diag_update, sa_scr, sb_scr,
                     overlap_with_first_scores=lambda: None):
    streams = range(n_streams)
    max_a = tuple(scores(n, 0, sa_scr) for n in streams)
    overlap_with_first_scores()
    running = tuple(jnp.full((1, tq), _RUNNING_MAX_INIT, _F32) for _ in streams)

    def pair(t, carry):
        running, max_a = (list(c) for c in carry)
        j0 = 2 * t
        max_b = [None] * n_streams
        for n in streams:
            max_b[n] = scores(n, j0 + 1, sb_scr)
            running[n] = update(n, j0, sa_scr, max_a[n], running[n])
        for n in streams:
            max_a[n] = scores(n, j0 + 2, sa_scr)
            running[n] = update(n, j0 + 1, sb_scr, max_b[n], running[n])
        return tuple(running), tuple(max_a)

    running, max_a = lax.fori_loop(0, n_past // 2, pair, (running, max_a))

    @pl.when(n_past % 2 == 0)
    def _():
        for n in streams:
            diag_update(n, sa_scr, running[n])

    @pl.when(n_past % 2 == 1)
    def _():
        for n in streams:
            scores(n, n_past, sb_scr)
            m = update(n, n_past - 1, sa_scr, max_a[n], running[n])
            diag_update(n, sb_scr, m)


def _head_lanes(g):
    return slice(g * HEAD_DIM, (g + 1) * HEAD_DIM)


def _head_group_spec(rows, tq, width, first_blk, hps):
    assert first_blk % hps == 0
    index_map = lambda bi, hi, i: (bi, i if rows == tq else 0, first_blk // hps + hi)
    if rows == tq:
        return pl.BlockSpec((None, rows, width), index_map)
    return pl.BlockSpec((None, rows, width), index_map, pipeline_mode=pl.Buffered(1))


def _diff_attention_kernel(slopes_ref, lq1_ref, lk1_ref, lq2_ref, lk2_ref, subg_ref,
                           q_ref, k_ref, v_ref, gate_ref, o_ref,
                           vt_scr, ka_scr, acc_scr, sa_scr, sb_scr,
                           *, lam_init, tq, heads_per_step):
    i = pl.program_id(2)
    heads = range(heads_per_step)
    lane = lax.broadcasted_iota(jnp.int32, (tq, HEAD_DIM), 1)
    own_half = (lane < DIFF_HALF, lane >= DIFF_HALF)
    bias_lane0 = (DIFF_HALF, 0)

    @pl.when(i == 0)
    def _():
        for g in heads:
            _fill_values_transposed(v_ref, vt_scr, g, tq)
            for n in range(ka_scr.shape[2]):
                k = k_ref[n * tq:(n + 1) * tq, _head_lanes(g)].astype(_F32)
                for c in range(2):
                    ka_scr[g, c, n] = jnp.where(
                        own_half[c], k, _bias_key_lanes(lane, bias_lane0[c])).astype(_BF16)

    key_idx = lax.broadcasted_iota(jnp.int32, (tq, tq), 0)
    qry_idx = lax.broadcasted_iota(jnp.int32, (tq, tq), 1)
    qry_pos = lax.broadcasted_iota(jnp.int32, (1, tq), 1).astype(_F32)

    streams = []
    for g in heads:
        slope = slopes_ref[pl.program_id(1) * heads_per_step + g] * _LOG2E
        q = q_ref[:, _head_lanes(g)].astype(_F32)
        for c in range(2):
            qc = jnp.where(own_half[c], q, _bias_query_lanes(slope, lane, bias_lane0[c]))
            streams.append((g, c, slope, qc.astype(_BF16)))

    def scores(n, j, s_buf):
        g, c, _, qc = streams[n]
        s = _dot_nt(ka_scr[g, c, j], qc)
        s_buf[n] = s
        return jnp.max(s, axis=0, keepdims=True)

    def update(n, j, s_buf, tile_max, running_max):
        g, c, slope, _ = streams[n]
        rb = -slope * qry_pos - slope * ((i - j) * tq).astype(_F32)
        return _flash_step(s_buf.at[n], tile_max, rb, vt_scr[g, j], running_max,
                           acc_scr.at[g, c])

    def diag_update(n, s_buf, running_max):
        g, c, slope, _ = streams[n]
        return _causal_flash_step(s_buf[n], key_idx <= qry_idx, -slope * qry_pos,
                                  vt_scr[g, i], running_max, acc_scr.at[g, c])

    acc_scr[...] = jnp.zeros(acc_scr.shape, _F32)
    _pipelined_flash(len(streams), i, tq, scores, update, diag_update, sa_scr, sb_scr)

    lam = (jnp.exp(jnp.sum(lq1_ref[...] * lk1_ref[...], axis=-1, keepdims=True))
           - jnp.exp(jnp.sum(lq2_ref[...] * lk2_ref[...], axis=-1, keepdims=True))
           + lam_init)
    for g in heads:
        o_t = _normalised(acc_scr[g, 0]) - lam * _normalised(acc_scr[g, 1])
        ms = jnp.mean(o_t * o_t, axis=0, keepdims=True)
        o = (o_t * lax.rsqrt(ms + EPS)).T * subg_ref[...] * (1.0 - lam_init)
        gate = gate_ref[:, _head_lanes(g)].astype(_F32)
        o_ref[:, _head_lanes(g)] = (o * _silu(gate)).astype(o_ref.dtype)


def diff_attention(proj, slopes, lq1, lk1, lq2, lk2, subg, *, lam_init, tq, heads_per_step):
    b, s, _ = proj.shape
    hps = heads_per_step
    assert s % tq == 0 and DIFF_HEADS % hps == 0
    width = hps * HEAD_DIM
    vec = lambda n: pl.BlockSpec((1, n), lambda bi, hi, i: (0, 0))
    spec = functools.partial(_head_group_spec, tq=tq, width=width, hps=hps)
    scratch = [pltpu.VMEM((hps, s // tq, _VT_ROWS, tq), _BF16),
               pltpu.VMEM((hps, 2, s // tq, tq, HEAD_DIM), _BF16),
               pltpu.VMEM((hps, 2, _VT_ROWS, tq), _F32),
               pltpu.VMEM((2 * hps, tq, tq), _F32),
               pltpu.VMEM((2 * hps, tq, tq), _F32)]
    return pl.pallas_call(
        functools.partial(_diff_attention_kernel, lam_init=lam_init, tq=tq,
                          heads_per_step=hps),
        grid=(b, DIFF_HEADS // hps, s // tq),
        in_specs=[
            pl.BlockSpec(memory_space=pltpu.SMEM),
            vec(DIFF_HALF), vec(DIFF_HALF), vec(DIFF_HALF), vec(DIFF_HALF), vec(HEAD_DIM),
            spec(tq, first_blk=_DQ_BLK), spec(s, first_blk=_DK_BLK), spec(s, first_blk=_DV_BLK),
            spec(tq, first_blk=_GATE_BLK),
        ],
        out_specs=pl.BlockSpec((None, tq, width), lambda bi, hi, i: (bi, i, hi)),
        out_shape=jax.ShapeDtypeStruct((b, s, DIFF_WIDTH), _BF16),
        scratch_shapes=scratch,
        compiler_params=pltpu.CompilerParams(
            dimension_semantics=("parallel", "parallel", "arbitrary"),
            vmem_limit_bytes=_VMEM_LIMIT),
        name="diff_attention",
    )(slopes, lq1.reshape(1, -1), lk1.reshape(1, -1), lq2.reshape(1, -1), lk2.reshape(1, -1),
      subg.reshape(1, -1), proj, proj, proj, proj)


def _moba_attention_kernel(slopes_ref, q_ref, k_ref, v_ref, gate_ref, o_ref,
                           kmean_scr, vt_scr, sel_scr, acc_scr, sa_scr, sb_scr,
                           *, n_blocks, heads_per_step):
    i = pl.program_id(2)
    tq = MOBA_BLOCK
    heads = range(heads_per_step)

    lane = lax.broadcasted_iota(jnp.int32, (tq, HEAD_DIM), 1)

    @pl.when(i == 0)
    def _():
        for g in heads:
            _fill_values_transposed(v_ref, vt_scr, g, tq)
            for n in range(n_blocks):
                k = k_ref[n * MOBA_BLOCK:(n + 1) * MOBA_BLOCK, _head_lanes(g)]
                kmean_scr[g, n:n + 1, :] = (jnp.sum(k.astype(_F32), axis=0, keepdims=True)
                                            * (1.0 / MOBA_BLOCK))

    key_idx = lax.broadcasted_iota(jnp.int32, (tq, tq), 0)
    qry_idx = lax.broadcasted_iota(jnp.int32, (tq, tq), 1)
    qry_pos = lax.broadcasted_iota(jnp.int32, (1, tq), 1).astype(_F32)
    blk = lax.broadcasted_iota(jnp.int32, (n_blocks, tq), 0)
    past = blk < i

    slopes = [slopes_ref[pl.program_id(1) * heads_per_step + g] * _LOG2E for g in heads]
    qs = [q_ref[:, _head_lanes(g)] for g in heads]
    qas = [jnp.concatenate([qs[g], _bias_query_lanes(slopes[g], lane, 0).astype(_BF16)], axis=1)
           for g in heads]
    key_pos_lanes = _bias_key_lanes(lane, 0).astype(_BF16)

    def scores(g, j, s_buf):
        ka = jnp.concatenate([k_ref[pl.ds(j * tq, tq), _head_lanes(g)], key_pos_lanes], axis=1)
        s = _dot_nt(ka, qas[g])
        s_buf[g] = s
        return jnp.max(s, axis=0, keepdims=True)

    def select_blocks(g):
        gate = lax.dot_general(kmean_scr[g], qs[g].astype(_F32), (((1,), (1,)), ((), ())),
                               precision=lax.Precision.HIGHEST,
                               preferred_element_type=_F32)
        gate = jnp.where(past, gate, _NEG_INF)
        rank = jnp.zeros(gate.shape, jnp.int32)
        for n in range(n_blocks):
            g_n = gate[n:n + 1, :]
            beats = (g_n > gate) | ((g_n == gate) & (n < blk))
            rank = rank + beats.astype(jnp.int32)
        sel_scr[g] = jnp.where(past & (rank < MOBA_TOPK), 0.0, _NEG_INF)

    def update(g, j, s_buf, tile_max, running_max):
        rb = (-slopes[g] * qry_pos - slopes[g] * ((i - j) * tq).astype(_F32)
              + sel_scr[g, pl.ds(j, 1), :])
        return _flash_step(s_buf.at[g], tile_max, rb, vt_scr[g, j], running_max, acc_scr.at[g])

    def diag_update(g, s_buf, running_max):
        return _causal_flash_step(s_buf[g], key_idx <= qry_idx, -slopes[g] * qry_pos,
                                  vt_scr[g, i], running_max, acc_scr.at[g])

    def select_all_blocks():
        for g in heads:
            select_blocks(g)

    acc_scr[...] = jnp.zeros(acc_scr.shape, _F32)
    _pipelined_flash(heads_per_step, i, tq, scores, update, diag_update, sa_scr, sb_scr,
                     overlap_with_first_scores=select_all_blocks)
    for g in heads:
        o = _normalised(acc_scr[g]).T
        gate = gate_ref[:, _head_lanes(g)].astype(_F32)
        o_ref[:, _head_lanes(g)] = (o * _silu(gate)).astype(o_ref.dtype)


def moba_attention(proj, slopes, *, heads_per_step):
    b, s, _ = proj.shape
    hps = heads_per_step
    assert s % MOBA_BLOCK == 0 and MOBA_HEADS % hps == 0
    n_blocks = s // MOBA_BLOCK
    tq = MOBA_BLOCK
    width = hps * HEAD_DIM
    spec = functools.partial(_head_group_spec, tq=tq, width=width, hps=hps)
    return pl.pallas_call(
        functools.partial(_moba_attention_kernel, n_blocks=n_blocks, heads_per_step=hps),
        grid=(b, MOBA_HEADS // hps, n_blocks),
        in_specs=[
            pl.BlockSpec(memory_space=pltpu.SMEM),
            spec(tq, first_blk=_MQ_BLK), spec(s, first_blk=_MK_BLK), spec(s, first_blk=_MV_BLK),
            spec(tq, first_blk=_GATE_BLK + DIFF_HEADS),
        ],
        out_specs=pl.BlockSpec((None, tq, width), lambda bi, hi, i: (bi, i, hi)),
        out_shape=jax.ShapeDtypeStruct((b, s, MOBA_WIDTH), _BF16),
        scratch_shapes=[pltpu.VMEM((hps, n_blocks, HEAD_DIM), _F32),
                        pltpu.VMEM((hps, n_blocks, _VT_ROWS, tq), _BF16),
                        pltpu.VMEM((hps, n_blocks, tq), _F32),
                        pltpu.VMEM((hps, _VT_ROWS, tq), _F32),
                        pltpu.VMEM((hps, tq, tq), _F32),
                        pltpu.VMEM((hps, tq, tq), _F32)],
        compiler_params=pltpu.CompilerParams(
            dimension_semantics=("parallel", "parallel", "arbitrary"),
            vmem_limit_bytes=_VMEM_LIMIT),
        name="moba_attention",
    )(slopes, proj, proj, proj, proj)


def _layer_tail_kernel(yd_ref, ym_ref, x_ref, wa_ref, wb_ref, g_ref, wq_ref, kv_ref, wo_ref,
                       fg_ref, o_ref, q_scr, o_scr, *, apply_final):
    x = x_ref[...] + _dot(yd_ref[...], wa_ref[...]) + _dot(ym_ref[...], wb_ref[...])
    ms = jnp.mean(x * x, axis=-1, keepdims=True)
    h = (x * lax.rsqrt(ms + EPS) * g_ref[...]).astype(_BF16)
    q_scr[...] = (_dot(h, wq_ref[...]) * (HEAD_DIM ** -0.5 * _LOG2E)).astype(_BF16)
    for hh in range(MEM_HEADS):
        lo, hi = hh * HEAD_DIM, (hh + 1) * HEAD_DIM
        s = _dot_nt(q_scr[:, lo:hi], kv_ref[:, lo:hi])
        p = jnp.exp2(s - jnp.max(s, axis=-1, keepdims=True))
        l = jnp.sum(p, axis=-1, keepdims=True)
        o_h = _dot(p.astype(_BF16), kv_ref[:, MEM_WIDTH + lo:MEM_WIDTH + hi]) / l
        o_scr[:, lo:hi] = o_h.astype(_BF16)
    x = x + _dot(o_scr[...], wo_ref[...])
    if apply_final:
        ms = jnp.mean(x * x, axis=-1, keepdims=True)
        x = x * lax.rsqrt(ms + EPS) * fg_ref[...]
    o_ref[...] = x


def layer_tail(yd, ym, x, w_out, g, wq, kv, wo, final_g, *, apply_final, tm):
    b, s, d = x.shape
    mlen = kv.shape[1]
    ka, kb = yd.shape[-1], ym.shape[-1]
    assert s % tm == 0 and ka == kb and w_out.shape[0] == ka + kb
    rows = lambda width: pl.BlockSpec((None, tm, width), lambda bi, i: (bi, i, 0))
    whole = lambda shape: pl.BlockSpec(shape, lambda bi, i: (0,) * len(shape))
    return pl.pallas_call(
        functools.partial(_layer_tail_kernel, apply_final=apply_final),
        grid=(b, s // tm),
        in_specs=[
            rows(ka), rows(kb), rows(d),
            pl.BlockSpec((ka, d), lambda bi, i: (0, 0)),
            pl.BlockSpec((kb, d), lambda bi, i: (1, 0)),
            whole((1, d)), whole((d, MEM_WIDTH)),
            pl.BlockSpec((None, mlen, 2 * MEM_WIDTH), lambda bi, i: (bi, 0, 0)),
            whole((MEM_WIDTH, d)), whole((1, d)),
        ],
        out_specs=rows(d),
        out_shape=jax.ShapeDtypeStruct((b, s, d), _F32),
        scratch_shapes=[pltpu.VMEM((tm, MEM_WIDTH), _BF16),
                        pltpu.VMEM((tm, MEM_WIDTH), _BF16)],
        compiler_params=pltpu.CompilerParams(
            dimension_semantics=("parallel", "parallel"),
            vmem_limit_bytes=_VMEM_LIMIT),
        name="layer_tail",
    )(yd, ym, x, w_out, w_out, g.reshape(1, d), wq, kv, wo, final_g.reshape(1, d))


def _in_proj_col_scale(n_cols):
    cs = np.ones((n_cols,), np.float32)
    cs[_DQ_BLK * HEAD_DIM:_DK_BLK * HEAD_DIM] = DIFF_HALF ** -0.5 * _LOG2E
    cs[_MQ_BLK * HEAD_DIM:_MK_BLK * HEAD_DIM] = HEAD_DIM ** -0.5 * _LOG2E
    return jnp.asarray(cs)


def _alibi_slopes(n):
    return jnp.asarray(2.0 ** (-8.0 * np.arange(1, n + 1) / n), dtype=_F32)


_IN_PROJ_TILE = (1024, 1024)
_ATTN_Q_TILE = 256
_DIFF_HEADS_PER_STEP = 4
_MOBA_HEADS_PER_STEP = 8
_TAIL_ROW_TILE = 256


def kernel(x, mem, norm_mix_g, w_in, lambda_q1, lambda_k1, lambda_q2, lambda_k2, subln_g,
           w_out, norm_mem_g, mem_norm_g, w_q_mem, w_kv_mem, w_o_mem, final_g):
    b, s, d = x.shape
    mlen = mem.shape[1]
    depth = w_in.shape[0]
    diff_slopes = _alibi_slopes(DIFF_HEADS)
    moba_slopes = _alibi_slopes(MOBA_HEADS)
    mem2 = mem.reshape(b * mlen, d)
    for l in range(depth):
        lam_init = 0.8 - 0.6 * math.exp(-0.3 * l)
        proj = rms_matmul(x.reshape(b * s, d), norm_mix_g[l], w_in[l].astype(_BF16),
                          _in_proj_col_scale(w_in.shape[-1]),
                          tm=_IN_PROJ_TILE[0], tn=_IN_PROJ_TILE[1]).reshape(b, s, -1)
        y_d = diff_attention(proj, diff_slopes, lambda_q1[l], lambda_k1[l], lambda_q2[l],
                             lambda_k2[l], subln_g[l], lam_init=lam_init, tq=_ATTN_Q_TILE,
                             heads_per_step=_DIFF_HEADS_PER_STEP)
        y_m = moba_attention(proj, moba_slopes, heads_per_step=_MOBA_HEADS_PER_STEP)
        kv = rms_matmul(mem2, mem_norm_g[l], w_kv_mem[l].astype(_BF16),
                        tm=b * mlen, tn=2 * MEM_WIDTH).reshape(b, mlen, 2 * MEM_WIDTH)
        x = layer_tail(y_d, y_m, x, w_out[l].astype(_BF16), norm_mem_g[l],
                       w_q_mem[l].astype(_BF16), kv, w_o_mem[l].astype(_BF16), final_g,
                       apply_final=(l == depth - 1), tm=_TAIL_ROW_TILE)
    return x
```

```python
import functools
import math

import jax
import jax.numpy as jnp
import numpy as np
from jax import lax
from jax.experimental import pallas as pl
from jax.experimental.pallas import tpu as pltpu

HEAD_DIM = 128
DIFF_HEADS = 8
DIFF_HALF = HEAD_DIM // 2
DIFF_WIDTH = DIFF_HEADS * HEAD_DIM
MOBA_HEADS = 8
MOBA_WIDTH = MOBA_HEADS * HEAD_DIM
MOBA_BLOCK = 256
MOBA_TOPK = 3
MEM_HEADS = 4
MEM_WIDTH = MEM_HEADS * HEAD_DIM
EPS = 1e-6

_DQ_BLK = 0
_DK_BLK = DIFF_HEADS
_DV_BLK = 2 * DIFF_HEADS
_MQ_BLK = 3 * DIFF_HEADS
_MK_BLK = 3 * DIFF_HEADS + MOBA_HEADS
_MV_BLK = 3 * DIFF_HEADS + 2 * MOBA_HEADS
_GATE_BLK = 3 * DIFF_HEADS + 3 * MOBA_HEADS

_V7X_VMEM_BYTES = 64 * 1024 * 1024
_VMEM_LIMIT = _V7X_VMEM_BYTES * 3 // 4

_BF16 = jnp.bfloat16
_F32 = jnp.float32
_NEG_INF = float("-inf")


def _dot_nt(a, b):
    return lax.dot_general(a, b, (((1,), (1,)), ((), ())), preferred_element_type=_F32)


def _dot(a, b):
    return jnp.dot(a, b, preferred_element_type=_F32)


def _silu(g):
    return g * jax.nn.sigmoid(g)


def _rms_matmul_kernel(x_ref, g_ref, w_ref, *rest, has_col_scale):
    cs_ref, o_ref, h_scr = rest if has_col_scale else (None,) + rest

    @pl.when(pl.program_id(1) == 0)
    def _():
        x = x_ref[...]
        ms = jnp.mean(x * x, axis=-1, keepdims=True)
        h_scr[...] = (x * lax.rsqrt(ms + EPS) * g_ref[...]).astype(_BF16)

    out = _dot(h_scr[...], w_ref[...])
    if has_col_scale:
        out = out * cs_ref[...]
    o_ref[...] = out.astype(o_ref.dtype)


def rms_matmul(x, g, w, col_scale=None, *, tm, tn, out_dtype=_BF16):
    m, k = x.shape
    n = w.shape[1]
    assert m % tm == 0 and n % tn == 0
    has_cs = col_scale is not None
    cs_specs = [pl.BlockSpec((1, tn), lambda i, j: (0, j))] if has_cs else []
    cs_args = [col_scale.reshape(1, n)] if has_cs else []
    return pl.pallas_call(
        functools.partial(_rms_matmul_kernel, has_col_scale=has_cs),
        grid=(m // tm, n // tn),
        in_specs=[
            pl.BlockSpec((tm, k), lambda i, j: (i, 0)),
            pl.BlockSpec((1, k), lambda i, j: (0, 0)),
            pl.BlockSpec((k, tn), lambda i, j: (0, j)),
        ] + cs_specs,
        out_specs=pl.BlockSpec((tm, tn), lambda i, j: (i, j)),
        out_shape=jax.ShapeDtypeStruct((m, n), out_dtype),
        scratch_shapes=[pltpu.VMEM((tm, k), _BF16)],
        compiler_params=pltpu.CompilerParams(
            dimension_semantics=("parallel", "arbitrary"),
            vmem_limit_bytes=_VMEM_LIMIT),
        name="rms_matmul",
    )(x, g.reshape(1, k), w, *cs_args)


_LOG2E = math.log2(math.e)
_ONES_ROWS = 16
_VT_ROWS = HEAD_DIM + _ONES_ROWS


def _first_flash_step(z, rb, vt, acc_ref):
    m = jnp.max(z, axis=0, keepdims=True) + rb
    p = jnp.exp2(z - (m - rb))
    acc_ref[...] = _dot(vt, p.astype(_BF16))
    return m


def _flash_step(z_ref, z_max, rb, vt, m, acc_ref):
    m_new = jnp.maximum(m, z_max + rb)
    p = jnp.exp2(z_ref[...] - (m_new - rb))
    acc_ref[...] = jnp.exp2(m - m_new) * acc_ref[...] + _dot(vt, p.astype(_BF16))
    return m_new


_BIAS_LANES = 3


def _bias_key_lanes(lane, lane0):
    pos = lax.broadcasted_iota(jnp.int32, lane.shape, 0).astype(_F32)
    return jnp.where((lane >= lane0) & (lane < lane0 + _BIAS_LANES), pos, 0.0)


def _bias_query_lanes(slope, lane, lane0):
    rest = jnp.full(lane.shape, slope, _F32)
    out = jnp.zeros(lane.shape, _F32)
    for t in range(_BIAS_LANES):
        piece = rest.astype(_BF16).astype(_F32)
        out = jnp.where(lane == lane0 + t, piece, out)
        rest = rest - piece
    return out


def _fill_values_transposed(v_ref, vt_scr, g, tk):
    for n in range(vt_scr.shape[1]):
        vt_scr[g, n, :HEAD_DIM, :] = v_ref[n * tk:(n + 1) * tk, _head_lanes(g)].T
        vt_scr[g, n, HEAD_DIM:, :] = jnp.ones((_ONES_ROWS, tk), _BF16)


def _normalised(acc):
    return acc[:HEAD_DIM] / acc[HEAD_DIM:HEAD_DIM + 1]


def _pipelined_flash(n_streams, n_past, scores, first_update, update, sa_scr, sb_scr):
    streams = range(n_streams)
    for n in streams:
        scores(n, n_past, sb_scr)
    max_a = tuple(scores(n, 0, sa_scr) for n in streams)
    running = tuple(first_update(n, sb_scr) for n in streams)

    def pair(t, carry):
        running, max_a = (list(c) for c in carry)
        j0 = 2 * t
        j1 = j0 + 1
        j2 = jnp.minimum(j1 + 1, n_past)
        max_b = [None] * n_streams
        for n in streams:
            max_b[n] = scores(n, j1, sb_scr)
            running[n] = update(n, j0, sa_scr, max_a[n], running[n])
        for n in streams:
            max_a[n] = scores(n, j2, sa_scr)
            running[n] = update(n, j1, sb_scr, max_b[n], running[n], valid=j1 < n_past)
        return tuple(running), tuple(max_a)

    def two_pairs(t, carry):
        return pair(2 * t + 1, pair(2 * t, carry))

    n_pairs = (n_past + 1) // 2
    carry = lax.fori_loop(0, n_pairs // 2, two_pairs, (running, max_a))
    lax.fori_loop(n_pairs - n_pairs % 2, n_pairs, pair, carry)


def _head_lanes(g):
    return slice(g * HEAD_DIM, (g + 1) * HEAD_DIM)


def _head_group_spec(rows, tq, width, first_blk, hps):
    assert first_blk % hps == 0
    index_map = lambda bi, hi, i: (bi, i if rows == tq else 0, first_blk // hps + hi)
    if rows == tq:
        return pl.BlockSpec((None, rows, width), index_map)
    return pl.BlockSpec((None, rows, width), index_map, pipeline_mode=pl.Buffered(1))


def _diff_attention_kernel(slopes_ref, lq1_ref, lk1_ref, lq2_ref, lk2_ref, subg_ref,
                           q_ref, k_ref, v_ref, gate_ref, o_ref,
                           vt_scr, ka_scr, acc_scr, sa_scr, sb_scr,
                           *, lam_init, tq, heads_per_step):
    i = pl.program_id(2)
    heads = range(heads_per_step)
    lane = lax.broadcasted_iota(jnp.int32, (tq, HEAD_DIM), 1)
    own_half = (lane < DIFF_HALF, lane >= DIFF_HALF)
    bias_lane0 = (DIFF_HALF, 0)

    @pl.when(i == 0)
    def _():
        for g in heads:
            _fill_values_transposed(v_ref, vt_scr, g, tq)
            for n in range(ka_scr.shape[2]):
                k = k_ref[n * tq:(n + 1) * tq, _head_lanes(g)].astype(_F32)
                for c in range(2):
                    ka_scr[g, c, n] = jnp.where(
                        own_half[c], k, _bias_key_lanes(lane, bias_lane0[c])).astype(_BF16)

    key_idx = lax.broadcasted_iota(jnp.int32, (tq, tq), 0)
    qry_idx = lax.broadcasted_iota(jnp.int32, (tq, tq), 1)
    qry_pos = lax.broadcasted_iota(jnp.int32, (1, tq), 1).astype(_F32)

    streams = []
    for g in heads:
        slope = slopes_ref[pl.program_id(1) * heads_per_step + g] * _LOG2E
        q = q_ref[:, _head_lanes(g)].astype(_F32)
        for c in range(2):
            qc = jnp.where(own_half[c], q, _bias_query_lanes(slope, lane, bias_lane0[c]))
            streams.append((g, c, slope, qc.astype(_BF16)))

    def scores(n, j, s_buf):
        g, c, _, qc = streams[n]
        s = _dot_nt(ka_scr[g, c, j], qc)
        s_buf[n] = s
        return jnp.max(s, axis=0, keepdims=True)

    def first_update(n, s_buf):
        g, c, slope, _ = streams[n]
        z = jnp.where(key_idx <= qry_idx, s_buf[n], _NEG_INF)
        return _first_flash_step(z, -slope * qry_pos, vt_scr[g, i], acc_scr.at[g, c])

    def update(n, j, s_buf, tile_max, running_max, valid=None):
        g, c, slope, _ = streams[n]
        rb = -slope * qry_pos - slope * ((i - j) * tq).astype(_F32)
        if valid is not None:
            rb = jnp.where(valid, rb, _NEG_INF)
        return _flash_step(s_buf.at[n], tile_max, rb, vt_scr[g, j], running_max,
                           acc_scr.at[g, c])

    _pipelined_flash(len(streams), i, scores, first_update, update, sa_scr, sb_scr)

    lam = (jnp.exp(jnp.sum(lq1_ref[...] * lk1_ref[...], axis=-1, keepdims=True))
           - jnp.exp(jnp.sum(lq2_ref[...] * lk2_ref[...], axis=-1, keepdims=True))
           + lam_init)
    for g in heads:
        o_t = _normalised(acc_scr[g, 0]) - lam * _normalised(acc_scr[g, 1])
        ms = jnp.mean(o_t * o_t, axis=0, keepdims=True)
        o = (o_t * lax.rsqrt(ms + EPS)).T * subg_ref[...] * (1.0 - lam_init)
        gate = gate_ref[:, _head_lanes(g)].astype(_F32)
        o_ref[:, _head_lanes(g)] = (o * _silu(gate)).astype(o_ref.dtype)


def diff_attention(proj, slopes, lq1, lk1, lq2, lk2, subg, *, lam_init, tq, heads_per_step):
    b, s, _ = proj.shape
    hps = heads_per_step
    assert s % tq == 0 and DIFF_HEADS % hps == 0
    width = hps * HEAD_DIM
    vec = lambda n: pl.BlockSpec((1, n), lambda bi, hi, i: (0, 0))
    spec = functools.partial(_head_group_spec, tq=tq, width=width, hps=hps)
    scratch = [pltpu.VMEM((hps, s // tq, _VT_ROWS, tq), _BF16),
               pltpu.VMEM((hps, 2, s // tq, tq, HEAD_DIM), _BF16),
               pltpu.VMEM((hps, 2, _VT_ROWS, tq), _F32),
               pltpu.VMEM((2 * hps, tq, tq), _F32),
               pltpu.VMEM((2 * hps, tq, tq), _F32)]
    return pl.pallas_call(
        functools.partial(_diff_attention_kernel, lam_init=lam_init, tq=tq,
                          heads_per_step=hps),
        grid=(b, DIFF_HEADS // hps, s // tq),
        in_specs=[
            pl.BlockSpec(memory_space=pltpu.SMEM),
            vec(DIFF_HALF), vec(DIFF_HALF), vec(DIFF_HALF), vec(DIFF_HALF), vec(HEAD_DIM),
            spec(tq, first_blk=_DQ_BLK), spec(s, first_blk=_DK_BLK), spec(s, first_blk=_DV_BLK),
            spec(tq, first_blk=_GATE_BLK),
        ],
        out_specs=pl.BlockSpec((None, tq, width), lambda bi, hi, i: (bi, i, hi)),
        out_shape=jax.ShapeDtypeStruct((b, s, DIFF_WIDTH), _BF16),
        scratch_shapes=scratch,
        compiler_params=pltpu.CompilerParams(
            dimension_semantics=("parallel", "parallel", "arbitrary"),
            vmem_limit_bytes=_VMEM_LIMIT),
        name="diff_attention",
    )(slopes, lq1.reshape(1, -1), lk1.reshape(1, -1), lq2.reshape(1, -1), lk2.reshape(1, -1),
      subg.reshape(1, -1), proj, proj, proj, proj)


def _moba_attention_kernel(slopes_ref, q_ref, k_ref, v_ref, gate_ref, o_ref,
                           kmean_scr, vt_scr, sel_scr, acc_scr, sa_scr, sb_scr,
                           *, n_blocks, heads_per_step):
    i = pl.program_id(2)
    tq = MOBA_BLOCK
    heads = range(heads_per_step)

    lane = lax.broadcasted_iota(jnp.int32, (tq, HEAD_DIM), 1)

    @pl.when(i == 0)
    def _():
        for g in heads:
            _fill_values_transposed(v_ref, vt_scr, g, tq)
            for n in range(n_blocks):
                k = k_ref[n * MOBA_BLOCK:(n + 1) * MOBA_BLOCK, _head_lanes(g)]
                kmean_scr[g, n:n + 1, :] = (jnp.sum(k.astype(_F32), axis=0, keepdims=True)
                                            * (1.0 / MOBA_BLOCK))

    key_idx = lax.broadcasted_iota(jnp.int32, (tq, tq), 0)
    qry_idx = lax.broadcasted_iota(jnp.int32, (tq, tq), 1)
    qry_pos = lax.broadcasted_iota(jnp.int32, (1, tq), 1).astype(_F32)
    blk = lax.broadcasted_iota(jnp.int32, (n_blocks, tq), 0)
    past = blk < i

    slopes = [slopes_ref[pl.program_id(1) * heads_per_step + g] * _LOG2E for g in heads]
    qs = [q_ref[:, _head_lanes(g)] for g in heads]
    qas = [jnp.concatenate([qs[g], _bias_query_lanes(slopes[g], lane, 0).astype(_BF16)], axis=1)
           for g in heads]
    key_pos_lanes = _bias_key_lanes(lane, 0).astype(_BF16)

    def scores(g, j, s_buf):
        ka = jnp.concatenate([k_ref[pl.ds(j * tq, tq), _head_lanes(g)], key_pos_lanes], axis=1)
        s = _dot_nt(ka, qas[g])
        s_buf[g] = s
        return jnp.max(s, axis=0, keepdims=True)

    def select_blocks(g):
        gate = lax.dot_general(kmean_scr[g], qs[g].astype(_F32), (((1,), (1,)), ((), ())),
                               precision=lax.Precision.HIGHEST,
                               preferred_element_type=_F32)
        gate = jnp.where(past, gate, _NEG_INF)
        rank = jnp.zeros(gate.shape, jnp.int32)
        for n in range(n_blocks):
            g_n = gate[n:n + 1, :]
            beats = (g_n > gate) | ((g_n == gate) & (n < blk))
            rank = rank + beats.astype(jnp.int32)
        sel_scr[g] = jnp.where(past & (rank < MOBA_TOPK), 0.0, _NEG_INF)

    def first_update(g, s_buf):
        select_blocks(g)
        z = jnp.where(key_idx <= qry_idx, s_buf[g], _NEG_INF)
        return _first_flash_step(z, -slopes[g] * qry_pos, vt_scr[g, i], acc_scr.at[g])

    def update(g, j, s_buf, tile_max, running_max, valid=None):
        rb = (-slopes[g] * qry_pos - slopes[g] * ((i - j) * tq).astype(_F32)
              + sel_scr[g, pl.ds(j, 1), :])
        return _flash_step(s_buf.at[g], tile_max, rb, vt_scr[g, j], running_max, acc_scr.at[g])

    _pipelined_flash(heads_per_step, i, scores, first_update, update, sa_scr, sb_scr)
    for g in heads:
        o = _normalised(acc_scr[g]).T
        gate = gate_ref[:, _head_lanes(g)].astype(_F32)
        o_ref[:, _head_lanes(g)] = (o * _silu(gate)).astype(o_ref.dtype)


def moba_attention(proj, slopes, *, heads_per_step):
    b, s, _ = proj.shape
    hps = heads_per_step
    assert s % MOBA_BLOCK == 0 and MOBA_HEADS % hps == 0
    n_blocks = s // MOBA_BLOCK
    tq = MOBA_BLOCK
    width = hps * HEAD_DIM
    spec = functools.partial(_head_group_spec, tq=tq, width=width, hps=hps)
    return pl.pallas_call(
        functools.partial(_moba_attention_kernel, n_blocks=n_blocks, heads_per_step=hps),
        grid=(b, MOBA_HEADS // hps, n_blocks),
        in_specs=[
            pl.BlockSpec(memory_space=pltpu.SMEM),
            spec(tq, first_blk=_MQ_BLK), spec(s, first_blk=_MK_BLK), spec(s, first_blk=_MV_BLK),
            spec(tq, first_blk=_GATE_BLK + DIFF_HEADS),
        ],
        out_specs=pl.BlockSpec((None, tq, width), lambda bi, hi, i: (bi, i, hi)),
        out_shape=jax.ShapeDtypeStruct((b, s, MOBA_WIDTH), _BF16),
        scratch_shapes=[pltpu.VMEM((hps, n_blocks, HEAD_DIM), _F32),
                        pltpu.VMEM((hps, n_blocks, _VT_ROWS, tq), _BF16),
                        pltpu.VMEM((hps, n_blocks, tq), _F32),
                        pltpu.VMEM((hps, _VT_ROWS, tq), _F32),
                        pltpu.VMEM((hps, tq, tq), _F32),
                        pltpu.VMEM((hps, tq, tq), _F32)],
        compiler_params=pltpu.CompilerParams(
            dimension_semantics=("parallel", "parallel", "arbitrary"),
            vmem_limit_bytes=_VMEM_LIMIT),
        name="moba_attention",
    )(slopes, proj, proj, proj, proj)


def _layer_tail_kernel(yd_ref, ym_ref, x_ref, wa_ref, wb_ref, g_ref, wq_ref, kv_ref, wo_ref,
                       fg_ref, o_ref, q_scr, o_scr, *, apply_final):
    x = x_ref[...] + _dot(yd_ref[...], wa_ref[...]) + _dot(ym_ref[...], wb_ref[...])
    ms = jnp.mean(x * x, axis=-1, keepdims=True)
    h = (x * lax.rsqrt(ms + EPS) * g_ref[...]).astype(_BF16)
    q_scr[...] = (_dot(h, wq_ref[...]) * (HEAD_DIM ** -0.5 * _LOG2E)).astype(_BF16)
    for hh in range(MEM_HEADS):
        lo, hi = hh * HEAD_DIM, (hh + 1) * HEAD_DIM
        s = _dot_nt(q_scr[:, lo:hi], kv_ref[:, lo:hi])
        p = jnp.exp2(s - jnp.max(s, axis=-1, keepdims=True))
        l = jnp.sum(p, axis=-1, keepdims=True)
        o_h = _dot(p.astype(_BF16), kv_ref[:, MEM_WIDTH + lo:MEM_WIDTH + hi]) / l
        o_scr[:, lo:hi] = o_h.astype(_BF16)
    x = x + _dot(o_scr[...], wo_ref[...])
    if apply_final:
        ms = jnp.mean(x * x, axis=-1, keepdims=True)
        x = x * lax.rsqrt(ms + EPS) * fg_ref[...]
    o_ref[...] = x


def layer_tail(yd, ym, x, w_out, g, wq, kv, wo, final_g, *, apply_final, tm):
    b, s, d = x.shape
    mlen = kv.shape[1]
    ka, kb = yd.shape[-1], ym.shape[-1]
    assert s % tm == 0 and ka == kb and w_out.shape[0] == ka + kb
    rows = lambda width: pl.BlockSpec((None, tm, width), lambda bi, i: (bi, i, 0))
    whole = lambda shape: pl.BlockSpec(shape, lambda bi, i: (0,) * len(shape))
    return pl.pallas_call(
        functools.partial(_layer_tail_kernel, apply_final=apply_final),
        grid=(b, s // tm),
        in_specs=[
            rows(ka), rows(kb), rows(d),
            pl.BlockSpec((ka, d), lambda bi, i: (0, 0)),
            pl.BlockSpec((kb, d), lambda bi, i: (1, 0)),
            whole((1, d)), whole((d, MEM_WIDTH)),
            pl.BlockSpec((None, mlen, 2 * MEM_WIDTH), lambda bi, i: (bi, 0, 0)),
            whole((MEM_WIDTH, d)), whole((1, d)),
        ],
        out_specs=rows(d),
        out_shape=jax.ShapeDtypeStruct((b, s, d), _F32),
        scratch_shapes=[pltpu.VMEM((tm, MEM_WIDTH), _BF16),
                        pltpu.VMEM((tm, MEM_WIDTH), _BF16)],
        compiler_params=pltpu.CompilerParams(
            dimension_semantics=("parallel", "parallel"),
            vmem_limit_bytes=_VMEM_LIMIT),
        name="layer_tail",
    )(yd, ym, x, w_out, w_out, g.reshape(1, d), wq, kv, wo, final_g.reshape(1, d))


def _in_proj_col_scale(n_cols):
    cs = np.ones((n_cols,), np.float32)
    cs[_DQ_BLK * HEAD_DIM:_DK_BLK * HEAD_DIM] = DIFF_HALF ** -0.5 * _LOG2E
    cs[_MQ_BLK * HEAD_DIM:_MK_BLK * HEAD_DIM] = HEAD_DIM ** -0.5 * _LOG2E
    return jnp.asarray(cs)


def _alibi_slopes(n):
    return jnp.asarray(2.0 ** (-8.0 * np.arange(1, n + 1) / n), dtype=_F32)


_IN_PROJ_TILE = (1024, 1024)
_ATTN_Q_TILE = 256
_DIFF_HEADS_PER_STEP = 4
_MOBA_HEADS_PER_STEP = 8
_TAIL_ROW_TILE = 256


def kernel(x, mem, norm_mix_g, w_in, lambda_q1, lambda_k1, lambda_q2, lambda_k2, subln_g,
           w_out, norm_mem_g, mem_norm_g, w_q_mem, w_kv_mem, w_o_mem, final_g):
    b, s, d = x.shape
    mlen = mem.shape[1]
    depth = w_in.shape[0]
    diff_slopes = _alibi_slopes(DIFF_HEADS)
    moba_slopes = _alibi_slopes(MOBA_HEADS)
    mem2 = mem.reshape(b * mlen, d)
    for l in range(depth):
        lam_init = 0.8 - 0.6 * math.exp(-0.3 * l)
        proj = rms_matmul(x.reshape(b * s, d), norm_mix_g[l], w_in[l].astype(_BF16),
                          _in_proj_col_scale(w_in.shape[-1]),
                          tm=_IN_PROJ_TILE[0], tn=_IN_PROJ_TILE[1]).reshape(b, s, -1)
        y_d = diff_attention(proj, diff_slopes, lambda_q1[l], lambda_k1[l], lambda_q2[l],
                             lambda_k2[l], subln_g[l], lam_init=lam_init, tq=_ATTN_Q_TILE,
                             heads_per_step=_DIFF_HEADS_PER_STEP)
        y_m = moba_attention(proj, moba_slopes, heads_per_step=_MOBA_HEADS_PER_STEP)
        kv = rms_matmul(mem2, mem_norm_g[l], w_kv_mem[l].astype(_BF16),
                        tm=b * mlen, tn=2 * MEM_WIDTH).reshape(b, mlen, 2 * MEM_WIDTH)
        x = layer_tail(y_d, y_m, x, w_out[l].astype(_BF16), norm_mem_g[l],
                       w_q_mem[l].astype(_BF16), kv, w_o_mem[l].astype(_BF16), final_g,
                       apply_final=(l == depth - 1), tm=_TAIL_ROW_TILE)
    return x
```

```python
import functools
import math

import jax
import jax.numpy as jnp
import numpy as np
from jax import lax
from jax.experimental import pallas as pl
from jax.experimental.pallas import tpu as pltpu

HEAD_DIM = 128
DIFF_HEADS = 8
DIFF_HALF = HEAD_DIM // 2
DIFF_WIDTH = DIFF_HEADS * HEAD_DIM
MOBA_HEADS = 8
MOBA_WIDTH = MOBA_HEADS * HEAD_DIM
MOBA_BLOCK = 256
MOBA_TOPK = 3
MEM_HEADS = 4
MEM_WIDTH = MEM_HEADS * HEAD_DIM
EPS = 1e-6

_DQ_BLK = 0
_DK_BLK = DIFF_HEADS
_DV_BLK = 2 * DIFF_HEADS
_MQ_BLK = 3 * DIFF_HEADS
_MK_BLK = 3 * DIFF_HEADS + MOBA_HEADS
_MV_BLK = 3 * DIFF_HEADS + 2 * MOBA_HEADS
_GATE_BLK = 3 * DIFF_HEADS + 3 * MOBA_HEADS

_V7X_VMEM_BYTES = 64 * 1024 * 1024
_VMEM_LIMIT = _V7X_VMEM_BYTES * 3 // 4

_BF16 = jnp.bfloat16
_F32 = jnp.float32
_NEG_INF = float("-inf")


def _dot_nt(a, b):
    return lax.dot_general(a, b, (((1,), (1,)), ((), ())), preferred_element_type=_F32)


def _dot(a, b):
    return jnp.dot(a, b, preferred_element_type=_F32)


def _silu(g):
    return g * jax.nn.sigmoid(g)


def _rms_matmul_kernel(x_ref, g_ref, w_ref, *rest, has_col_scale):
    cs_ref, o_ref, h_scr = rest if has_col_scale else (None,) + rest

    @pl.when(pl.program_id(1) == 0)
    def _():
        x = x_ref[...]
        ms = jnp.mean(x * x, axis=-1, keepdims=True)
        h_scr[...] = (x * lax.rsqrt(ms + EPS) * g_ref[...]).astype(_BF16)

    out = _dot(h_scr[...], w_ref[...])
    if has_col_scale:
        out = out * cs_ref[...]
    o_ref[...] = out.astype(o_ref.dtype)


def rms_matmul(x, g, w, col_scale=None, *, tm, tn, out_dtype=_BF16):
    m, k = x.shape
    n = w.shape[1]
    assert m % tm == 0 and n % tn == 0
    has_cs = col_scale is not None
    cs_specs = [pl.BlockSpec((1, tn), lambda i, j: (0, j))] if has_cs else []
    cs_args = [col_scale.reshape(1, n)] if has_cs else []
    return pl.pallas_call(
        functools.partial(_rms_matmul_kernel, has_col_scale=has_cs),
        grid=(m // tm, n // tn),
        in_specs=[
            pl.BlockSpec((tm, k), lambda i, j: (i, 0)),
            pl.BlockSpec((1, k), lambda i, j: (0, 0)),
            pl.BlockSpec((k, tn), lambda i, j: (0, j)),
        ] + cs_specs,
        out_specs=pl.BlockSpec((tm, tn), lambda i, j: (i, j)),
        out_shape=jax.ShapeDtypeStruct((m, n), out_dtype),
        scratch_shapes=[pltpu.VMEM((tm, k), _BF16)],
        compiler_params=pltpu.CompilerParams(
            dimension_semantics=("parallel", "arbitrary"),
            vmem_limit_bytes=_VMEM_LIMIT),
        name="rms_matmul",
    )(x, g.reshape(1, k), w, *cs_args)


_LOG2E = math.log2(math.e)
_ONES_ROWS = 16
_VT_ROWS = HEAD_DIM + _ONES_ROWS


def _first_flash_step(z, rb, vt, acc_ref):
    m = jnp.max(z, axis=0, keepdims=True) + rb
    p = jnp.exp2(z - (m - rb))
    acc_ref[...] = _dot(vt, p.astype(_BF16))
    return m


def _flash_step(z_ref, z_max, rb, vt, m, acc_ref):
    m_new = jnp.maximum(m, z_max + rb)
    p = jnp.exp2(z_ref[...] - (m_new - rb))
    acc_ref[...] = jnp.exp2(m - m_new) * acc_ref[...] + _dot(vt, p.astype(_BF16))
    return m_new


_BIAS_LANES = 3


def _bias_key_lanes(lane, lane0):
    pos = lax.broadcasted_iota(jnp.int32, lane.shape, 0).astype(_F32)
    return jnp.where((lane >= lane0) & (lane < lane0 + _BIAS_LANES), pos, 0.0)


def _bias_query_lanes(slope, lane, lane0):
    rest = jnp.full(lane.shape, slope, _F32)
    out = jnp.zeros(lane.shape, _F32)
    for t in range(_BIAS_LANES):
        piece = rest.astype(_BF16).astype(_F32)
        out = jnp.where(lane == lane0 + t, piece, out)
        rest = rest - piece
    return out


def _fill_values_transposed(v_ref, vt_scr, g, tk):
    for n in range(vt_scr.shape[1]):
        vt_scr[g, n, :HEAD_DIM, :] = v_ref[n * tk:(n + 1) * tk, _head_lanes(g)].T
        vt_scr[g, n, HEAD_DIM:, :] = jnp.ones((_ONES_ROWS, tk), _BF16)


def _normalised(acc):
    return acc[:HEAD_DIM] / acc[HEAD_DIM:HEAD_DIM + 1]


def _pipelined_flash(n_streams, n_past, scores, first_update, update, sa_scr, sb_scr):
    streams = range(n_streams)
    for n in streams:
        scores(n, n_past, sb_scr)
    max_a = tuple(scores(n, 0, sa_scr) for n in streams)
    running = tuple(first_update(n, sb_scr) for n in streams)

    def pair(t, carry):
        running, max_a = (list(c) for c in carry)
        j0 = 2 * t
        j1 = j0 + 1
        j2 = jnp.minimum(j1 + 1, n_past)
        max_b = [None] * n_streams
        for n in streams:
            max_b[n] = scores(n, j1, sb_scr)
            running[n] = update(n, j0, sa_scr, max_a[n], running[n])
        for n in streams:
            max_a[n] = scores(n, j2, sa_scr)
            running[n] = update(n, j1, sb_scr, max_b[n], running[n], valid=j1 < n_past)
        return tuple(running), tuple(max_a)

    def two_pairs(t, carry):
        return pair(2 * t + 1, pair(2 * t, carry))

    n_pairs = (n_past + 1) // 2
    carry = lax.fori_loop(0, n_pairs // 2, two_pairs, (running, max_a))
    lax.fori_loop(n_pairs - n_pairs % 2, n_pairs, pair, carry)


def _head_lanes(g):
    return slice(g * HEAD_DIM, (g + 1) * HEAD_DIM)


def _head_group_spec(rows, tq, width, first_blk, hps, seq_buffers):
    assert first_blk % hps == 0
    index_map = lambda bi, hi, i: (bi, i if rows == tq else 0, first_blk // hps + hi)
    if rows == tq:
        return pl.BlockSpec((None, rows, width), index_map)
    return pl.BlockSpec((None, rows, width), index_map,
                        pipeline_mode=pl.Buffered(seq_buffers))


def _diff_attention_kernel(slopes_ref, lq1_ref, lk1_ref, lq2_ref, lk2_ref, subg_ref,
                           q_ref, k_ref, v_ref, gate_ref, o_ref,
                           vt_scr, ka_scr, acc_scr, sa_scr, sb_scr,
                           *, lam_init, tq, heads_per_step):
    i = pl.program_id(2)
    heads = range(heads_per_step)
    lane = lax.broadcasted_iota(jnp.int32, (tq, HEAD_DIM), 1)
    own_half = (lane < DIFF_HALF, lane >= DIFF_HALF)
    bias_lane0 = (DIFF_HALF, 0)

    @pl.when(i == 0)
    def _():
        for g in heads:
            _fill_values_transposed(v_ref, vt_scr, g, tq)
            for n in range(ka_scr.shape[2]):
                k = k_ref[n * tq:(n + 1) * tq, _head_lanes(g)].astype(_F32)
                for c in range(2):
                    ka_scr[g, c, n] = jnp.where(
                        own_half[c], k, _bias_key_lanes(lane, bias_lane0[c])).astype(_BF16)

    key_idx = lax.broadcasted_iota(jnp.int32, (tq, tq), 0)
    qry_idx = lax.broadcasted_iota(jnp.int32, (tq, tq), 1)
    qry_pos = lax.broadcasted_iota(jnp.int32, (1, tq), 1).astype(_F32)

    streams = []
    for g in heads:
        slope = slopes_ref[pl.program_id(1) * heads_per_step + g] * _LOG2E
        q = q_ref[:, _head_lanes(g)].astype(_F32)
        for c in range(2):
            qc = jnp.where(own_half[c], q, _bias_query_lanes(slope, lane, bias_lane0[c]))
            streams.append((g, c, slope, qc.astype(_BF16)))

    def scores(n, j, s_buf):
        g, c, _, qc = streams[n]
        s = _dot_nt(ka_scr[g, c, j], qc)
        s_buf[n] = s
        return jnp.max(s, axis=0, keepdims=True)

    def first_update(n, s_buf):
        g, c, slope, _ = streams[n]
        z = jnp.where(key_idx <= qry_idx, s_buf[n], _NEG_INF)
        return _first_flash_step(z, -slope * qry_pos, vt_scr[g, i], acc_scr.at[g, c])

    def update(n, j, s_buf, tile_max, running_max, valid=None):
        g, c, slope, _ = streams[n]
        rb = -slope * qry_pos - slope * ((i - j) * tq).astype(_F32)
        if valid is not None:
            rb = jnp.where(valid, rb, _NEG_INF)
        return _flash_step(s_buf.at[n], tile_max, rb, vt_scr[g, j], running_max,
                           acc_scr.at[g, c])

    _pipelined_flash(len(streams), i, scores, first_update, update, sa_scr, sb_scr)

    lam = (jnp.exp(jnp.sum(lq1_ref[...] * lk1_ref[...], axis=-1, keepdims=True))
           - jnp.exp(jnp.sum(lq2_ref[...] * lk2_ref[...], axis=-1, keepdims=True))
           + lam_init)
    for g in heads:
        o_t = _normalised(acc_scr[g, 0]) - lam * _normalised(acc_scr[g, 1])
        ms = jnp.mean(o_t * o_t, axis=0, keepdims=True)
        o = (o_t * lax.rsqrt(ms + EPS)).T * subg_ref[...] * (1.0 - lam_init)
        gate = gate_ref[:, _head_lanes(g)].astype(_F32)
        o_ref[:, _head_lanes(g)] = (o * _silu(gate)).astype(o_ref.dtype)


def diff_attention(proj, slopes, lq1, lk1, lq2, lk2, subg, *, lam_init, tq, heads_per_step):
    b, s, _ = proj.shape
    hps = heads_per_step
    assert s % tq == 0 and DIFF_HEADS % hps == 0
    width = hps * HEAD_DIM
    vec = lambda n: pl.BlockSpec((1, n), lambda bi, hi, i: (0, 0))
    spec = functools.partial(_head_group_spec, tq=tq, width=width, hps=hps, seq_buffers=2)
    scratch = [pltpu.VMEM((hps, s // tq, _VT_ROWS, tq), _BF16),
               pltpu.VMEM((hps, 2, s // tq, tq, HEAD_DIM), _BF16),
               pltpu.VMEM((hps, 2, _VT_ROWS, tq), _F32),
               pltpu.VMEM((2 * hps, tq, tq), _F32),
               pltpu.VMEM((2 * hps, tq, tq), _F32)]
    return pl.pallas_call(
        functools.partial(_diff_attention_kernel, lam_init=lam_init, tq=tq,
                          heads_per_step=hps),
        grid=(b, DIFF_HEADS // hps, s // tq),
        in_specs=[
            pl.BlockSpec(memory_space=pltpu.SMEM),
            vec(DIFF_HALF), vec(DIFF_HALF), vec(DIFF_HALF), vec(DIFF_HALF), vec(HEAD_DIM),
            spec(tq, first_blk=_DQ_BLK), spec(s, first_blk=_DK_BLK), spec(s, first_blk=_DV_BLK),
            spec(tq, first_blk=_GATE_BLK),
        ],
        out_specs=pl.BlockSpec((None, tq, width), lambda bi, hi, i: (bi, i, hi)),
        out_shape=jax.ShapeDtypeStruct((b, s, DIFF_WIDTH), _BF16),
        scratch_shapes=scratch,
        compiler_params=pltpu.CompilerParams(
            dimension_semantics=("parallel", "parallel", "arbitrary"),
            vmem_limit_bytes=_VMEM_LIMIT),
        name="diff_attention",
    )(slopes, lq1.reshape(1, -1), lk1.reshape(1, -1), lq2.reshape(1, -1), lk2.reshape(1, -1),
      subg.reshape(1, -1), proj, proj, proj, proj)


def _moba_attention_kernel(slopes_ref, q_ref, k_ref, v_ref, gate_ref, o_ref,
                           kmean_scr, vt_scr, sel_scr, acc_scr, sa_scr, sb_scr,
                           *, n_blocks, heads_per_step):
    i = pl.program_id(2)
    tq = MOBA_BLOCK
    heads = range(heads_per_step)

    lane = lax.broadcasted_iota(jnp.int32, (tq, HEAD_DIM), 1)

    @pl.when(i == 0)
    def _():
        for g in heads:
            _fill_values_transposed(v_ref, vt_scr, g, tq)
            for n in range(n_blocks):
                k = k_ref[n * MOBA_BLOCK:(n + 1) * MOBA_BLOCK, _head_lanes(g)]
                kmean_scr[g, n:n + 1, :] = (jnp.sum(k.astype(_F32), axis=0, keepdims=True)
                                            * (1.0 / MOBA_BLOCK))

    key_idx = lax.broadcasted_iota(jnp.int32, (tq, tq), 0)
    qry_idx = lax.broadcasted_iota(jnp.int32, (tq, tq), 1)
    qry_pos = lax.broadcasted_iota(jnp.int32, (1, tq), 1).astype(_F32)
    blk = lax.broadcasted_iota(jnp.int32, (n_blocks, tq), 0)
    past = blk < i

    slopes = [slopes_ref[pl.program_id(1) * heads_per_step + g] * _LOG2E for g in heads]
    qs = [q_ref[:, _head_lanes(g)] for g in heads]
    qas = [jnp.concatenate([qs[g], _bias_query_lanes(slopes[g], lane, 0).astype(_BF16)], axis=1)
           for g in heads]
    key_pos_lanes = _bias_key_lanes(lane, 0).astype(_BF16)

    def scores(g, j, s_buf):
        ka = jnp.concatenate([k_ref[pl.ds(j * tq, tq), _head_lanes(g)], key_pos_lanes], axis=1)
        s = _dot_nt(ka, qas[g])
        s_buf[g] = s
        return jnp.max(s, axis=0, keepdims=True)

    def select_blocks(g):
        gate = lax.dot_general(kmean_scr[g], qs[g].astype(_F32), (((1,), (1,)), ((), ())),
                               precision=lax.Precision.HIGHEST,
                               preferred_element_type=_F32)
        gate = jnp.where(past, gate, _NEG_INF)
        rank = jnp.zeros(gate.shape, jnp.int32)
        for n in range(n_blocks):
            g_n = gate[n:n + 1, :]
            beats = (g_n > gate) | ((g_n == gate) & (n < blk))
            rank = rank + beats.astype(jnp.int32)
        sel_scr[g] = jnp.where(past & (rank < MOBA_TOPK), 0.0, _NEG_INF)

    def first_update(g, s_buf):
        select_blocks(g)
        z = jnp.where(key_idx <= qry_idx, s_buf[g], _NEG_INF)
        return _first_flash_step(z, -slopes[g] * qry_pos, vt_scr[g, i], acc_scr.at[g])

    def update(g, j, s_buf, tile_max, running_max, valid=None):
        rb = (-slopes[g] * qry_pos - slopes[g] * ((i - j) * tq).astype(_F32)
              + sel_scr[g, pl.ds(j, 1), :])
        return _flash_step(s_buf.at[g], tile_max, rb, vt_scr[g, j], running_max, acc_scr.at[g])

    _pipelined_flash(heads_per_step, i, scores, first_update, update, sa_scr, sb_scr)
    for g in heads:
        o = _normalised(acc_scr[g]).T
        gate = gate_ref[:, _head_lanes(g)].astype(_F32)
        o_ref[:, _head_lanes(g)] = (o * _silu(gate)).astype(o_ref.dtype)


def moba_attention(proj, slopes, *, heads_per_step):
    b, s, _ = proj.shape
    hps = heads_per_step
    assert s % MOBA_BLOCK == 0 and MOBA_HEADS % hps == 0
    n_blocks = s // MOBA_BLOCK
    tq = MOBA_BLOCK
    width = hps * HEAD_DIM
    spec = functools.partial(_head_group_spec, tq=tq, width=width, hps=hps, seq_buffers=1)
    return pl.pallas_call(
        functools.partial(_moba_attention_kernel, n_blocks=n_blocks, heads_per_step=hps),
        grid=(b, MOBA_HEADS // hps, n_blocks),
        in_specs=[
            pl.BlockSpec(memory_space=pltpu.SMEM),
            spec(tq, first_blk=_MQ_BLK), spec(s, first_blk=_MK_BLK), spec(s, first_blk=_MV_BLK),
            spec(tq, first_blk=_GATE_BLK + DIFF_HEADS),
        ],
        out_specs=pl.BlockSpec((None, tq, width), lambda bi, hi, i: (bi, i, hi)),
        out_shape=jax.ShapeDtypeStruct((b, s, MOBA_WIDTH), _BF16),
        scratch_shapes=[pltpu.VMEM((hps, n_blocks, HEAD_DIM), _F32),
                        pltpu.VMEM((hps, n_blocks, _VT_ROWS, tq), _BF16),
                        pltpu.VMEM((hps, n_blocks, tq), _F32),
                        pltpu.VMEM((hps, _VT_ROWS, tq), _F32),
                        pltpu.VMEM((hps, tq, tq), _F32),
                        pltpu.VMEM((hps, tq, tq), _F32)],
        compiler_params=pltpu.CompilerParams(
            dimension_semantics=("parallel", "parallel", "arbitrary"),
            vmem_limit_bytes=_VMEM_LIMIT),
        name="moba_attention",
    )(slopes, proj, proj, proj, proj)


def _layer_tail_kernel(yd_ref, ym_ref, x_ref, wa_ref, wb_ref, g_ref, wq_ref, kv_ref, wo_ref,
                       fg_ref, o_ref, q_scr, o_scr, *, apply_final, sub_rows):
    subs = [slice(r, r + sub_rows) for r in range(0, x_ref.shape[0], sub_rows)]
    xs = [x_ref[r, :] + _dot(yd_ref[r, :], wa_ref[...]) + _dot(ym_ref[r, :], wb_ref[...])
          for r in subs]
    for r, x in zip(subs, xs):
        ms = jnp.mean(x * x, axis=-1, keepdims=True)
        h = (x * lax.rsqrt(ms + EPS) * g_ref[...]).astype(_BF16)
        q_scr[r, :] = (_dot(h, wq_ref[...]) * (HEAD_DIM ** -0.5 * _LOG2E)).astype(_BF16)
    for r in subs:
        for hh in range(MEM_HEADS):
            lo, hi = hh * HEAD_DIM, (hh + 1) * HEAD_DIM
            s = _dot_nt(q_scr[r, lo:hi], kv_ref[:, lo:hi])
            p = jnp.exp2(s - jnp.max(s, axis=-1, keepdims=True))
            l = jnp.sum(p, axis=-1, keepdims=True)
            o_h = _dot(p.astype(_BF16), kv_ref[:, MEM_WIDTH + lo:MEM_WIDTH + hi]) / l
            o_scr[r, lo:hi] = o_h.astype(_BF16)
    for r, x in zip(subs, xs):
        x = x + _dot(o_scr[r, :], wo_ref[...])
        if apply_final:
            ms = jnp.mean(x * x, axis=-1, keepdims=True)
            x = x * lax.rsqrt(ms + EPS) * fg_ref[...]
        o_ref[r, :] = x


def layer_tail(yd, ym, x, w_out, g, wq, kv, wo, final_g, *, apply_final, tm, sub_rows):
    b, s, d = x.shape
    mlen = kv.shape[1]
    ka, kb = yd.shape[-1], ym.shape[-1]
    assert s % tm == 0 and ka == kb and w_out.shape[0] == ka + kb
    rows = lambda width: pl.BlockSpec((None, tm, width), lambda bi, i: (bi, i, 0))
    whole = lambda shape: pl.BlockSpec(shape, lambda bi, i: (0,) * len(shape))
    return pl.pallas_call(
        functools.partial(_layer_tail_kernel, apply_final=apply_final, sub_rows=sub_rows),
        grid=(b, s // tm),
        in_specs=[
            rows(ka), rows(kb), rows(d),
            pl.BlockSpec((ka, d), lambda bi, i: (0, 0)),
            pl.BlockSpec((kb, d), lambda bi, i: (1, 0)),
            whole((1, d)), whole((d, MEM_WIDTH)),
            pl.BlockSpec((None, mlen, 2 * MEM_WIDTH), lambda bi, i: (bi, 0, 0)),
            whole((MEM_WIDTH, d)), whole((1, d)),
        ],
        out_specs=rows(d),
        out_shape=jax.ShapeDtypeStruct((b, s, d), _F32),
        scratch_shapes=[pltpu.VMEM((tm, MEM_WIDTH), _BF16),
                        pltpu.VMEM((tm, MEM_WIDTH), _BF16)],
        compiler_params=pltpu.CompilerParams(
            dimension_semantics=("parallel", "parallel"),
            vmem_limit_bytes=_VMEM_LIMIT),
        name="layer_tail",
    )(yd, ym, x, w_out, w_out, g.reshape(1, d), wq, kv, wo, final_g.reshape(1, d))


def _in_proj_col_scale(n_cols):
    cs = np.ones((n_cols,), np.float32)
    cs[_DQ_BLK * HEAD_DIM:_DK_BLK * HEAD_DIM] = DIFF_HALF ** -0.5 * _LOG2E
    cs[_MQ_BLK * HEAD_DIM:_MK_BLK * HEAD_DIM] = HEAD_DIM ** -0.5 * _LOG2E
    return jnp.asarray(cs)


def _alibi_slopes(n):
    return jnp.asarray(2.0 ** (-8.0 * np.arange(1, n + 1) / n), dtype=_F32)


_IN_PROJ_TILE = (1024, 1024)
_ATTN_Q_TILE = 256
_DIFF_HEADS_PER_STEP = 4
_MOBA_HEADS_PER_STEP = 8
_TAIL_ROW_TILE = 512
_TAIL_SUB_ROWS = 256


def kernel(x, mem, norm_mix_g, w_in, lambda_q1, lambda_k1, lambda_q2, lambda_k2, subln_g,
           w_out, norm_mem_g, mem_norm_g, w_q_mem, w_kv_mem, w_o_mem, final_g):
    b, s, d = x.shape
    mlen = mem.shape[1]
    depth = w_in.shape[0]
    diff_slopes = _alibi_slopes(DIFF_HEADS)
    moba_slopes = _alibi_slopes(MOBA_HEADS)
    mem2 = mem.reshape(b * mlen, d)
    for l in range(depth):
        lam_init = 0.8 - 0.6 * math.exp(-0.3 * l)
        proj = rms_matmul(x.reshape(b * s, d), norm_mix_g[l], w_in[l].astype(_BF16),
                          _in_proj_col_scale(w_in.shape[-1]),
                          tm=_IN_PROJ_TILE[0], tn=_IN_PROJ_TILE[1]).reshape(b, s, -1)
        y_d = diff_attention(proj, diff_slopes, lambda_q1[l], lambda_k1[l], lambda_q2[l],
                             lambda_k2[l], subln_g[l], lam_init=lam_init, tq=_ATTN_Q_TILE,
                             heads_per_step=_DIFF_HEADS_PER_STEP)
        y_m = moba_attention(proj, moba_slopes, heads_per_step=_MOBA_HEADS_PER_STEP)
        kv = rms_matmul(mem2, mem_norm_g[l], w_kv_mem[l].astype(_BF16),
                        tm=b * mlen, tn=2 * MEM_WIDTH).reshape(b, mlen, 2 * MEM_WIDTH)
        x = layer_tail(y_d, y_m, x, w_out[l].astype(_BF16), norm_mem_g[l],
                       w_q_mem[l].astype(_BF16), kv, w_o_mem[l].astype(_BF16), final_g,
                       apply_final=(l == depth - 1), tm=_TAIL_ROW_TILE,
                       sub_rows=_TAIL_SUB_ROWS)
    return x
```

```python
import functools
import math

import jax
import jax.numpy as jnp
import numpy as np
from jax import lax
from jax.experimental import pallas as pl
from jax.experimental.pallas import tpu as pltpu

HEAD_DIM = 128
DIFF_HEADS = 8
DIFF_HALF = HEAD_DIM // 2
DIFF_WIDTH = DIFF_HEADS * HEAD_DIM
MOBA_HEADS = 8
MOBA_WIDTH = MOBA_HEADS * HEAD_DIM
MOBA_BLOCK = 256
MOBA_TOPK = 3
MEM_HEADS = 4
MEM_WIDTH = MEM_HEADS * HEAD_DIM
EPS = 1e-6

_DQ_BLK = 0
_DK_BLK = DIFF_HEADS
_DV_BLK = 2 * DIFF_HEADS
_MQ_BLK = 3 * DIFF_HEADS
_MK_BLK = 3 * DIFF_HEADS + MOBA_HEADS
_MV_BLK = 3 * DIFF_HEADS + 2 * MOBA_HEADS
_GATE_BLK = 3 * DIFF_HEADS + 3 * MOBA_HEADS

_V7X_VMEM_BYTES = 64 * 1024 * 1024
_VMEM_LIMIT = _V7X_VMEM_BYTES * 3 // 4

_BF16 = jnp.bfloat16
_F32 = jnp.float32
_NEG_INF = float("-inf")


def _dot_nt(a, b):
    return lax.dot_general(a, b, (((1,), (1,)), ((), ())), preferred_element_type=_F32)


def _dot(a, b):
    return jnp.dot(a, b, preferred_element_type=_F32)


def _silu(g):
    return g * jax.nn.sigmoid(g)


def _rms_matmul_kernel(x_ref, g_ref, w_ref, *rest, has_col_scale):
    cs_ref, o_ref, h_scr = rest if has_col_scale else (None,) + rest

    @pl.when(pl.program_id(1) == 0)
    def _():
        x = x_ref[...]
        ms = jnp.mean(x * x, axis=-1, keepdims=True)
        h_scr[...] = (x * lax.rsqrt(ms + EPS) * g_ref[...]).astype(_BF16)

    out = _dot(h_scr[...], w_ref[...])
    if has_col_scale:
        out = out * cs_ref[...]
    o_ref[...] = out.astype(o_ref.dtype)


def rms_matmul(x, g, w, col_scale=None, *, tm, tn, out_dtype=_BF16):
    m, k = x.shape
    n = w.shape[1]
    assert m % tm == 0 and n % tn == 0
    has_cs = col_scale is not None
    cs_specs = [pl.BlockSpec((1, tn), lambda i, j: (0, j))] if has_cs else []
    cs_args = [col_scale.reshape(1, n)] if has_cs else []
    return pl.pallas_call(
        functools.partial(_rms_matmul_kernel, has_col_scale=has_cs),
        grid=(m // tm, n // tn),
        in_specs=[
            pl.BlockSpec((tm, k), lambda i, j: (i, 0)),
            pl.BlockSpec((1, k), lambda i, j: (0, 0)),
            pl.BlockSpec((k, tn), lambda i, j: (0, j)),
        ] + cs_specs,
        out_specs=pl.BlockSpec((tm, tn), lambda i, j: (i, j)),
        out_shape=jax.ShapeDtypeStruct((m, n), out_dtype),
        scratch_shapes=[pltpu.VMEM((tm, k), _BF16)],
        compiler_params=pltpu.CompilerParams(
            dimension_semantics=("parallel", "arbitrary"),
            vmem_limit_bytes=_VMEM_LIMIT),
        name="rms_matmul",
    )(x, g.reshape(1, k), w, *cs_args)


_LOG2E = math.log2(math.e)
_ONES_ROWS = 16
_VT_ROWS = HEAD_DIM + _ONES_ROWS


def _first_flash_step(z, rb, vt, acc_ref):
    m = jnp.max(z, axis=0, keepdims=True) + rb
    p = jnp.exp2(z - (m - rb))
    acc_ref[...] = _dot(vt, p.astype(_BF16))
    return m


def _flash_step(z_ref, z_max, rb, vt, m, acc_ref):
    m_new = jnp.maximum(m, z_max + rb)
    p = jnp.exp2(z_ref[...] - (m_new - rb))
    acc_ref[...] = jnp.exp2(m - m_new) * acc_ref[...] + _dot(vt, p.astype(_BF16))
    return m_new


_F32_AS_BF16_PIECES = 3
_BIAS_LANES = _F32_AS_BF16_PIECES
_SUBLANES = 8


def _bias_key_lanes(lane, lane0):
    pos = lax.broadcasted_iota(jnp.int32, lane.shape, 0).astype(_F32)
    return jnp.where((lane >= lane0) & (lane < lane0 + _BIAS_LANES), pos, 0.0)


def _bias_query_lanes(slope, lane, lane0):
    rest = jnp.full(lane.shape, slope, _F32)
    out = jnp.zeros(lane.shape, _F32)
    for t in range(_BIAS_LANES):
        piece = rest.astype(_BF16).astype(_F32)
        out = jnp.where(lane == lane0 + t, piece, out)
        rest = rest - piece
    return out


def _fill_values_transposed(v_ref, vt_scr, g, tk):
    for n in range(vt_scr.shape[1]):
        vt_scr[g, n, :HEAD_DIM, :] = v_ref[n * tk:(n + 1) * tk, _head_lanes(g)].T
        vt_scr[g, n, HEAD_DIM:, :] = jnp.ones((_ONES_ROWS, tk), _BF16)


def _normalised(acc):
    return acc[:HEAD_DIM] / acc[HEAD_DIM:HEAD_DIM + 1]


def _pipelined_flash(n_streams, n_past, scores, first_update, update, sa_scr, sb_scr):
    streams = range(n_streams)
    for n in streams:
        scores(n, n_past, sb_scr)
    max_a = tuple(scores(n, 0, sa_scr) for n in streams)
    running = tuple(first_update(n, sb_scr) for n in streams)

    def pair(t, carry):
        running, max_a = (list(c) for c in carry)
        j0 = 2 * t
        j1 = j0 + 1
        j2 = jnp.minimum(j1 + 1, n_past)
        max_b = [None] * n_streams
        for n in streams:
            max_b[n] = scores(n, j1, sb_scr)
            running[n] = update(n, j0, sa_scr, max_a[n], running[n])
        for n in streams:
            max_a[n] = scores(n, j2, sa_scr)
            running[n] = update(n, j1, sb_scr, max_b[n], running[n], valid=j1 < n_past)
        return tuple(running), tuple(max_a)

    def two_pairs(t, carry):
        return pair(2 * t + 1, pair(2 * t, carry))

    n_pairs = (n_past + 1) // 2
    carry = lax.fori_loop(0, n_pairs // 2, two_pairs, (running, max_a))
    lax.fori_loop(n_pairs - n_pairs % 2, n_pairs, pair, carry)


def _head_lanes(g):
    return slice(g * HEAD_DIM, (g + 1) * HEAD_DIM)


def _head_group_spec(rows, tq, width, first_blk, hps, seq_buffers):
    assert first_blk % hps == 0
    index_map = lambda bi, hi, i: (bi, i if rows == tq else 0, first_blk // hps + hi)
    if rows == tq:
        return pl.BlockSpec((None, rows, width), index_map)
    return pl.BlockSpec((None, rows, width), index_map,
                        pipeline_mode=pl.Buffered(seq_buffers))


def _diff_attention_kernel(slopes_ref, lq1_ref, lk1_ref, lq2_ref, lk2_ref, subg_ref,
                           q_ref, k_ref, v_ref, gate_ref, o_ref,
                           vt_scr, ka_scr, acc_scr, sa_scr, sb_scr,
                           *, lam_init, tq, heads_per_step):
    i = pl.program_id(2)
    heads = range(heads_per_step)
    lane = lax.broadcasted_iota(jnp.int32, (tq, HEAD_DIM), 1)
    own_half = (lane < DIFF_HALF, lane >= DIFF_HALF)
    bias_lane0 = (DIFF_HALF, 0)

    @pl.when(i == 0)
    def _():
        for g in heads:
            _fill_values_transposed(v_ref, vt_scr, g, tq)
            for n in range(ka_scr.shape[2]):
                k = k_ref[n * tq:(n + 1) * tq, _head_lanes(g)].astype(_F32)
                for c in range(2):
                    ka_scr[g, c, n] = jnp.where(
                        own_half[c], k, _bias_key_lanes(lane, bias_lane0[c])).astype(_BF16)

    key_idx = lax.broadcasted_iota(jnp.int32, (tq, tq), 0)
    qry_idx = lax.broadcasted_iota(jnp.int32, (tq, tq), 1)
    qry_pos = lax.broadcasted_iota(jnp.int32, (1, tq), 1).astype(_F32)

    streams = []
    for g in heads:
        slope = slopes_ref[pl.program_id(1) * heads_per_step + g] * _LOG2E
        q = q_ref[:, _head_lanes(g)].astype(_F32)
        for c in range(2):
            qc = jnp.where(own_half[c], q, _bias_query_lanes(slope, lane, bias_lane0[c]))
            streams.append((g, c, slope, qc.astype(_BF16)))

    def scores(n, j, s_buf):
        g, c, _, qc = streams[n]
        s = _dot_nt(ka_scr[g, c, j], qc)
        s_buf[n] = s
        return jnp.max(s, axis=0, keepdims=True)

    def first_update(n, s_buf):
        g, c, slope, _ = streams[n]
        z = jnp.where(key_idx <= qry_idx, s_buf[n], _NEG_INF)
        return _first_flash_step(z, -slope * qry_pos, vt_scr[g, i], acc_scr.at[g, c])

    def update(n, j, s_buf, tile_max, running_max, valid=None):
        g, c, slope, _ = streams[n]
        rb = -slope * qry_pos - slope * ((i - j) * tq).astype(_F32)
        if valid is not None:
            rb = jnp.where(valid, rb, _NEG_INF)
        return _flash_step(s_buf.at[n], tile_max, rb, vt_scr[g, j], running_max,
                           acc_scr.at[g, c])

    _pipelined_flash(len(streams), i, scores, first_update, update, sa_scr, sb_scr)

    lam = (jnp.exp(jnp.sum(lq1_ref[...] * lk1_ref[...], axis=-1, keepdims=True))
           - jnp.exp(jnp.sum(lq2_ref[...] * lk2_ref[...], axis=-1, keepdims=True))
           + lam_init)
    for g in heads:
        o_t = _normalised(acc_scr[g, 0]) - lam * _normalised(acc_scr[g, 1])
        ms = jnp.mean(o_t * o_t, axis=0, keepdims=True)
        o = (o_t * lax.rsqrt(ms + EPS)).T * subg_ref[...] * (1.0 - lam_init)
        gate = gate_ref[:, _head_lanes(g)].astype(_F32)
        o_ref[:, _head_lanes(g)] = (o * _silu(gate)).astype(o_ref.dtype)


def diff_attention(proj, slopes, lq1, lk1, lq2, lk2, subg, *, lam_init, tq, heads_per_step):
    b, s, _ = proj.shape
    hps = heads_per_step
    assert s % tq == 0 and DIFF_HEADS % hps == 0
    width = hps * HEAD_DIM
    vec = lambda n: pl.BlockSpec((1, n), lambda bi, hi, i: (0, 0))
    spec = functools.partial(_head_group_spec, tq=tq, width=width, hps=hps, seq_buffers=2)
    scratch = [pltpu.VMEM((hps, s // tq, _VT_ROWS, tq), _BF16),
               pltpu.VMEM((hps, 2, s // tq, tq, HEAD_DIM), _BF16),
               pltpu.VMEM((hps, 2, _VT_ROWS, tq), _F32),
               pltpu.VMEM((2 * hps, tq, tq), _F32),
               pltpu.VMEM((2 * hps, tq, tq), _F32)]
    return pl.pallas_call(
        functools.partial(_diff_attention_kernel, lam_init=lam_init, tq=tq,
                          heads_per_step=hps),
        grid=(b, DIFF_HEADS // hps, s // tq),
        in_specs=[
            pl.BlockSpec(memory_space=pltpu.SMEM),
            vec(DIFF_HALF), vec(DIFF_HALF), vec(DIFF_HALF), vec(DIFF_HALF), vec(HEAD_DIM),
            spec(tq, first_blk=_DQ_BLK), spec(s, first_blk=_DK_BLK), spec(s, first_blk=_DV_BLK),
            spec(tq, first_blk=_GATE_BLK),
        ],
        out_specs=pl.BlockSpec((None, tq, width), lambda bi, hi, i: (bi, i, hi)),
        out_shape=jax.ShapeDtypeStruct((b, s, DIFF_WIDTH), _BF16),
        scratch_shapes=scratch,
        compiler_params=pltpu.CompilerParams(
            dimension_semantics=("parallel", "parallel", "arbitrary"),
            vmem_limit_bytes=_VMEM_LIMIT),
        name="diff_attention",
    )(slopes, lq1.reshape(1, -1), lk1.reshape(1, -1), lq2.reshape(1, -1), lk2.reshape(1, -1),
      subg.reshape(1, -1), proj, proj, proj, proj)


def _moba_attention_kernel(slopes_ref, q_ref, k_ref, v_ref, gate_ref, o_ref,
                           kmean_scr, kpieces_scr, vt_scr, sel_scr, acc_scr, sa_scr, sb_scr,
                           *, n_blocks, heads_per_step):
    i = pl.program_id(2)
    tq = MOBA_BLOCK
    heads = range(heads_per_step)

    lane = lax.broadcasted_iota(jnp.int32, (tq, HEAD_DIM), 1)

    @pl.when(i == 0)
    def _():
        for g in heads:
            _fill_values_transposed(v_ref, vt_scr, g, tq)
            for n in range(n_blocks):
                k = k_ref[n * MOBA_BLOCK:(n + 1) * MOBA_BLOCK, _head_lanes(g)]
                kmean_scr[g, n:n + 1, :] = (jnp.sum(k.astype(_F32), axis=0, keepdims=True)
                                            * (1.0 / MOBA_BLOCK))
            rest = kmean_scr[g]
            for t in range(_F32_AS_BF16_PIECES):
                piece = rest.astype(_BF16)
                kpieces_scr[g, t * n_blocks:(t + 1) * n_blocks, :] = piece
                rest = rest - piece.astype(_F32)

    key_idx = lax.broadcasted_iota(jnp.int32, (tq, tq), 0)
    qry_idx = lax.broadcasted_iota(jnp.int32, (tq, tq), 1)
    qry_pos = lax.broadcasted_iota(jnp.int32, (1, tq), 1).astype(_F32)
    blk = lax.broadcasted_iota(jnp.int32, (n_blocks, tq), 0)
    past = blk < i

    slopes = [slopes_ref[pl.program_id(1) * heads_per_step + g] * _LOG2E for g in heads]
    qs = [q_ref[:, _head_lanes(g)] for g in heads]
    qas = [jnp.concatenate([qs[g], _bias_query_lanes(slopes[g], lane, 0).astype(_BF16)], axis=1)
           for g in heads]
    key_pos_lanes = _bias_key_lanes(lane, 0).astype(_BF16)

    def scores(g, j, s_buf):
        ka = jnp.concatenate([k_ref[pl.ds(j * tq, tq), _head_lanes(g)], key_pos_lanes], axis=1)
        s = _dot_nt(ka, qas[g])
        s_buf[g] = s
        return jnp.max(s, axis=0, keepdims=True)

    def select_blocks(g):
        parts = _dot_nt(kpieces_scr[g], qs[g])
        gate = sum(parts[t * n_blocks:(t + 1) * n_blocks] for t in range(_F32_AS_BF16_PIECES))
        gate = jnp.where(past, gate, _NEG_INF)
        for r0 in range(0, n_blocks, _SUBLANES):
            tile = gate[r0:r0 + _SUBLANES]
            row = lax.broadcasted_iota(jnp.int32, tile.shape, 0) + r0
            rank = jnp.zeros(tile.shape, _F32)
            for n in range(n_blocks):
                g_n = gate[n:n + 1, :]
                if n < r0:
                    beats = g_n >= tile
                elif n >= r0 + _SUBLANES:
                    beats = g_n > tile
                else:
                    beats = (g_n > tile) | ((g_n == tile) & (n < row))
                rank = rank + jnp.where(beats, 1.0, 0.0)
            sel_scr[g, r0:r0 + _SUBLANES, :] = jnp.where(
                (row < i) & (rank < MOBA_TOPK), 0.0, _NEG_INF)

    def first_update(g, s_buf):
        select_blocks(g)
        z = jnp.where(key_idx <= qry_idx, s_buf[g], _NEG_INF)
        return _first_flash_step(z, -slopes[g] * qry_pos, vt_scr[g, i], acc_scr.at[g])

    def update(g, j, s_buf, tile_max, running_max, valid=None):
        rb = (-slopes[g] * qry_pos - slopes[g] * ((i - j) * tq).astype(_F32)
              + sel_scr[g, pl.ds(j, 1), :])
        return _flash_step(s_buf.at[g], tile_max, rb, vt_scr[g, j], running_max, acc_scr.at[g])

    _pipelined_flash(heads_per_step, i, scores, first_update, update, sa_scr, sb_scr)
    for g in heads:
        o = _normalised(acc_scr[g]).T
        gate = gate_ref[:, _head_lanes(g)].astype(_F32)
        o_ref[:, _head_lanes(g)] = (o * _silu(gate)).astype(o_ref.dtype)


def moba_attention(proj, slopes, *, heads_per_step):
    b, s, _ = proj.shape
    hps = heads_per_step
    assert s % MOBA_BLOCK == 0 and MOBA_HEADS % hps == 0
    n_blocks = s // MOBA_BLOCK
    assert n_blocks % _SUBLANES == 0
    tq = MOBA_BLOCK
    width = hps * HEAD_DIM
    spec = functools.partial(_head_group_spec, tq=tq, width=width, hps=hps, seq_buffers=1)
    return pl.pallas_call(
        functools.partial(_moba_attention_kernel, n_blocks=n_blocks, heads_per_step=hps),
        grid=(b, MOBA_HEADS // hps, n_blocks),
        in_specs=[
            pl.BlockSpec(memory_space=pltpu.SMEM),
            spec(tq, first_blk=_MQ_BLK), spec(s, first_blk=_MK_BLK), spec(s, first_blk=_MV_BLK),
            spec(tq, first_blk=_GATE_BLK + DIFF_HEADS),
        ],
        out_specs=pl.BlockSpec((None, tq, width), lambda bi, hi, i: (bi, i, hi)),
        out_shape=jax.ShapeDtypeStruct((b, s, MOBA_WIDTH), _BF16),
        scratch_shapes=[pltpu.VMEM((hps, n_blocks, HEAD_DIM), _F32),
                        pltpu.VMEM((hps, _F32_AS_BF16_PIECES * n_blocks, HEAD_DIM), _BF16),
                        pltpu.VMEM((hps, n_blocks, _VT_ROWS, tq), _BF16),
                        pltpu.VMEM((hps, n_blocks, tq), _F32),
                        pltpu.VMEM((hps, _VT_ROWS, tq), _F32),
                        pltpu.VMEM((hps, tq, tq), _F32),
                        pltpu.VMEM((hps, tq, tq), _F32)],
        compiler_params=pltpu.CompilerParams(
            dimension_semantics=("parallel", "parallel", "arbitrary"),
            vmem_limit_bytes=_VMEM_LIMIT),
        name="moba_attention",
    )(slopes, proj, proj, proj, proj)


def _layer_tail_kernel(yd_ref, ym_ref, x_ref, wa_ref, wb_ref, g_ref, wq_ref, kv_ref, wo_ref,
                       fg_ref, o_ref, q_scr, o_scr, *, apply_final, sub_rows):
    subs = [slice(r, r + sub_rows) for r in range(0, x_ref.shape[0], sub_rows)]
    xs = [x_ref[r, :] + _dot(yd_ref[r, :], wa_ref[...]) + _dot(ym_ref[r, :], wb_ref[...])
          for r in subs]
    for r, x in zip(subs, xs):
        ms = jnp.mean(x * x, axis=-1, keepdims=True)
        h = (x * lax.rsqrt(ms + EPS) * g_ref[...]).astype(_BF16)
        q_scr[r, :] = (_dot(h, wq_ref[...]) * (HEAD_DIM ** -0.5 * _LOG2E)).astype(_BF16)
    for r in subs:
        for hh in range(MEM_HEADS):
            lo, hi = hh * HEAD_DIM, (hh + 1) * HEAD_DIM
            s = _dot_nt(q_scr[r, lo:hi], kv_ref[:, lo:hi])
            p = jnp.exp2(s - jnp.max(s, axis=-1, keepdims=True))
            l = jnp.sum(p, axis=-1, keepdims=True)
            o_h = _dot(p.astype(_BF16), kv_ref[:, MEM_WIDTH + lo:MEM_WIDTH + hi]) / l
            o_scr[r, lo:hi] = o_h.astype(_BF16)
    for r, x in zip(subs, xs):
        x = x + _dot(o_scr[r, :], wo_ref[...])
        if apply_final:
            ms = jnp.mean(x * x, axis=-1, keepdims=True)
            x = x * lax.rsqrt(ms + EPS) * fg_ref[...]
        o_ref[r, :] = x


def layer_tail(yd, ym, x, w_out, g, wq, kv, wo, final_g, *, apply_final, tm, sub_rows):
    b, s, d = x.shape
    mlen = kv.shape[1]
    ka, kb = yd.shape[-1], ym.shape[-1]
    assert s % tm == 0 and ka == kb and w_out.shape[0] == ka + kb
    rows = lambda width: pl.BlockSpec((None, tm, width), lambda bi, i: (bi, i, 0))
    whole = lambda shape: pl.BlockSpec(shape, lambda bi, i: (0,) * len(shape))
    return pl.pallas_call(
        functools.partial(_layer_tail_kernel, apply_final=apply_final, sub_rows=sub_rows),
        grid=(b, s // tm),
        in_specs=[
            rows(ka), rows(kb), rows(d),
            pl.BlockSpec((ka, d), lambda bi, i: (0, 0)),
            pl.BlockSpec((kb, d), lambda bi, i: (1, 0)),
            whole((1, d)), whole((d, MEM_WIDTH)),
            pl.BlockSpec((None, mlen, 2 * MEM_WIDTH), lambda bi, i: (bi, 0, 0)),
            whole((MEM_WIDTH, d)), whole((1, d)),
        ],
        out_specs=rows(d),
        out_shape=jax.ShapeDtypeStruct((b, s, d), _F32),
        scratch_shapes=[pltpu.VMEM((tm, MEM_WIDTH), _BF16),
                        pltpu.VMEM((tm, MEM_WIDTH), _BF16)],
        compiler_params=pltpu.CompilerParams(
            dimension_semantics=("parallel", "parallel"),
            vmem_limit_bytes=_VMEM_LIMIT),
        name="layer_tail",
    )(yd, ym, x, w_out, w_out, g.reshape(1, d), wq, kv, wo, final_g.reshape(1, d))


def _in_proj_col_scale(n_cols):
    cs = np.ones((n_cols,), np.float32)
    cs[_DQ_BLK * HEAD_DIM:_DK_BLK * HEAD_DIM] = DIFF_HALF ** -0.5 * _LOG2E
    cs[_MQ_BLK * HEAD_DIM:_MK_BLK * HEAD_DIM] = HEAD_DIM ** -0.5 * _LOG2E
    return jnp.asarray(cs)


def _alibi_slopes(n):
    return jnp.asarray(2.0 ** (-8.0 * np.arange(1, n + 1) / n), dtype=_F32)


_IN_PROJ_TILE = (1024, 1024)
_ATTN_Q_TILE = 256
_DIFF_HEADS_PER_STEP = 4
_MOBA_HEADS_PER_STEP = 8
_TAIL_ROW_TILE = 512
_TAIL_SUB_ROWS = 256


def kernel(x, mem, norm_mix_g, w_in, lambda_q1, lambda_k1, lambda_q2, lambda_k2, subln_g,
           w_out, norm_mem_g, mem_norm_g, w_q_mem, w_kv_mem, w_o_mem, final_g):
    b, s, d = x.shape
    mlen = mem.shape[1]
    depth = w_in.shape[0]
    diff_slopes = _alibi_slopes(DIFF_HEADS)
    moba_slopes = _alibi_slopes(MOBA_HEADS)
    mem2 = mem.reshape(b * mlen, d)
    for l in range(depth):
        lam_init = 0.8 - 0.6 * math.exp(-0.3 * l)
        proj = rms_matmul(x.reshape(b * s, d), norm_mix_g[l], w_in[l].astype(_BF16),
                          _in_proj_col_scale(w_in.shape[-1]),
                          tm=_IN_PROJ_TILE[0], tn=_IN_PROJ_TILE[1]).reshape(b, s, -1)
        y_d = diff_attention(proj, diff_slopes, lambda_q1[l], lambda_k1[l], lambda_q2[l],
                             lambda_k2[l], subln_g[l], lam_init=lam_init, tq=_ATTN_Q_TILE,
                             heads_per_step=_DIFF_HEADS_PER_STEP)
        y_m = moba_attention(proj, moba_slopes, heads_per_step=_MOBA_HEADS_PER_STEP)
        kv = rms_matmul(mem2, mem_norm_g[l], w_kv_mem[l].astype(_BF16),
                        tm=b * mlen, tn=2 * MEM_WIDTH).reshape(b, mlen, 2 * MEM_WIDTH)
        x = layer_tail(y_d, y_m, x, w_out[l].astype(_BF16), norm_mem_g[l],
                       w_q_mem[l].astype(_BF16), kv, w_o_mem[l].astype(_BF16), final_g,
                       apply_final=(l == depth - 1), tm=_TAIL_ROW_TILE,
                       sub_rows=_TAIL_SUB_ROWS)
    return x
```

```python
import functools
import math

import jax
import jax.numpy as jnp
import numpy as np
from jax import lax
from jax.experimental import pallas as pl
from jax.experimental.pallas import tpu as pltpu

HEAD_DIM = 128
DIFF_HEADS = 8
DIFF_HALF = HEAD_DIM // 2
DIFF_WIDTH = DIFF_HEADS * HEAD_DIM
MOBA_HEADS = 8
MOBA_WIDTH = MOBA_HEADS * HEAD_DIM
MOBA_BLOCK = 256
MOBA_TOPK = 3
MEM_HEADS = 4
MEM_WIDTH = MEM_HEADS * HEAD_DIM
EPS = 1e-6

_DQ_BLK = 0
_DK_BLK = DIFF_HEADS
_DV_BLK = 2 * DIFF_HEADS
_MQ_BLK = 3 * DIFF_HEADS
_MK_BLK = 3 * DIFF_HEADS + MOBA_HEADS
_MV_BLK = 3 * DIFF_HEADS + 2 * MOBA_HEADS
_GATE_BLK = 3 * DIFF_HEADS + 3 * MOBA_HEADS

_V7X_VMEM_BYTES = 64 * 1024 * 1024
_VMEM_LIMIT = _V7X_VMEM_BYTES * 3 // 4

_BF16 = jnp.bfloat16
_F32 = jnp.float32
_NEG_INF = float("-inf")


def _dot_nt(a, b):
    return lax.dot_general(a, b, (((1,), (1,)), ((), ())), preferred_element_type=_F32)


def _dot(a, b):
    return jnp.dot(a, b, preferred_element_type=_F32)


def _silu(g):
    return g * jax.nn.sigmoid(g)


def _rms_matmul_kernel(x_ref, g_ref, w_ref, *rest, has_col_scale):
    cs_ref, o_ref, h_scr = rest if has_col_scale else (None,) + rest

    @pl.when(pl.program_id(1) == 0)
    def _():
        x = x_ref[...]
        ms = jnp.mean(x * x, axis=-1, keepdims=True)
        h_scr[...] = (x * lax.rsqrt(ms + EPS) * g_ref[...]).astype(_BF16)

    out = _dot(h_scr[...], w_ref[...].astype(_BF16))
    if has_col_scale:
        out = out * cs_ref[...]
    o_ref[...] = out.astype(o_ref.dtype)


def rms_matmul(x, g, w, col_scale=None, *, tm, tn, out_dtype=_BF16):
    m, k = x.shape
    n = w.shape[1]
    assert m % tm == 0 and n % tn == 0
    has_cs = col_scale is not None
    cs_specs = [pl.BlockSpec((1, tn), lambda i, j: (0, j))] if has_cs else []
    cs_args = [col_scale.reshape(1, n)] if has_cs else []
    return pl.pallas_call(
        functools.partial(_rms_matmul_kernel, has_col_scale=has_cs),
        grid=(m // tm, n // tn),
        in_specs=[
            pl.BlockSpec((tm, k), lambda i, j: (i, 0)),
            pl.BlockSpec((1, k), lambda i, j: (0, 0)),
            pl.BlockSpec((k, tn), lambda i, j: (0, j)),
        ] + cs_specs,
        out_specs=pl.BlockSpec((tm, tn), lambda i, j: (i, j)),
        out_shape=jax.ShapeDtypeStruct((m, n), out_dtype),
        scratch_shapes=[pltpu.VMEM((tm, k), _BF16)],
        compiler_params=pltpu.CompilerParams(
            dimension_semantics=("parallel", "arbitrary"),
            vmem_limit_bytes=_VMEM_LIMIT),
        name="rms_matmul",
    )(x, g.reshape(1, k), w, *cs_args)


_LOG2E = math.log2(math.e)
_ONES_ROWS = 16
_VT_ROWS = HEAD_DIM + _ONES_ROWS


def _first_flash_step(z, rb, vt, acc_ref):
    m = jnp.max(z, axis=0, keepdims=True) + rb
    p = jnp.exp2(z - (m - rb))
    acc_ref[...] = _dot(vt, p.astype(_BF16))
    return m


def _flash_step(z_ref, z_max, rb, vt, m, acc_ref):
    m_new = jnp.maximum(m, z_max + rb)
    p = jnp.exp2(z_ref[...] - (m_new - rb))
    acc_ref[...] = jnp.exp2(m - m_new) * acc_ref[...] + _dot(vt, p.astype(_BF16))
    return m_new


_F32_AS_BF16_PIECES = 3
_BIAS_LANES = _F32_AS_BF16_PIECES
_SUBLANES = 8


def _bias_key_lanes(lane, lane0):
    pos = lax.broadcasted_iota(jnp.int32, lane.shape, 0).astype(_F32)
    return jnp.where((lane >= lane0) & (lane < lane0 + _BIAS_LANES), pos, 0.0)


def _bias_query_lanes(slope, lane, lane0):
    rest = jnp.full(lane.shape, slope, _F32)
    out = jnp.zeros(lane.shape, _F32)
    for t in range(_BIAS_LANES):
        piece = rest.astype(_BF16).astype(_F32)
        out = jnp.where(lane == lane0 + t, piece, out)
        rest = rest - piece
    return out


def _fill_values_transposed(v_ref, vt_scr, g, tk):
    for n in range(vt_scr.shape[1]):
        vt_scr[g, n, :HEAD_DIM, :] = v_ref[n * tk:(n + 1) * tk, _head_lanes(g)].T
        vt_scr[g, n, HEAD_DIM:, :] = jnp.ones((_ONES_ROWS, tk), _BF16)


def _normalised(acc):
    return acc[:HEAD_DIM] / acc[HEAD_DIM:HEAD_DIM + 1]


def _pipelined_flash(n_streams, n_past, scores, first_update, update, sa_scr, sb_scr):
    streams = range(n_streams)
    for n in streams:
        scores(n, n_past, sb_scr)
    max_a = tuple(scores(n, 0, sa_scr) for n in streams)
    running = tuple(first_update(n, sb_scr) for n in streams)

    def pair(t, carry):
        running, max_a = (list(c) for c in carry)
        j0 = 2 * t
        j1 = j0 + 1
        j2 = jnp.minimum(j1 + 1, n_past)
        max_b = [None] * n_streams
        for n in streams:
            max_b[n] = scores(n, j1, sb_scr)
            running[n] = update(n, j0, sa_scr, max_a[n], running[n])
        for n in streams:
            max_a[n] = scores(n, j2, sa_scr)
            running[n] = update(n, j1, sb_scr, max_b[n], running[n], valid=j1 < n_past)
        return tuple(running), tuple(max_a)

    def two_pairs(t, carry):
        return pair(2 * t + 1, pair(2 * t, carry))

    n_pairs = (n_past + 1) // 2
    carry = lax.fori_loop(0, n_pairs // 2, two_pairs, (running, max_a))
    lax.fori_loop(n_pairs - n_pairs % 2, n_pairs, pair, carry)


def _head_lanes(g):
    return slice(g * HEAD_DIM, (g + 1) * HEAD_DIM)


def _head_group_spec(rows, tq, width, first_blk, hps, seq_buffers):
    assert first_blk % hps == 0
    index_map = lambda bi, hi, i: (bi, i if rows == tq else 0, first_blk // hps + hi)
    if rows == tq:
        return pl.BlockSpec((None, rows, width), index_map)
    return pl.BlockSpec((None, rows, width), index_map,
                        pipeline_mode=pl.Buffered(seq_buffers))


def _diff_attention_kernel(slopes_ref, lq1_ref, lk1_ref, lq2_ref, lk2_ref, subg_ref,
                           q_ref, k_ref, v_ref, gate_ref, o_ref,
                           vt_scr, ka_scr, acc_scr, sa_scr, sb_scr,
                           *, lam_init, tq, heads_per_step):
    i = pl.program_id(2)
    heads = range(heads_per_step)
    lane = lax.broadcasted_iota(jnp.int32, (tq, HEAD_DIM), 1)
    own_half = (lane < DIFF_HALF, lane >= DIFF_HALF)
    bias_lane0 = (DIFF_HALF, 0)

    @pl.when(i == 0)
    def _():
        for g in heads:
            _fill_values_transposed(v_ref, vt_scr, g, tq)
            for n in range(ka_scr.shape[2]):
                k = k_ref[n * tq:(n + 1) * tq, _head_lanes(g)].astype(_F32)
                for c in range(2):
                    ka_scr[g, c, n] = jnp.where(
                        own_half[c], k, _bias_key_lanes(lane, bias_lane0[c])).astype(_BF16)

    key_idx = lax.broadcasted_iota(jnp.int32, (tq, tq), 0)
    qry_idx = lax.broadcasted_iota(jnp.int32, (tq, tq), 1)
    qry_pos = lax.broadcasted_iota(jnp.int32, (1, tq), 1).astype(_F32)

    streams = []
    for g in heads:
        slope = slopes_ref[pl.program_id(1) * heads_per_step + g] * _LOG2E
        q = q_ref[:, _head_lanes(g)].astype(_F32)
        for c in range(2):
            qc = jnp.where(own_half[c], q, _bias_query_lanes(slope, lane, bias_lane0[c]))
            streams.append((g, c, slope, qc.astype(_BF16)))

    def scores(n, j, s_buf):
        g, c, _, qc = streams[n]
        s = _dot_nt(ka_scr[g, c, j], qc)
        s_buf[n] = s
        return jnp.max(s, axis=0, keepdims=True)

    def first_update(n, s_buf):
        g, c, slope, _ = streams[n]
        z = jnp.where(key_idx <= qry_idx, s_buf[n], _NEG_INF)
        return _first_flash_step(z, -slope * qry_pos, vt_scr[g, i], acc_scr.at[g, c])

    def update(n, j, s_buf, tile_max, running_max, valid=None):
        g, c, slope, _ = streams[n]
        rb = -slope * qry_pos - slope * ((i - j) * tq).astype(_F32)
        if valid is not None:
            rb = jnp.where(valid, rb, _NEG_INF)
        return _flash_step(s_buf.at[n], tile_max, rb, vt_scr[g, j], running_max,
                           acc_scr.at[g, c])

    _pipelined_flash(len(streams), i, scores, first_update, update, sa_scr, sb_scr)

    lam = (jnp.exp(jnp.sum(lq1_ref[...] * lk1_ref[...], axis=-1, keepdims=True))
           - jnp.exp(jnp.sum(lq2_ref[...] * lk2_ref[...], axis=-1, keepdims=True))
           + lam_init)
    for g in heads:
        o_t = _normalised(acc_scr[g, 0]) - lam * _normalised(acc_scr[g, 1])
        ms = jnp.mean(o_t * o_t, axis=0, keepdims=True)
        o = (o_t * lax.rsqrt(ms + EPS)).T * subg_ref[...] * (1.0 - lam_init)
        gate = gate_ref[:, _head_lanes(g)].astype(_F32)
        o_ref[:, _head_lanes(g)] = (o * _silu(gate)).astype(o_ref.dtype)


def diff_attention(proj, slopes, lq1, lk1, lq2, lk2, subg, *, lam_init, tq, heads_per_step):
    b, s, _ = proj.shape
    hps = heads_per_step
    assert s % tq == 0 and DIFF_HEADS % hps == 0
    width = hps * HEAD_DIM
    vec = lambda n: pl.BlockSpec((1, n), lambda bi, hi, i: (0, 0))
    spec = functools.partial(_head_group_spec, tq=tq, width=width, hps=hps, seq_buffers=2)
    scratch = [pltpu.VMEM((hps, s // tq, _VT_ROWS, tq), _BF16),
               pltpu.VMEM((hps, 2, s // tq, tq, HEAD_DIM), _BF16),
               pltpu.VMEM((hps, 2, _VT_ROWS, tq), _F32),
               pltpu.VMEM((2 * hps, tq, tq), _F32),
               pltpu.VMEM((2 * hps, tq, tq), _F32)]
    return pl.pallas_call(
        functools.partial(_diff_attention_kernel, lam_init=lam_init, tq=tq,
                          heads_per_step=hps),
        grid=(b, DIFF_HEADS // hps, s // tq),
        in_specs=[
            pl.BlockSpec(memory_space=pltpu.SMEM),
            vec(DIFF_HALF), vec(DIFF_HALF), vec(DIFF_HALF), vec(DIFF_HALF), vec(HEAD_DIM),
            spec(tq, first_blk=_DQ_BLK), spec(s, first_blk=_DK_BLK), spec(s, first_blk=_DV_BLK),
            spec(tq, first_blk=_GATE_BLK),
        ],
        out_specs=pl.BlockSpec((None, tq, width), lambda bi, hi, i: (bi, i, hi)),
        out_shape=jax.ShapeDtypeStruct((b, s, DIFF_WIDTH), _BF16),
        scratch_shapes=scratch,
        compiler_params=pltpu.CompilerParams(
            dimension_semantics=("parallel", "parallel", "arbitrary"),
            vmem_limit_bytes=_VMEM_LIMIT),
        name="diff_attention",
    )(slopes, lq1.reshape(1, -1), lk1.reshape(1, -1), lq2.reshape(1, -1), lk2.reshape(1, -1),
      subg.reshape(1, -1), proj, proj, proj, proj)


def _moba_attention_kernel(slopes_ref, q_ref, k_ref, v_ref, gate_ref, o_ref,
                           kmean_scr, kpieces_scr, vt_scr, sel_scr, acc_scr, sa_scr, sb_scr,
                           *, n_blocks, heads_per_step):
    i = pl.program_id(2)
    tq = MOBA_BLOCK
    heads = range(heads_per_step)

    lane = lax.broadcasted_iota(jnp.int32, (tq, HEAD_DIM), 1)

    @pl.when(i == 0)
    def _():
        for g in heads:
            _fill_values_transposed(v_ref, vt_scr, g, tq)
            for n in range(n_blocks):
                k = k_ref[n * MOBA_BLOCK:(n + 1) * MOBA_BLOCK, _head_lanes(g)]
                kmean_scr[g, n:n + 1, :] = (jnp.sum(k.astype(_F32), axis=0, keepdims=True)
                                            * (1.0 / MOBA_BLOCK))
            rest = kmean_scr[g]
            for t in range(_F32_AS_BF16_PIECES):
                piece = rest.astype(_BF16)
                kpieces_scr[g, t * n_blocks:(t + 1) * n_blocks, :] = piece
                rest = rest - piece.astype(_F32)

    key_idx = lax.broadcasted_iota(jnp.int32, (tq, tq), 0)
    qry_idx = lax.broadcasted_iota(jnp.int32, (tq, tq), 1)
    qry_pos = lax.broadcasted_iota(jnp.int32, (1, tq), 1).astype(_F32)
    blk = lax.broadcasted_iota(jnp.int32, (n_blocks, tq), 0)
    past = blk < i

    slopes = [slopes_ref[pl.program_id(1) * heads_per_step + g] * _LOG2E for g in heads]
    qs = [q_ref[:, _head_lanes(g)] for g in heads]
    qas = [jnp.concatenate([qs[g], _bias_query_lanes(slopes[g], lane, 0).astype(_BF16)], axis=1)
           for g in heads]
    key_pos_lanes = _bias_key_lanes(lane, 0).astype(_BF16)

    def scores(g, j, s_buf):
        ka = jnp.concatenate([k_ref[pl.ds(j * tq, tq), _head_lanes(g)], key_pos_lanes], axis=1)
        s = _dot_nt(ka, qas[g])
        s_buf[g] = s
        return jnp.max(s, axis=0, keepdims=True)

    def select_blocks(g):
        parts = _dot_nt(kpieces_scr[g], qs[g])
        gate = sum(parts[t * n_blocks:(t + 1) * n_blocks] for t in range(_F32_AS_BF16_PIECES))
        gate = jnp.where(past, gate, _NEG_INF)
        for r0 in range(0, n_blocks, _SUBLANES):
            tile = gate[r0:r0 + _SUBLANES]
            row = lax.broadcasted_iota(jnp.int32, tile.shape, 0) + r0
            rank = jnp.zeros(tile.shape, _F32)
            for n in range(n_blocks):
                g_n = gate[n:n + 1, :]
                if n < r0:
                    beats = g_n >= tile
                elif n >= r0 + _SUBLANES:
                    beats = g_n > tile
                else:
                    beats = (g_n > tile) | ((g_n == tile) & (n < row))
                rank = rank + jnp.where(beats, 1.0, 0.0)
            sel_scr[g, r0:r0 + _SUBLANES, :] = jnp.where(
                (row < i) & (rank < MOBA_TOPK), 0.0, _NEG_INF)

    def first_update(g, s_buf):
        select_blocks(g)
        z = jnp.where(key_idx <= qry_idx, s_buf[g], _NEG_INF)
        return _first_flash_step(z, -slopes[g] * qry_pos, vt_scr[g, i], acc_scr.at[g])

    def update(g, j, s_buf, tile_max, running_max, valid=None):
        rb = (-slopes[g] * qry_pos - slopes[g] * ((i - j) * tq).astype(_F32)
              + sel_scr[g, pl.ds(j, 1), :])
        return _flash_step(s_buf.at[g], tile_max, rb, vt_scr[g, j], running_max, acc_scr.at[g])

    _pipelined_flash(heads_per_step, i, scores, first_update, update, sa_scr, sb_scr)
    for g in heads:
        o = _normalised(acc_scr[g]).T
        gate = gate_ref[:, _head_lanes(g)].astype(_F32)
        o_ref[:, _head_lanes(g)] = (o * _silu(gate)).astype(o_ref.dtype)


def moba_attention(proj, slopes, *, heads_per_step):
    b, s, _ = proj.shape
    hps = heads_per_step
    assert s % MOBA_BLOCK == 0 and MOBA_HEADS % hps == 0
    n_blocks = s // MOBA_BLOCK
    assert n_blocks % _SUBLANES == 0
    tq = MOBA_BLOCK
    width = hps * HEAD_DIM
    spec = functools.partial(_head_group_spec, tq=tq, width=width, hps=hps, seq_buffers=1)
    return pl.pallas_call(
        functools.partial(_moba_attention_kernel, n_blocks=n_blocks, heads_per_step=hps),
        grid=(b, MOBA_HEADS // hps, n_blocks),
        in_specs=[
            pl.BlockSpec(memory_space=pltpu.SMEM),
            spec(tq, first_blk=_MQ_BLK), spec(s, first_blk=_MK_BLK), spec(s, first_blk=_MV_BLK),
            spec(tq, first_blk=_GATE_BLK + DIFF_HEADS),
        ],
        out_specs=pl.BlockSpec((None, tq, width), lambda bi, hi, i: (bi, i, hi)),
        out_shape=jax.ShapeDtypeStruct((b, s, MOBA_WIDTH), _BF16),
        scratch_shapes=[pltpu.VMEM((hps, n_blocks, HEAD_DIM), _F32),
                        pltpu.VMEM((hps, _F32_AS_BF16_PIECES * n_blocks, HEAD_DIM), _BF16),
                        pltpu.VMEM((hps, n_blocks, _VT_ROWS, tq), _BF16),
                        pltpu.VMEM((hps, n_blocks, tq), _F32),
                        pltpu.VMEM((hps, _VT_ROWS, tq), _F32),
                        pltpu.VMEM((hps, tq, tq), _F32),
                        pltpu.VMEM((hps, tq, tq), _F32)],
        compiler_params=pltpu.CompilerParams(
            dimension_semantics=("parallel", "parallel", "arbitrary"),
            vmem_limit_bytes=_VMEM_LIMIT),
        name="moba_attention",
    )(slopes, proj, proj, proj, proj)


def _layer_tail_kernel(yd_ref, ym_ref, x_ref, wa_ref, wb_ref, g_ref, wq_ref, kv_ref, wo_ref,
                       fg_ref, o_ref, q_scr, o_scr, *, apply_final, sub_rows):
    subs = [slice(r, r + sub_rows) for r in range(0, x_ref.shape[0], sub_rows)]
    xs = [x_ref[r, :] + _dot(yd_ref[r, :], wa_ref[...]) + _dot(ym_ref[r, :], wb_ref[...])
          for r in subs]
    for r, x in zip(subs, xs):
        ms = jnp.mean(x * x, axis=-1, keepdims=True)
        h = (x * lax.rsqrt(ms + EPS) * g_ref[...]).astype(_BF16)
        q_scr[r, :] = (_dot(h, wq_ref[...]) * (HEAD_DIM ** -0.5 * _LOG2E)).astype(_BF16)
    for r in subs:
        for hh in range(MEM_HEADS):
            lo, hi = hh * HEAD_DIM, (hh + 1) * HEAD_DIM
            s = _dot_nt(q_scr[r, lo:hi], kv_ref[:, lo:hi])
            p = jnp.exp2(s - jnp.max(s, axis=-1, keepdims=True))
            l = jnp.sum(p, axis=-1, keepdims=True)
            o_h = _dot(p.astype(_BF16), kv_ref[:, MEM_WIDTH + lo:MEM_WIDTH + hi]) / l
            o_scr[r, lo:hi] = o_h.astype(_BF16)
    for r, x in zip(subs, xs):
        x = x + _dot(o_scr[r, :], wo_ref[...])
        if apply_final:
            ms = jnp.mean(x * x, axis=-1, keepdims=True)
            x = x * lax.rsqrt(ms + EPS) * fg_ref[...]
        o_ref[r, :] = x


def layer_tail(yd, ym, x, w_out, g, wq, kv, wo, final_g, *, apply_final, tm, sub_rows):
    b, s, d = x.shape
    mlen = kv.shape[1]
    ka, kb = yd.shape[-1], ym.shape[-1]
    assert s % tm == 0 and ka == kb and w_out.shape[0] == ka + kb
    rows = lambda width: pl.BlockSpec((None, tm, width), lambda bi, i: (bi, i, 0))
    whole = lambda shape: pl.BlockSpec(shape, lambda bi, i: (0,) * len(shape))
    return pl.pallas_call(
        functools.partial(_layer_tail_kernel, apply_final=apply_final, sub_rows=sub_rows),
        grid=(b, s // tm),
        in_specs=[
            rows(ka), rows(kb), rows(d),
            pl.BlockSpec((ka, d), lambda bi, i: (0, 0)),
            pl.BlockSpec((kb, d), lambda bi, i: (1, 0)),
            whole((1, d)), whole((d, MEM_WIDTH)),
            pl.BlockSpec((None, mlen, 2 * MEM_WIDTH), lambda bi, i: (bi, 0, 0)),
            whole((MEM_WIDTH, d)), whole((1, d)),
        ],
        out_specs=rows(d),
        out_shape=jax.ShapeDtypeStruct((b, s, d), _F32),
        scratch_shapes=[pltpu.VMEM((tm, MEM_WIDTH), _BF16),
                        pltpu.VMEM((tm, MEM_WIDTH), _BF16)],
        compiler_params=pltpu.CompilerParams(
            dimension_semantics=("parallel", "parallel"),
            vmem_limit_bytes=_VMEM_LIMIT),
        name="layer_tail",
    )(yd, ym, x, w_out, w_out, g.reshape(1, d), wq, kv, wo, final_g.reshape(1, d))


def _in_proj_col_scale(n_cols):
    cs = np.ones((n_cols,), np.float32)
    cs[_DQ_BLK * HEAD_DIM:_DK_BLK * HEAD_DIM] = DIFF_HALF ** -0.5 * _LOG2E
    cs[_MQ_BLK * HEAD_DIM:_MK_BLK * HEAD_DIM] = HEAD_DIM ** -0.5 * _LOG2E
    return jnp.asarray(cs)


def _alibi_slopes(n):
    return jnp.asarray(2.0 ** (-8.0 * np.arange(1, n + 1) / n), dtype=_F32)


_IN_PROJ_TILE = (1024, 1024)
_ATTN_Q_TILE = 256
_DIFF_HEADS_PER_STEP = 4
_MOBA_HEADS_PER_STEP = 8
_TAIL_ROW_TILE = 512
_TAIL_SUB_ROWS = 256


def kernel(x, mem, norm_mix_g, w_in, lambda_q1, lambda_k1, lambda_q2, lambda_k2, subln_g,
           w_out, norm_mem_g, mem_norm_g, w_q_mem, w_kv_mem, w_o_mem, final_g):
    b, s, d = x.shape
    mlen = mem.shape[1]
    depth = w_in.shape[0]
    diff_slopes = _alibi_slopes(DIFF_HEADS)
    moba_slopes = _alibi_slopes(MOBA_HEADS)
    mem2 = mem.reshape(b * mlen, d)
    for l in range(depth):
        lam_init = 0.8 - 0.6 * math.exp(-0.3 * l)
        proj = rms_matmul(x.reshape(b * s, d), norm_mix_g[l], w_in[l],
                          _in_proj_col_scale(w_in.shape[-1]),
                          tm=_IN_PROJ_TILE[0], tn=_IN_PROJ_TILE[1]).reshape(b, s, -1)
        y_d = diff_attention(proj, diff_slopes, lambda_q1[l], lambda_k1[l], lambda_q2[l],
                             lambda_k2[l], subln_g[l], lam_init=lam_init, tq=_ATTN_Q_TILE,
                             heads_per_step=_DIFF_HEADS_PER_STEP)
        y_m = moba_attention(proj, moba_slopes, heads_per_step=_MOBA_HEADS_PER_STEP)
        kv = rms_matmul(mem2, mem_norm_g[l], w_kv_mem[l].astype(_BF16),
                        tm=b * mlen, tn=2 * MEM_WIDTH).reshape(b, mlen, 2 * MEM_WIDTH)
        x = layer_tail(y_d, y_m, x, w_out[l].astype(_BF16), norm_mem_g[l],
                       w_q_mem[l].astype(_BF16), kv, w_o_mem[l].astype(_BF16), final_g,
                       apply_final=(l == depth - 1), tm=_TAIL_ROW_TILE,
                       sub_rows=_TAIL_SUB_ROWS)
    return x
```

```python
import functools
import math

import jax
import jax.numpy as jnp
import numpy as np
from jax import lax
from jax.experimental import pallas as pl
from jax.experimental.pallas import tpu as pltpu

HEAD_DIM = 128
DIFF_HEADS = 8
DIFF_HALF = HEAD_DIM // 2
DIFF_WIDTH = DIFF_HEADS * HEAD_DIM
MOBA_HEADS = 8
MOBA_WIDTH = MOBA_HEADS * HEAD_DIM
MOBA_BLOCK = 256
MOBA_TOPK = 3
MEM_HEADS = 4
MEM_WIDTH = MEM_HEADS * HEAD_DIM
EPS = 1e-6

_DQ_BLK = 0
_DK_BLK = DIFF_HEADS
_DV_BLK = 2 * DIFF_HEADS
_MQ_BLK = 3 * DIFF_HEADS
_MK_BLK = 3 * DIFF_HEADS + MOBA_HEADS
_MV_BLK = 3 * DIFF_HEADS + 2 * MOBA_HEADS
_GATE_BLK = 3 * DIFF_HEADS + 3 * MOBA_HEADS

_V7X_VMEM_BYTES = 64 * 1024 * 1024
_VMEM_LIMIT = _V7X_VMEM_BYTES * 3 // 4

_BF16 = jnp.bfloat16
_F32 = jnp.float32
_NEG_INF = float("-inf")


def _dot_nt(a, b):
    return lax.dot_general(a, b, (((1,), (1,)), ((), ())), preferred_element_type=_F32)


def _dot(a, b):
    return jnp.dot(a, b, preferred_element_type=_F32)


def _silu(g):
    return g * jax.nn.sigmoid(g)


def _rms_matmul_kernel(x_ref, g_ref, w_ref, *rest, has_col_scale):
    cs_ref, o_ref, h_scr = rest if has_col_scale else (None,) + rest

    @pl.when(pl.program_id(1) == 0)
    def _():
        x = x_ref[...]
        ms = jnp.mean(x * x, axis=-1, keepdims=True)
        h_scr[...] = (x * lax.rsqrt(ms + EPS) * g_ref[...]).astype(_BF16)

    out = _dot(h_scr[...], w_ref[...].astype(_BF16))
    if has_col_scale:
        out = out * cs_ref[...]
    o_ref[...] = out.astype(o_ref.dtype)


def rms_matmul(x, g, w, col_scale=None, *, tm, tn, out_dtype=_BF16):
    m, k = x.shape
    n = w.shape[1]
    assert m % tm == 0 and n % tn == 0
    has_cs = col_scale is not None
    cs_specs = [pl.BlockSpec((1, tn), lambda i, j: (0, j))] if has_cs else []
    cs_args = [col_scale.reshape(1, n)] if has_cs else []
    return pl.pallas_call(
        functools.partial(_rms_matmul_kernel, has_col_scale=has_cs),
        grid=(m // tm, n // tn),
        in_specs=[
            pl.BlockSpec((tm, k), lambda i, j: (i, 0)),
            pl.BlockSpec((1, k), lambda i, j: (0, 0)),
            pl.BlockSpec((k, tn), lambda i, j: (0, j)),
        ] + cs_specs,
        out_specs=pl.BlockSpec((tm, tn), lambda i, j: (i, j)),
        out_shape=jax.ShapeDtypeStruct((m, n), out_dtype),
        scratch_shapes=[pltpu.VMEM((tm, k), _BF16)],
        compiler_params=pltpu.CompilerParams(
            dimension_semantics=("parallel", "arbitrary"),
            vmem_limit_bytes=_VMEM_LIMIT),
        name="rms_matmul",
    )(x, g.reshape(1, k), w, *cs_args)


_LOG2E = math.log2(math.e)
_ONES_ROWS = 16
_VT_ROWS = HEAD_DIM + _ONES_ROWS


def _first_flash_step(z, rb, vt, acc_ref):
    m = jnp.max(z, axis=0, keepdims=True) + rb
    p = jnp.exp2(z - (m - rb))
    acc_ref[...] = _dot(vt, p.astype(_BF16))
    return m


def _flash_step(z_ref, z_max, rb, vt, m, acc_ref):
    m_new = jnp.maximum(m, z_max + rb)
    p = jnp.exp2(z_ref[...] - (m_new - rb))
    acc_ref[...] = jnp.exp2(m - m_new) * acc_ref[...] + _dot(vt, p.astype(_BF16))
    return m_new


_F32_AS_BF16_PIECES = 3
_BIAS_LANES = _F32_AS_BF16_PIECES
_SUBLANES = 8


def _bias_key_lanes(lane, lane0):
    pos = lax.broadcasted_iota(jnp.int32, lane.shape, 0).astype(_F32)
    return jnp.where((lane >= lane0) & (lane < lane0 + _BIAS_LANES), pos, 0.0)


def _bias_query_lanes(slope, lane, lane0):
    rest = jnp.full(lane.shape, slope, _F32)
    out = jnp.zeros(lane.shape, _F32)
    for t in range(_BIAS_LANES):
        piece = rest.astype(_BF16).astype(_F32)
        out = jnp.where(lane == lane0 + t, piece, out)
        rest = rest - piece
    return out


def _fill_values_transposed(v_ref, vt_scr, g, tk):
    for n in range(vt_scr.shape[1]):
        vt_scr[g, n, :HEAD_DIM, :] = v_ref[n * tk:(n + 1) * tk, _head_lanes(g)].T
        vt_scr[g, n, HEAD_DIM:, :] = jnp.ones((_ONES_ROWS, tk), _BF16)


def _normalised(acc):
    return acc[:HEAD_DIM] / acc[HEAD_DIM:HEAD_DIM + 1]


def _pipelined_flash(n_streams, n_past, scores, first_update, update, sa_scr, sb_scr):
    streams = range(n_streams)
    for n in streams:
        scores(n, n_past, sb_scr)
    max_a = tuple(scores(n, 0, sa_scr) for n in streams)
    running = tuple(first_update(n, sb_scr) for n in streams)

    def pair(t, carry):
        running, max_a = (list(c) for c in carry)
        j0 = 2 * t
        j1 = j0 + 1
        j2 = jnp.minimum(j1 + 1, n_past)
        max_b = [None] * n_streams
        for n in streams:
            max_b[n] = scores(n, j1, sb_scr)
            running[n] = update(n, j0, sa_scr, max_a[n], running[n])
        for n in streams:
            max_a[n] = scores(n, j2, sa_scr)
            running[n] = update(n, j1, sb_scr, max_b[n], running[n], valid=j1 < n_past)
        return tuple(running), tuple(max_a)

    def two_pairs(t, carry):
        return pair(2 * t + 1, pair(2 * t, carry))

    n_pairs = (n_past + 1) // 2
    carry = lax.fori_loop(0, n_pairs // 2, two_pairs, (running, max_a))
    lax.fori_loop(n_pairs - n_pairs % 2, n_pairs, pair, carry)


def _head_lanes(g):
    return slice(g * HEAD_DIM, (g + 1) * HEAD_DIM)


def _head_group_spec(rows, tq, width, first_blk, hps, seq_buffers):
    assert first_blk % hps == 0
    index_map = lambda bi, hi, i: (bi, i if rows == tq else 0, first_blk // hps + hi)
    if rows == tq:
        return pl.BlockSpec((None, rows, width), index_map)
    return pl.BlockSpec((None, rows, width), index_map,
                        pipeline_mode=pl.Buffered(seq_buffers))


def _diff_attention_kernel(slopes_ref, lq1_ref, lk1_ref, lq2_ref, lk2_ref, subg_ref,
                           q_ref, k_ref, v_ref, gate_ref, o_ref,
                           vt_scr, ka_scr, acc_scr, sa_scr, sb_scr,
                           *, lam_init, tq, heads_per_step):
    i = pl.program_id(2)
    heads = range(heads_per_step)
    lane = lax.broadcasted_iota(jnp.int32, (tq, HEAD_DIM), 1)
    own_half = (lane < DIFF_HALF, lane >= DIFF_HALF)
    bias_lane0 = (DIFF_HALF, 0)

    @pl.when(i == 0)
    def _():
        for g in heads:
            _fill_values_transposed(v_ref, vt_scr, g, tq)
            for n in range(ka_scr.shape[2]):
                k = k_ref[n * tq:(n + 1) * tq, _head_lanes(g)].astype(_F32)
                for c in range(2):
                    ka_scr[g, c, n] = jnp.where(
                        own_half[c], k, _bias_key_lanes(lane, bias_lane0[c])).astype(_BF16)

    key_idx = lax.broadcasted_iota(jnp.int32, (tq, tq), 0)
    qry_idx = lax.broadcasted_iota(jnp.int32, (tq, tq), 1)
    qry_pos = lax.broadcasted_iota(jnp.int32, (1, tq), 1).astype(_F32)

    streams = []
    for g in heads:
        slope = slopes_ref[pl.program_id(1) * heads_per_step + g] * _LOG2E
        q = q_ref[:, _head_lanes(g)].astype(_F32)
        for c in range(2):
            qc = jnp.where(own_half[c], q, _bias_query_lanes(slope, lane, bias_lane0[c]))
            streams.append((g, c, slope, qc.astype(_BF16)))

    def scores(n, j, s_buf):
        g, c, _, qc = streams[n]
        s = _dot_nt(ka_scr[g, c, j], qc)
        s_buf[n] = s
        return jnp.max(s, axis=0, keepdims=True)

    def first_update(n, s_buf):
        g, c, slope, _ = streams[n]
        z = jnp.where(key_idx <= qry_idx, s_buf[n], _NEG_INF)
        return _first_flash_step(z, -slope * qry_pos, vt_scr[g, i], acc_scr.at[g, c])

    def update(n, j, s_buf, tile_max, running_max, valid=None):
        g, c, slope, _ = streams[n]
        rb = -slope * qry_pos - slope * ((i - j) * tq).astype(_F32)
        if valid is not None:
            rb = jnp.where(valid, rb, _NEG_INF)
        return _flash_step(s_buf.at[n], tile_max, rb, vt_scr[g, j], running_max,
                           acc_scr.at[g, c])

    _pipelined_flash(len(streams), i, scores, first_update, update, sa_scr, sb_scr)

    lam = (jnp.exp(jnp.sum(lq1_ref[...] * lk1_ref[...], axis=-1, keepdims=True))
           - jnp.exp(jnp.sum(lq2_ref[...] * lk2_ref[...], axis=-1, keepdims=True))
           + lam_init)
    for g in heads:
        o_t = _normalised(acc_scr[g, 0]) - lam * _normalised(acc_scr[g, 1])
        ms = jnp.mean(o_t * o_t, axis=0, keepdims=True)
        o = (o_t * lax.rsqrt(ms + EPS)).T * subg_ref[...] * (1.0 - lam_init)
        gate = gate_ref[:, _head_lanes(g)].astype(_F32)
        o_ref[:, _head_lanes(g)] = (o * _silu(gate)).astype(o_ref.dtype)


def diff_attention(proj, slopes, lq1, lk1, lq2, lk2, subg, *, lam_init, tq, heads_per_step):
    b, s, _ = proj.shape
    hps = heads_per_step
    assert s % tq == 0 and DIFF_HEADS % hps == 0
    width = hps * HEAD_DIM
    vec = lambda n: pl.BlockSpec((1, n), lambda bi, hi, i: (0, 0))
    spec = functools.partial(_head_group_spec, tq=tq, width=width, hps=hps, seq_buffers=2)
    scratch = [pltpu.VMEM((hps, s // tq, _VT_ROWS, tq), _BF16),
               pltpu.VMEM((hps, 2, s // tq, tq, HEAD_DIM), _BF16),
               pltpu.VMEM((hps, 2, _VT_ROWS, tq), _F32),
               pltpu.VMEM((2 * hps, tq, tq), _F32),
               pltpu.VMEM((2 * hps, tq, tq), _F32)]
    return pl.pallas_call(
        functools.partial(_diff_attention_kernel, lam_init=lam_init, tq=tq,
                          heads_per_step=hps),
        grid=(b, DIFF_HEADS // hps, s // tq),
        in_specs=[
            pl.BlockSpec(memory_space=pltpu.SMEM),
            vec(DIFF_HALF), vec(DIFF_HALF), vec(DIFF_HALF), vec(DIFF_HALF), vec(HEAD_DIM),
            spec(tq, first_blk=_DQ_BLK), spec(s, first_blk=_DK_BLK), spec(s, first_blk=_DV_BLK),
            spec(tq, first_blk=_GATE_BLK),
        ],
        out_specs=pl.BlockSpec((None, tq, width), lambda bi, hi, i: (bi, i, hi)),
        out_shape=jax.ShapeDtypeStruct((b, s, DIFF_WIDTH), _BF16),
        scratch_shapes=scratch,
        compiler_params=pltpu.CompilerParams(
            dimension_semantics=("parallel", "parallel", "arbitrary"),
            vmem_limit_bytes=_VMEM_LIMIT),
        name="diff_attention",
    )(slopes, lq1.reshape(1, -1), lk1.reshape(1, -1), lq2.reshape(1, -1), lk2.reshape(1, -1),
      subg.reshape(1, -1), proj, proj, proj, proj)


def _moba_attention_kernel(slopes_ref, q_ref, k_ref, v_ref, gate_ref, o_ref,
                           kmean_scr, kpieces_scr, vt_scr, sel_scr, acc_scr, sa_scr, sb_scr,
                           *, n_blocks, heads_per_step):
    i = pl.program_id(2)
    tq = MOBA_BLOCK
    heads = range(heads_per_step)

    lane = lax.broadcasted_iota(jnp.int32, (tq, HEAD_DIM), 1)

    @pl.when(i == 0)
    def _():
        for g in heads:
            _fill_values_transposed(v_ref, vt_scr, g, tq)
            for n in range(n_blocks):
                k = k_ref[n * MOBA_BLOCK:(n + 1) * MOBA_BLOCK, _head_lanes(g)]
                kmean_scr[g, n:n + 1, :] = (jnp.sum(k.astype(_F32), axis=0, keepdims=True)
                                            * (1.0 / MOBA_BLOCK))
            rest = kmean_scr[g]
            for t in range(_F32_AS_BF16_PIECES):
                piece = rest.astype(_BF16)
                kpieces_scr[g, t * n_blocks:(t + 1) * n_blocks, :] = piece
                rest = rest - piece.astype(_F32)

    key_idx = lax.broadcasted_iota(jnp.int32, (tq, tq), 0)
    qry_idx = lax.broadcasted_iota(jnp.int32, (tq, tq), 1)
    qry_pos = lax.broadcasted_iota(jnp.int32, (1, tq), 1).astype(_F32)
    blk = lax.broadcasted_iota(jnp.int32, (n_blocks, tq), 0)
    past = blk < i

    slopes = [slopes_ref[pl.program_id(1) * heads_per_step + g] * _LOG2E for g in heads]
    qs = [q_ref[:, _head_lanes(g)] for g in heads]
    qa_ts = [jnp.concatenate([qs[g].astype(_F32), _bias_query_lanes(slopes[g], lane, 0)],
                             axis=1).T.astype(_BF16) for g in heads]
    key_pos_lanes = _bias_key_lanes(lane, 0).astype(_BF16)

    def scores(g, j, s_buf):
        ka = jnp.concatenate([k_ref[pl.ds(j * tq, tq), _head_lanes(g)], key_pos_lanes], axis=1)
        s = _dot(ka, qa_ts[g])
        s_buf[g] = s
        return jnp.max(s, axis=0, keepdims=True)

    def select_blocks(g):
        parts = _dot_nt(kpieces_scr[g], qs[g])
        gate = sum(parts[t * n_blocks:(t + 1) * n_blocks] for t in range(_F32_AS_BF16_PIECES))
        gate = jnp.where(past, gate, _NEG_INF)
        for r0 in range(0, n_blocks, _SUBLANES):
            tile = gate[r0:r0 + _SUBLANES]
            row = lax.broadcasted_iota(jnp.int32, tile.shape, 0) + r0
            rank = jnp.zeros(tile.shape, _F32)
            for n in range(n_blocks):
                g_n = gate[n:n + 1, :]
                if n < r0:
                    beats = g_n >= tile
                elif n >= r0 + _SUBLANES:
                    beats = g_n > tile
                else:
                    beats = (g_n > tile) | ((g_n == tile) & (n < row))
                rank = rank + jnp.where(beats, 1.0, 0.0)
            sel_scr[g, r0:r0 + _SUBLANES, :] = jnp.where(
                (row < i) & (rank < MOBA_TOPK), 0.0, _NEG_INF)

    def first_update(g, s_buf):
        select_blocks(g)
        z = jnp.where(key_idx <= qry_idx, s_buf[g], _NEG_INF)
        return _first_flash_step(z, -slopes[g] * qry_pos, vt_scr[g, i], acc_scr.at[g])

    def update(g, j, s_buf, tile_max, running_max, valid=None):
        rb = (-slopes[g] * qry_pos - slopes[g] * ((i - j) * tq).astype(_F32)
              + sel_scr[g, pl.ds(j, 1), :])
        return _flash_step(s_buf.at[g], tile_max, rb, vt_scr[g, j], running_max, acc_scr.at[g])

    _pipelined_flash(heads_per_step, i, scores, first_update, update, sa_scr, sb_scr)
    for g in heads:
        o = _normalised(acc_scr[g]).T
        gate = gate_ref[:, _head_lanes(g)].astype(_F32)
        o_ref[:, _head_lanes(g)] = (o * _silu(gate)).astype(o_ref.dtype)


def moba_attention(proj, slopes, *, heads_per_step):
    b, s, _ = proj.shape
    hps = heads_per_step
    assert s % MOBA_BLOCK == 0 and MOBA_HEADS % hps == 0
    n_blocks = s // MOBA_BLOCK
    assert n_blocks % _SUBLANES == 0
    tq = MOBA_BLOCK
    width = hps * HEAD_DIM
    spec = functools.partial(_head_group_spec, tq=tq, width=width, hps=hps, seq_buffers=1)
    return pl.pallas_call(
        functools.partial(_moba_attention_kernel, n_blocks=n_blocks, heads_per_step=hps),
        grid=(b, MOBA_HEADS // hps, n_blocks),
        in_specs=[
            pl.BlockSpec(memory_space=pltpu.SMEM),
            spec(tq, first_blk=_MQ_BLK), spec(s, first_blk=_MK_BLK), spec(s, first_blk=_MV_BLK),
            spec(tq, first_blk=_GATE_BLK + DIFF_HEADS),
        ],
        out_specs=pl.BlockSpec((None, tq, width), lambda bi, hi, i: (bi, i, hi)),
        out_shape=jax.ShapeDtypeStruct((b, s, MOBA_WIDTH), _BF16),
        scratch_shapes=[pltpu.VMEM((hps, n_blocks, HEAD_DIM), _F32),
                        pltpu.VMEM((hps, _F32_AS_BF16_PIECES * n_blocks, HEAD_DIM), _BF16),
                        pltpu.VMEM((hps, n_blocks, _VT_ROWS, tq), _BF16),
                        pltpu.VMEM((hps, n_blocks, tq), _F32),
                        pltpu.VMEM((hps, _VT_ROWS, tq), _F32),
                        pltpu.VMEM((hps, tq, tq), _F32),
                        pltpu.VMEM((hps, tq, tq), _F32)],
        compiler_params=pltpu.CompilerParams(
            dimension_semantics=("parallel", "parallel", "arbitrary"),
            vmem_limit_bytes=_VMEM_LIMIT),
        name="moba_attention",
    )(slopes, proj, proj, proj, proj)


def _layer_tail_kernel(yd_ref, ym_ref, x_ref, wa_ref, wb_ref, g_ref, wq_ref, kv_ref, wo_ref,
                       fg_ref, o_ref, q_scr, o_scr, *, apply_final, sub_rows):
    subs = [slice(r, r + sub_rows) for r in range(0, x_ref.shape[0], sub_rows)]
    xs = [x_ref[r, :] + _dot(yd_ref[r, :], wa_ref[...]) + _dot(ym_ref[r, :], wb_ref[...])
          for r in subs]
    for r, x in zip(subs, xs):
        ms = jnp.mean(x * x, axis=-1, keepdims=True)
        h = (x * lax.rsqrt(ms + EPS) * g_ref[...]).astype(_BF16)
        q_scr[r, :] = (_dot(h, wq_ref[...]) * (HEAD_DIM ** -0.5 * _LOG2E)).astype(_BF16)
    for r in subs:
        for hh in range(MEM_HEADS):
            lo, hi = hh * HEAD_DIM, (hh + 1) * HEAD_DIM
            s = _dot_nt(q_scr[r, lo:hi], kv_ref[:, lo:hi])
            p = jnp.exp2(s - jnp.max(s, axis=-1, keepdims=True))
            l = jnp.sum(p, axis=-1, keepdims=True)
            o_h = _dot(p.astype(_BF16), kv_ref[:, MEM_WIDTH + lo:MEM_WIDTH + hi]) / l
            o_scr[r, lo:hi] = o_h.astype(_BF16)
    for r, x in zip(subs, xs):
        x = x + _dot(o_scr[r, :], wo_ref[...])
        if apply_final:
            ms = jnp.mean(x * x, axis=-1, keepdims=True)
            x = x * lax.rsqrt(ms + EPS) * fg_ref[...]
        o_ref[r, :] = x


def layer_tail(yd, ym, x, w_out, g, wq, kv, wo, final_g, *, apply_final, tm, sub_rows):
    b, s, d = x.shape
    mlen = kv.shape[1]
    ka, kb = yd.shape[-1], ym.shape[-1]
    assert s % tm == 0 and ka == kb and w_out.shape[0] == ka + kb
    rows = lambda width: pl.BlockSpec((None, tm, width), lambda bi, i: (bi, i, 0))
    whole = lambda shape: pl.BlockSpec(shape, lambda bi, i: (0,) * len(shape))
    return pl.pallas_call(
        functools.partial(_layer_tail_kernel, apply_final=apply_final, sub_rows=sub_rows),
        grid=(b, s // tm),
        in_specs=[
            rows(ka), rows(kb), rows(d),
            pl.BlockSpec((ka, d), lambda bi, i: (0, 0)),
            pl.BlockSpec((kb, d), lambda bi, i: (1, 0)),
            whole((1, d)), whole((d, MEM_WIDTH)),
            pl.BlockSpec((None, mlen, 2 * MEM_WIDTH), lambda bi, i: (bi, 0, 0)),
            whole((MEM_WIDTH, d)), whole((1, d)),
        ],
        out_specs=rows(d),
        out_shape=jax.ShapeDtypeStruct((b, s, d), _F32),
        scratch_shapes=[pltpu.VMEM((tm, MEM_WIDTH), _BF16),
                        pltpu.VMEM((tm, MEM_WIDTH), _BF16)],
        compiler_params=pltpu.CompilerParams(
            dimension_semantics=("parallel", "parallel"),
            vmem_limit_bytes=_VMEM_LIMIT),
        name="layer_tail",
    )(yd, ym, x, w_out, w_out, g.reshape(1, d), wq, kv, wo, final_g.reshape(1, d))


def _in_proj_col_scale(n_cols):
    cs = np.ones((n_cols,), np.float32)
    cs[_DQ_BLK * HEAD_DIM:_DK_BLK * HEAD_DIM] = DIFF_HALF ** -0.5 * _LOG2E
    cs[_MQ_BLK * HEAD_DIM:_MK_BLK * HEAD_DIM] = HEAD_DIM ** -0.5 * _LOG2E
    return jnp.asarray(cs)


def _alibi_slopes(n):
    return jnp.asarray(2.0 ** (-8.0 * np.arange(1, n + 1) / n), dtype=_F32)


_IN_PROJ_TILE = (1024, 1024)
_ATTN_Q_TILE = 256
_DIFF_HEADS_PER_STEP = 4
_MOBA_HEADS_PER_STEP = 8
_TAIL_ROW_TILE = 512
_TAIL_SUB_ROWS = 256


def kernel(x, mem, norm_mix_g, w_in, lambda_q1, lambda_k1, lambda_q2, lambda_k2, subln_g,
           w_out, norm_mem_g, mem_norm_g, w_q_mem, w_kv_mem, w_o_mem, final_g):
    b, s, d = x.shape
    mlen = mem.shape[1]
    depth = w_in.shape[0]
    diff_slopes = _alibi_slopes(DIFF_HEADS)
    moba_slopes = _alibi_slopes(MOBA_HEADS)
    mem2 = mem.reshape(b * mlen, d)
    for l in range(depth):
        lam_init = 0.8 - 0.6 * math.exp(-0.3 * l)
        proj = rms_matmul(x.reshape(b * s, d), norm_mix_g[l], w_in[l],
                          _in_proj_col_scale(w_in.shape[-1]),
                          tm=_IN_PROJ_TILE[0], tn=_IN_PROJ_TILE[1]).reshape(b, s, -1)
        y_d = diff_attention(proj, diff_slopes, lambda_q1[l], lambda_k1[l], lambda_q2[l],
                             lambda_k2[l], subln_g[l], lam_init=lam_init, tq=_ATTN_Q_TILE,
                             heads_per_step=_DIFF_HEADS_PER_STEP)
        y_m = moba_attention(proj, moba_slopes, heads_per_step=_MOBA_HEADS_PER_STEP)
        kv = rms_matmul(mem2, mem_norm_g[l], w_kv_mem[l],
                        tm=b * mlen, tn=2 * MEM_WIDTH).reshape(b, mlen, 2 * MEM_WIDTH)
        x = layer_tail(y_d, y_m, x, w_out[l].astype(_BF16), norm_mem_g[l],
                       w_q_mem[l].astype(_BF16), kv, w_o_mem[l].astype(_BF16), final_g,
                       apply_final=(l == depth - 1), tm=_TAIL_ROW_TILE,
                       sub_rows=_TAIL_SUB_ROWS)
    return x
```

```python
import functools
import math

import jax
import jax.numpy as jnp
import numpy as np
from jax import lax
from jax.experimental import pallas as pl
from jax.experimental.pallas import tpu as pltpu

HEAD_DIM = 128
DIFF_HEADS = 8
DIFF_HALF = HEAD_DIM // 2
DIFF_WIDTH = DIFF_HEADS * HEAD_DIM
MOBA_HEADS = 8
MOBA_WIDTH = MOBA_HEADS * HEAD_DIM
MOBA_BLOCK = 256
MOBA_TOPK = 3
MEM_HEADS = 4
MEM_WIDTH = MEM_HEADS * HEAD_DIM
EPS = 1e-6

_DQ_BLK = 0
_DK_BLK = DIFF_HEADS
_DV_BLK = 2 * DIFF_HEADS
_MQ_BLK = 3 * DIFF_HEADS
_MK_BLK = 3 * DIFF_HEADS + MOBA_HEADS
_MV_BLK = 3 * DIFF_HEADS + 2 * MOBA_HEADS
_GATE_BLK = 3 * DIFF_HEADS + 3 * MOBA_HEADS

_V7X_VMEM_BYTES = 64 * 1024 * 1024
_VMEM_LIMIT = _V7X_VMEM_BYTES * 3 // 4

_BF16 = jnp.bfloat16
_F32 = jnp.float32
_NEG_INF = float("-inf")


def _dot_nt(a, b):
    return lax.dot_general(a, b, (((1,), (1,)), ((), ())), preferred_element_type=_F32)


def _dot(a, b):
    return jnp.dot(a, b, preferred_element_type=_F32)


def _silu(g):
    return g * jax.nn.sigmoid(g)


_NORM_CHUNK_ROWS = 256

def _rms_matmul_kernel(x_ref, g_ref, w_ref, *rest, has_col_scale, norm_rows):
    cs_ref, o_ref, h_scr = rest if has_col_scale else (None,) + rest

    def project(rows):
        out = _dot(h_scr[rows, :], w_ref[...].astype(_BF16))
        if has_col_scale:
            out = out * cs_ref[...]
        o_ref[rows, :] = out.astype(o_ref.dtype)

    @pl.when(pl.program_id(1) == 0)
    def _():
        for r0 in range(0, x_ref.shape[0], norm_rows):
            rows = slice(r0, r0 + norm_rows)
            x = x_ref[rows, :]
            ms = jnp.mean(x * x, axis=-1, keepdims=True)
            h_scr[rows, :] = (x * lax.rsqrt(ms + EPS) * g_ref[...]).astype(_BF16)
            project(rows)

    @pl.when(pl.program_id(1) != 0)
    def _():
        project(slice(None))


def rms_matmul(x, g, w, col_scale=None, *, tm, tn, out_dtype=_BF16):
    m, k = x.shape
    n = w.shape[1]
    assert m % tm == 0 and n % tn == 0
    has_cs = col_scale is not None
    cs_specs = [pl.BlockSpec((1, tn), lambda i, j: (0, j))] if has_cs else []
    cs_args = [col_scale.reshape(1, n)] if has_cs else []
    norm_rows = min(tm, _NORM_CHUNK_ROWS)
    assert tm % norm_rows == 0
    return pl.pallas_call(
        functools.partial(_rms_matmul_kernel, has_col_scale=has_cs, norm_rows=norm_rows),
        grid=(m // tm, n // tn),
        in_specs=[
            pl.BlockSpec((tm, k), lambda i, j: (i, 0)),
            pl.BlockSpec((1, k), lambda i, j: (0, 0)),
            pl.BlockSpec((k, tn), lambda i, j: (0, j)),
        ] + cs_specs,
        out_specs=pl.BlockSpec((tm, tn), lambda i, j: (i, j)),
        out_shape=jax.ShapeDtypeStruct((m, n), out_dtype),
        scratch_shapes=[pltpu.VMEM((tm, k), _BF16)],
        compiler_params=pltpu.CompilerParams(
            dimension_semantics=("parallel", "arbitrary"),
            vmem_limit_bytes=_VMEM_LIMIT),
        name="rms_matmul",
    )(x, g.reshape(1, k), w, *cs_args)


_LOG2E = math.log2(math.e)
_ONES_ROWS = 16
_VT_ROWS = HEAD_DIM + _ONES_ROWS


def _first_flash_step(z, rb, vt, acc_ref):
    m = jnp.max(z, axis=0, keepdims=True) + rb
    p = jnp.exp2(z - (m - rb))
    acc_ref[...] = _dot(vt, p.astype(_BF16))
    return m


def _flash_step(z_ref, z_max, rb, vt, m, acc_ref):
    m_new = jnp.maximum(m, z_max + rb)
    p = jnp.exp2(z_ref[...] - (m_new - rb))
    acc_ref[...] = jnp.exp2(m - m_new) * acc_ref[...] + _dot(vt, p.astype(_BF16))
    return m_new


_F32_AS_BF16_PIECES = 3
_BIAS_LANES = _F32_AS_BF16_PIECES
_SUBLANES = 8


def _bias_key_lanes(lane, lane0):
    pos = lax.broadcasted_iota(jnp.int32, lane.shape, 0).astype(_F32)
    return jnp.where((lane >= lane0) & (lane < lane0 + _BIAS_LANES), pos, 0.0)


def _bias_query_lanes(slope, lane, lane0):
    rest = jnp.full(lane.shape, slope, _F32)
    out = jnp.zeros(lane.shape, _F32)
    for t in range(_BIAS_LANES):
        piece = rest.astype(_BF16).astype(_F32)
        out = jnp.where(lane == lane0 + t, piece, out)
        rest = rest - piece
    return out


def _fill_values_transposed(v_ref, vt_scr, g, tk):
    for n in range(vt_scr.shape[1]):
        vt_scr[g, n, :HEAD_DIM, :] = v_ref[n * tk:(n + 1) * tk, _head_lanes(g)].T
        vt_scr[g, n, HEAD_DIM:, :] = jnp.ones((_ONES_ROWS, tk), _BF16)


def _normalised(acc):
    return acc[:HEAD_DIM] / acc[HEAD_DIM:HEAD_DIM + 1]


def _pipelined_flash(n_streams, n_past, scores, first_update, update, sa_scr, sb_scr):
    streams = range(n_streams)
    for n in streams:
        scores(n, n_past, sb_scr)
    max_a = tuple(scores(n, 0, sa_scr) for n in streams)
    running = tuple(first_update(n, sb_scr) for n in streams)

    def pair(t, carry):
        running, max_a = (list(c) for c in carry)
        j0 = 2 * t
        j1 = j0 + 1
        j2 = jnp.minimum(j1 + 1, n_past)
        max_b = [None] * n_streams
        for n in streams:
            max_b[n] = scores(n, j1, sb_scr)
            running[n] = update(n, j0, sa_scr, max_a[n], running[n])
        for n in streams:
            max_a[n] = scores(n, j2, sa_scr)
            running[n] = update(n, j1, sb_scr, max_b[n], running[n], valid=j1 < n_past)
        return tuple(running), tuple(max_a)

    def two_pairs(t, carry):
        return pair(2 * t + 1, pair(2 * t, carry))

    n_pairs = (n_past + 1) // 2
    carry = lax.fori_loop(0, n_pairs // 2, two_pairs, (running, max_a))
    lax.fori_loop(n_pairs - n_pairs % 2, n_pairs, pair, carry)


def _head_lanes(g):
    return slice(g * HEAD_DIM, (g + 1) * HEAD_DIM)


def _head_group_spec(rows, tq, width, first_blk, hps, seq_buffers):
    assert first_blk % hps == 0
    index_map = lambda bi, hi, i: (bi, i if rows == tq else 0, first_blk // hps + hi)
    if rows == tq:
        return pl.BlockSpec((None, rows, width), index_map)
    return pl.BlockSpec((None, rows, width), index_map,
                        pipeline_mode=pl.Buffered(seq_buffers))


def _diff_attention_kernel(slopes_ref, lq1_ref, lk1_ref, lq2_ref, lk2_ref, subg_ref,
                           q_ref, k_ref, v_ref, gate_ref, o_ref,
                           vt_scr, ka_scr, acc_scr, sa_scr, sb_scr,
                           *, lam_init, tq, heads_per_step):
    i = pl.program_id(2)
    heads = range(heads_per_step)
    lane = lax.broadcasted_iota(jnp.int32, (tq, HEAD_DIM), 1)
    own_half = (lane < DIFF_HALF, lane >= DIFF_HALF)
    bias_lane0 = (DIFF_HALF, 0)

    @pl.when(i == 0)
    def _():
        for g in heads:
            _fill_values_transposed(v_ref, vt_scr, g, tq)
            for n in range(ka_scr.shape[2]):
                k = k_ref[n * tq:(n + 1) * tq, _head_lanes(g)].astype(_F32)
                for c in range(2):
                    ka_scr[g, c, n] = jnp.where(
                        own_half[c], k, _bias_key_lanes(lane, bias_lane0[c])).astype(_BF16)

    key_idx = lax.broadcasted_iota(jnp.int32, (tq, tq), 0)
    qry_idx = lax.broadcasted_iota(jnp.int32, (tq, tq), 1)
    qry_pos = lax.broadcasted_iota(jnp.int32, (1, tq), 1).astype(_F32)

    streams = []
    for g in heads:
        slope = slopes_ref[pl.program_id(1) * heads_per_step + g] * _LOG2E
        q = q_ref[:, _head_lanes(g)].astype(_F32)
        for c in range(2):
            qc = jnp.where(own_half[c], q, _bias_query_lanes(slope, lane, bias_lane0[c]))
            streams.append((g, c, slope, qc.astype(_BF16)))

    def scores(n, j, s_buf):
        g, c, _, qc = streams[n]
        s = _dot_nt(ka_scr[g, c, j], qc)
        s_buf[n] = s
        return jnp.max(s, axis=0, keepdims=True)

    def first_update(n, s_buf):
        g, c, slope, _ = streams[n]
        z = jnp.where(key_idx <= qry_idx, s_buf[n], _NEG_INF)
        return _first_flash_step(z, -slope * qry_pos, vt_scr[g, i], acc_scr.at[g, c])

    def update(n, j, s_buf, tile_max, running_max, valid=None):
        g, c, slope, _ = streams[n]
        rb = -slope * qry_pos - slope * ((i - j) * tq).astype(_F32)
        if valid is not None:
            rb = jnp.where(valid, rb, _NEG_INF)
        return _flash_step(s_buf.at[n], tile_max, rb, vt_scr[g, j], running_max,
                           acc_scr.at[g, c])

    _pipelined_flash(len(streams), i, scores, first_update, update, sa_scr, sb_scr)

    lam = (jnp.exp(jnp.sum(lq1_ref[...] * lk1_ref[...], axis=-1, keepdims=True))
           - jnp.exp(jnp.sum(lq2_ref[...] * lk2_ref[...], axis=-1, keepdims=True))
           + lam_init)
    for g in heads:
        o_t = _normalised(acc_scr[g, 0]) - lam * _normalised(acc_scr[g, 1])
        ms = jnp.mean(o_t * o_t, axis=0, keepdims=True)
        o = (o_t * lax.rsqrt(ms + EPS)).T * subg_ref[...] * (1.0 - lam_init)
        gate = gate_ref[:, _head_lanes(g)].astype(_F32)
        o_ref[:, _head_lanes(g)] = (o * _silu(gate)).astype(o_ref.dtype)


def diff_attention(proj, slopes, lq1, lk1, lq2, lk2, subg, *, lam_init, tq, heads_per_step):
    b, s, _ = proj.shape
    hps = heads_per_step
    assert s % tq == 0 and DIFF_HEADS % hps == 0
    width = hps * HEAD_DIM
    vec = lambda n: pl.BlockSpec((1, n), lambda bi, hi, i: (0, 0))
    spec = functools.partial(_head_group_spec, tq=tq, width=width, hps=hps, seq_buffers=2)
    scratch = [pltpu.VMEM((hps, s // tq, _VT_ROWS, tq), _BF16),
               pltpu.VMEM((hps, 2, s // tq, tq, HEAD_DIM), _BF16),
               pltpu.VMEM((hps, 2, _VT_ROWS, tq), _F32),
               pltpu.VMEM((2 * hps, tq, tq), _F32),
               pltpu.VMEM((2 * hps, tq, tq), _F32)]
    return pl.pallas_call(
        functools.partial(_diff_attention_kernel, lam_init=lam_init, tq=tq,
                          heads_per_step=hps),
        grid=(b, DIFF_HEADS // hps, s // tq),
        in_specs=[
            pl.BlockSpec(memory_space=pltpu.SMEM),
            vec(DIFF_HALF), vec(DIFF_HALF), vec(DIFF_HALF), vec(DIFF_HALF), vec(HEAD_DIM),
            spec(tq, first_blk=_DQ_BLK), spec(s, first_blk=_DK_BLK), spec(s, first_blk=_DV_BLK),
            spec(tq, first_blk=_GATE_BLK),
        ],
        out_specs=pl.BlockSpec((None, tq, width), lambda bi, hi, i: (bi, i, hi)),
        out_shape=jax.ShapeDtypeStruct((b, s, DIFF_WIDTH), _BF16),
        scratch_shapes=scratch,
        compiler_params=pltpu.CompilerParams(
            dimension_semantics=("parallel", "parallel", "arbitrary"),
            vmem_limit_bytes=_VMEM_LIMIT),
        name="diff_attention",
    )(slopes, lq1.reshape(1, -1), lk1.reshape(1, -1), lq2.reshape(1, -1), lk2.reshape(1, -1),
      subg.reshape(1, -1), proj, proj, proj, proj)


def _moba_attention_kernel(slopes_ref, q_ref, k_ref, v_ref, gate_ref, o_ref,
                           kmean_scr, kpieces_scr, vt_scr, sel_scr, acc_scr, sa_scr, sb_scr,
                           *, n_blocks, heads_per_step):
    i = pl.program_id(2)
    tq = MOBA_BLOCK
    heads = range(heads_per_step)

    lane = lax.broadcasted_iota(jnp.int32, (tq, HEAD_DIM), 1)

    @pl.when(i == 0)
    def _():
        for g in heads:
            _fill_values_transposed(v_ref, vt_scr, g, tq)
            for n in range(n_blocks):
                k = k_ref[n * MOBA_BLOCK:(n + 1) * MOBA_BLOCK, _head_lanes(g)]
                kmean_scr[g, n:n + 1, :] = (jnp.sum(k.astype(_F32), axis=0, keepdims=True)
                                            * (1.0 / MOBA_BLOCK))
            rest = kmean_scr[g]
            for t in range(_F32_AS_BF16_PIECES):
                piece = rest.astype(_BF16)
                kpieces_scr[g, t * n_blocks:(t + 1) * n_blocks, :] = piece
                rest = rest - piece.astype(_F32)

    key_idx = lax.broadcasted_iota(jnp.int32, (tq, tq), 0)
    qry_idx = lax.broadcasted_iota(jnp.int32, (tq, tq), 1)
    qry_pos = lax.broadcasted_iota(jnp.int32, (1, tq), 1).astype(_F32)
    blk = lax.broadcasted_iota(jnp.int32, (n_blocks, tq), 0)
    past = blk < i

    slopes = [slopes_ref[pl.program_id(1) * heads_per_step + g] * _LOG2E for g in heads]
    qs = [q_ref[:, _head_lanes(g)] for g in heads]
    qa_ts = [jnp.concatenate([qs[g].astype(_F32), _bias_query_lanes(slopes[g], lane, 0)],
                             axis=1).T.astype(_BF16) for g in heads]
    key_pos_lanes = _bias_key_lanes(lane, 0).astype(_BF16)

    def scores(g, j, s_buf):
        ka = jnp.concatenate([k_ref[pl.ds(j * tq, tq), _head_lanes(g)], key_pos_lanes], axis=1)
        s = _dot(ka, qa_ts[g])
        s_buf[g] = s
        return jnp.max(s, axis=0, keepdims=True)

    def select_blocks(g):
        parts = _dot_nt(kpieces_scr[g], qs[g])
        gate = sum(parts[t * n_blocks:(t + 1) * n_blocks] for t in range(_F32_AS_BF16_PIECES))
        gate = jnp.where(past, gate, _NEG_INF)
        for r0 in range(0, n_blocks, _SUBLANES):
            tile = gate[r0:r0 + _SUBLANES]
            row = lax.broadcasted_iota(jnp.int32, tile.shape, 0) + r0
            rank = jnp.zeros(tile.shape, _F32)
            for n in range(n_blocks):
                g_n = gate[n:n + 1, :]
                if n < r0:
                    beats = g_n >= tile
                elif n >= r0 + _SUBLANES:
                    beats = g_n > tile
                else:
                    beats = (g_n > tile) | ((g_n == tile) & (n < row))
                rank = rank + jnp.where(beats, 1.0, 0.0)
            sel_scr[g, r0:r0 + _SUBLANES, :] = jnp.where(
                (row < i) & (rank < MOBA_TOPK), 0.0, _NEG_INF)

    def first_update(g, s_buf):
        select_blocks(g)
        z = jnp.where(key_idx <= qry_idx, s_buf[g], _NEG_INF)
        return _first_flash_step(z, -slopes[g] * qry_pos, vt_scr[g, i], acc_scr.at[g])

    def update(g, j, s_buf, tile_max, running_max, valid=None):
        rb = (-slopes[g] * qry_pos - slopes[g] * ((i - j) * tq).astype(_F32)
              + sel_scr[g, pl.ds(j, 1), :])
        return _flash_step(s_buf.at[g], tile_max, rb, vt_scr[g, j], running_max, acc_scr.at[g])

    _pipelined_flash(heads_per_step, i, scores, first_update, update, sa_scr, sb_scr)
    for g in heads:
        o = _normalised(acc_scr[g]).T
        gate = gate_ref[:, _head_lanes(g)].astype(_F32)
        o_ref[:, _head_lanes(g)] = (o * _silu(gate)).astype(o_ref.dtype)


def moba_attention(proj, slopes, *, heads_per_step):
    b, s, _ = proj.shape
    hps = heads_per_step
    assert s % MOBA_BLOCK == 0 and MOBA_HEADS % hps == 0
    n_blocks = s // MOBA_BLOCK
    assert n_blocks % _SUBLANES == 0
    tq = MOBA_BLOCK
    width = hps * HEAD_DIM
    spec = functools.partial(_head_group_spec, tq=tq, width=width, hps=hps, seq_buffers=1)
    return pl.pallas_call(
        functools.partial(_moba_attention_kernel, n_blocks=n_blocks, heads_per_step=hps),
        grid=(b, MOBA_HEADS // hps, n_blocks),
        in_specs=[
            pl.BlockSpec(memory_space=pltpu.SMEM),
            spec(tq, first_blk=_MQ_BLK), spec(s, first_blk=_MK_BLK), spec(s, first_blk=_MV_BLK),
            spec(tq, first_blk=_GATE_BLK + DIFF_HEADS),
        ],
        out_specs=pl.BlockSpec((None, tq, width), lambda bi, hi, i: (bi, i, hi)),
        out_shape=jax.ShapeDtypeStruct((b, s, MOBA_WIDTH), _BF16),
        scratch_shapes=[pltpu.VMEM((hps, n_blocks, HEAD_DIM), _F32),
                        pltpu.VMEM((hps, _F32_AS_BF16_PIECES * n_blocks, HEAD_DIM), _BF16),
                        pltpu.VMEM((hps, n_blocks, _VT_ROWS, tq), _BF16),
                        pltpu.VMEM((hps, n_blocks, tq), _F32),
                        pltpu.VMEM((hps, _VT_ROWS, tq), _F32),
                        pltpu.VMEM((hps, tq, tq), _F32),
                        pltpu.VMEM((hps, tq, tq), _F32)],
        compiler_params=pltpu.CompilerParams(
            dimension_semantics=("parallel", "parallel", "arbitrary"),
            vmem_limit_bytes=_VMEM_LIMIT),
        name="moba_attention",
    )(slopes, proj, proj, proj, proj)


def _layer_tail_kernel(yd_ref, ym_ref, x_ref, wa_ref, wb_ref, g_ref, wq_ref, kv_ref, wo_ref,
                       fg_ref, o_ref, q_scr, o_scr, *, apply_final, sub_rows):
    subs = [slice(r, r + sub_rows) for r in range(0, x_ref.shape[0], sub_rows)]
    xs = [x_ref[r, :] + _dot(yd_ref[r, :], wa_ref[...]) + _dot(ym_ref[r, :], wb_ref[...])
          for r in subs]
    for r, x in zip(subs, xs):
        ms = jnp.mean(x * x, axis=-1, keepdims=True)
        h = (x * lax.rsqrt(ms + EPS) * g_ref[...]).astype(_BF16)
        q_scr[r, :] = (_dot(h, wq_ref[...]) * (HEAD_DIM ** -0.5 * _LOG2E)).astype(_BF16)
    for r in subs:
        for hh in range(MEM_HEADS):
            lo, hi = hh * HEAD_DIM, (hh + 1) * HEAD_DIM
            s = _dot_nt(q_scr[r, lo:hi], kv_ref[:, lo:hi])
            p = jnp.exp2(s - jnp.max(s, axis=-1, keepdims=True))
            l = jnp.sum(p, axis=-1, keepdims=True)
            o_h = _dot(p.astype(_BF16), kv_ref[:, MEM_WIDTH + lo:MEM_WIDTH + hi]) / l
            o_scr[r, lo:hi] = o_h.astype(_BF16)
    for r, x in zip(subs, xs):
        x = x + _dot(o_scr[r, :], wo_ref[...])
        if apply_final:
            ms = jnp.mean(x * x, axis=-1, keepdims=True)
            x = x * lax.rsqrt(ms + EPS) * fg_ref[...]
        o_ref[r, :] = x


def layer_tail(yd, ym, x, w_out, g, wq, kv, wo, final_g, *, apply_final, tm, sub_rows):
    b, s, d = x.shape
    mlen = kv.shape[1]
    ka, kb = yd.shape[-1], ym.shape[-1]
    assert s % tm == 0 and ka == kb and w_out.shape[0] == ka + kb
    rows = lambda width: pl.BlockSpec((None, tm, width), lambda bi, i: (bi, i, 0))
    whole = lambda shape: pl.BlockSpec(shape, lambda bi, i: (0,) * len(shape))
    return pl.pallas_call(
        functools.partial(_layer_tail_kernel, apply_final=apply_final, sub_rows=sub_rows),
        grid=(b, s // tm),
        in_specs=[
            rows(ka), rows(kb), rows(d),
            pl.BlockSpec((ka, d), lambda bi, i: (0, 0)),
            pl.BlockSpec((kb, d), lambda bi, i: (1, 0)),
            whole((1, d)), whole((d, MEM_WIDTH)),
            pl.BlockSpec((None, mlen, 2 * MEM_WIDTH), lambda bi, i: (bi, 0, 0)),
            whole((MEM_WIDTH, d)), whole((1, d)),
        ],
        out_specs=rows(d),
        out_shape=jax.ShapeDtypeStruct((b, s, d), _F32),
        scratch_shapes=[pltpu.VMEM((tm, MEM_WIDTH), _BF16),
                        pltpu.VMEM((tm, MEM_WIDTH), _BF16)],
        compiler_params=pltpu.CompilerParams(
            dimension_semantics=("parallel", "parallel"),
            vmem_limit_bytes=_VMEM_LIMIT),
        name="layer_tail",
    )(yd, ym, x, w_out, w_out, g.reshape(1, d), wq, kv, wo, final_g.reshape(1, d))


def _in_proj_col_scale(n_cols):
    cs = np.ones((n_cols,), np.float32)
    cs[_DQ_BLK * HEAD_DIM:_DK_BLK * HEAD_DIM] = DIFF_HALF ** -0.5 * _LOG2E
    cs[_MQ_BLK * HEAD_DIM:_MK_BLK * HEAD_DIM] = HEAD_DIM ** -0.5 * _LOG2E
    return jnp.asarray(cs)


def _alibi_slopes(n):
    return jnp.asarray(2.0 ** (-8.0 * np.arange(1, n + 1) / n), dtype=_F32)


_IN_PROJ_TILE = (1024, 1024)
_ATTN_Q_TILE = 256
_DIFF_HEADS_PER_STEP = 4
_MOBA_HEADS_PER_STEP = 8
_TAIL_ROW_TILE = 512
_TAIL_SUB_ROWS = 256


def kernel(x, mem, norm_mix_g, w_in, lambda_q1, lambda_k1, lambda_q2, lambda_k2, subln_g,
           w_out, norm_mem_g, mem_norm_g, w_q_mem, w_kv_mem, w_o_mem, final_g):
    b, s, d = x.shape
    mlen = mem.shape[1]
    depth = w_in.shape[0]
    diff_slopes = _alibi_slopes(DIFF_HEADS)
    moba_slopes = _alibi_slopes(MOBA_HEADS)
    mem2 = mem.reshape(b * mlen, d)
    for l in range(depth):
        lam_init = 0.8 - 0.6 * math.exp(-0.3 * l)
        proj = rms_matmul(x.reshape(b * s, d), norm_mix_g[l], w_in[l],
                          _in_proj_col_scale(w_in.shape[-1]),
                          tm=_IN_PROJ_TILE[0], tn=_IN_PROJ_TILE[1]).reshape(b, s, -1)
        y_d = diff_attention(proj, diff_slopes, lambda_q1[l], lambda_k1[l], lambda_q2[l],
                             lambda_k2[l], subln_g[l], lam_init=lam_init, tq=_ATTN_Q_TILE,
                             heads_per_step=_DIFF_HEADS_PER_STEP)
        y_m = moba_attention(proj, moba_slopes, heads_per_step=_MOBA_HEADS_PER_STEP)
        kv = rms_matmul(mem2, mem_norm_g[l], w_kv_mem[l],
                        tm=b * mlen, tn=2 * MEM_WIDTH).reshape(b, mlen, 2 * MEM_WIDTH)
        x = layer_tail(y_d, y_m, x, w_out[l].astype(_BF16), norm_mem_g[l],
                       w_q_mem[l].astype(_BF16), kv, w_o_mem[l].astype(_BF16), final_g,
                       apply_final=(l == depth - 1), tm=_TAIL_ROW_TILE,
                       sub_rows=_TAIL_SUB_ROWS)
    return x
```

```python
import functools
import math

import jax
import jax.numpy as jnp
import numpy as np
from jax import lax
from jax.experimental import pallas as pl
from jax.experimental.pallas import tpu as pltpu

HEAD_DIM = 128
DIFF_HEADS = 8
DIFF_HALF = HEAD_DIM // 2
DIFF_WIDTH = DIFF_HEADS * HEAD_DIM
MOBA_HEADS = 8
MOBA_WIDTH = MOBA_HEADS * HEAD_DIM
MOBA_BLOCK = 256
MOBA_TOPK = 3
MEM_HEADS = 4
MEM_WIDTH = MEM_HEADS * HEAD_DIM
EPS = 1e-6

_DQ_BLK = 0
_DK_BLK = DIFF_HEADS
_DV_BLK = 2 * DIFF_HEADS
_MQ_BLK = 3 * DIFF_HEADS
_MK_BLK = 3 * DIFF_HEADS + MOBA_HEADS
_MV_BLK = 3 * DIFF_HEADS + 2 * MOBA_HEADS
_GATE_BLK = 3 * DIFF_HEADS + 3 * MOBA_HEADS

_V7X_VMEM_BYTES = 64 * 1024 * 1024
_VMEM_LIMIT = _V7X_VMEM_BYTES * 3 // 4

_BF16 = jnp.bfloat16
_F32 = jnp.float32
_NEG_INF = float("-inf")


def _dot_nt(a, b):
    return lax.dot_general(a, b, (((1,), (1,)), ((), ())), preferred_element_type=_F32)


def _dot(a, b):
    return jnp.dot(a, b, preferred_element_type=_F32)


def _silu(g):
    return g * jax.nn.sigmoid(g)


_NORM_CHUNK_ROWS = 256

def _rms_matmul_kernel(x_ref, g_ref, w_ref, *rest, has_col_scale, norm_rows):
    cs_ref, o_ref, h_scr = rest if has_col_scale else (None,) + rest

    def project(rows):
        out = _dot(h_scr[rows, :], w_ref[...].astype(_BF16))
        if has_col_scale:
            out = out * cs_ref[...]
        o_ref[rows, :] = out.astype(o_ref.dtype)

    @pl.when(pl.program_id(1) == 0)
    def _():
        for r0 in range(0, x_ref.shape[0], norm_rows):
            rows = slice(r0, r0 + norm_rows)
            x = x_ref[rows, :]
            ms = jnp.mean(x * x, axis=-1, keepdims=True)
            h_scr[rows, :] = (x * lax.rsqrt(ms + EPS) * g_ref[...]).astype(_BF16)
            project(rows)

    @pl.when(pl.program_id(1) != 0)
    def _():
        project(slice(None))


def rms_matmul(x, g, w, col_scale=None, *, tm, tn, out_dtype=_BF16):
    m, k = x.shape
    n = w.shape[1]
    assert m % tm == 0 and n % tn == 0
    has_cs = col_scale is not None
    cs_specs = [pl.BlockSpec((1, tn), lambda i, j: (0, j))] if has_cs else []
    cs_args = [col_scale.reshape(1, n)] if has_cs else []
    norm_rows = min(tm, _NORM_CHUNK_ROWS)
    assert tm % norm_rows == 0
    return pl.pallas_call(
        functools.partial(_rms_matmul_kernel, has_col_scale=has_cs, norm_rows=norm_rows),
        grid=(m // tm, n // tn),
        in_specs=[
            pl.BlockSpec((tm, k), lambda i, j: (i, 0)),
            pl.BlockSpec((1, k), lambda i, j: (0, 0)),
            pl.BlockSpec((k, tn), lambda i, j: (0, j)),
        ] + cs_specs,
        out_specs=pl.BlockSpec((tm, tn), lambda i, j: (i, j)),
        out_shape=jax.ShapeDtypeStruct((m, n), out_dtype),
        scratch_shapes=[pltpu.VMEM((tm, k), _BF16)],
        compiler_params=pltpu.CompilerParams(
            dimension_semantics=("parallel", "arbitrary"),
            vmem_limit_bytes=_VMEM_LIMIT),
        name="rms_matmul",
    )(x, g.reshape(1, k), w, *cs_args)


_LOG2E = math.log2(math.e)
_ONES_ROWS = 16
_VT_ROWS = HEAD_DIM + _ONES_ROWS


def _first_flash_step(z, rb, vt, acc_ref):
    m = jnp.max(z, axis=0, keepdims=True) + rb
    p = jnp.exp2(z - (m - rb))
    acc_ref[...] = _dot(vt, p.astype(_BF16))
    return m


def _flash_step(z_ref, z_max, rb, vt, m, acc_ref):
    m_new = jnp.maximum(m, z_max + rb)
    p = jnp.exp2(z_ref[...] - (m_new - rb))
    acc_ref[...] = jnp.exp2(m - m_new) * acc_ref[...] + _dot(vt, p.astype(_BF16))
    return m_new


_F32_AS_BF16_PIECES = 3
_BIAS_LANES = _F32_AS_BF16_PIECES
_SUBLANES = 8


def _bias_key_lanes(lane, lane0):
    pos = lax.broadcasted_iota(jnp.int32, lane.shape, 0).astype(_F32)
    return jnp.where((lane >= lane0) & (lane < lane0 + _BIAS_LANES), pos, 0.0)


def _bias_query_lanes(slope, lane, lane0):
    rest = jnp.full(lane.shape, slope, _F32)
    out = jnp.zeros(lane.shape, _F32)
    for t in range(_BIAS_LANES):
        piece = rest.astype(_BF16).astype(_F32)
        out = jnp.where(lane == lane0 + t, piece, out)
        rest = rest - piece
    return out


def _fill_values_transposed(v_ref, vt_scr, g, tk):
    for n in range(vt_scr.shape[1]):
        vt_scr[g, n, :HEAD_DIM, :] = v_ref[n * tk:(n + 1) * tk, _head_lanes(g)].T
        vt_scr[g, n, HEAD_DIM:, :] = jnp.ones((_ONES_ROWS, tk), _BF16)


def _normalised(acc):
    return acc[:HEAD_DIM] / acc[HEAD_DIM:HEAD_DIM + 1]


def _pipelined_flash(n_streams, n_past, scores, first_update, update, sa_scr, sb_scr):
    streams = range(n_streams)
    for n in streams:
        scores(n, n_past, sb_scr)
    max_a = tuple(scores(n, 0, sa_scr) for n in streams)
    running = tuple(first_update(n, sb_scr) for n in streams)

    def pair(t, carry):
        running, max_a = (list(c) for c in carry)
        j0 = 2 * t
        max_b = [None] * n_streams
        for n in streams:
            max_b[n] = scores(n, j0 + 1, sb_scr)
            running[n] = update(n, j0, sa_scr, max_a[n], running[n])
        for n in streams:
            max_a[n] = scores(n, j0 + 2, sa_scr)
            running[n] = update(n, j0 + 1, sb_scr, max_b[n], running[n])
        return tuple(running), tuple(max_a)

    def two_pairs(t, carry):
        return pair(2 * t + 1, pair(2 * t, carry))

    n_pairs = n_past // 2
    carry = lax.fori_loop(0, n_pairs // 2, two_pairs, (running, max_a))
    running, max_a = lax.fori_loop(n_pairs - n_pairs % 2, n_pairs, pair, carry)

    @pl.when(n_past % 2 == 1)
    def _():
        for n in streams:
            update(n, n_past - 1, sa_scr, max_a[n], running[n])


def _head_lanes(g):
    return slice(g * HEAD_DIM, (g + 1) * HEAD_DIM)


def _head_group_spec(rows, tq, width, first_blk, hps, seq_buffers):
    assert first_blk % hps == 0
    index_map = lambda bi, hi, i: (bi, i if rows == tq else 0, first_blk // hps + hi)
    if rows == tq:
        return pl.BlockSpec((None, rows, width), index_map)
    return pl.BlockSpec((None, rows, width), index_map,
                        pipeline_mode=pl.Buffered(seq_buffers))


def _diff_attention_kernel(slopes_ref, lq1_ref, lk1_ref, lq2_ref, lk2_ref, subg_ref,
                           q_ref, k_ref, v_ref, gate_ref, o_ref,
                           vt_scr, ka_scr, acc_scr, sa_scr, sb_scr,
                           *, lam_init, tq, heads_per_step):
    i = pl.program_id(2)
    heads = range(heads_per_step)
    lane = lax.broadcasted_iota(jnp.int32, (tq, HEAD_DIM), 1)
    own_half = (lane < DIFF_HALF, lane >= DIFF_HALF)
    bias_lane0 = (DIFF_HALF, 0)

    @pl.when(i == 0)
    def _():
        for g in heads:
            _fill_values_transposed(v_ref, vt_scr, g, tq)
            for n in range(ka_scr.shape[2]):
                k = k_ref[n * tq:(n + 1) * tq, _head_lanes(g)].astype(_F32)
                for c in range(2):
                    ka_scr[g, c, n] = jnp.where(
                        own_half[c], k, _bias_key_lanes(lane, bias_lane0[c])).astype(_BF16)

    key_idx = lax.broadcasted_iota(jnp.int32, (tq, tq), 0)
    qry_idx = lax.broadcasted_iota(jnp.int32, (tq, tq), 1)
    qry_pos = lax.broadcasted_iota(jnp.int32, (1, tq), 1).astype(_F32)

    streams = []
    for g in heads:
        slope = slopes_ref[pl.program_id(1) * heads_per_step + g] * _LOG2E
        q = q_ref[:, _head_lanes(g)].astype(_F32)
        for c in range(2):
            qc = jnp.where(own_half[c], q, _bias_query_lanes(slope, lane, bias_lane0[c]))
            streams.append((g, c, slope, qc.astype(_BF16)))

    def scores(n, j, s_buf):
        g, c, _, qc = streams[n]
        s = _dot_nt(ka_scr[g, c, j], qc)
        s_buf[n] = s
        return jnp.max(s, axis=0, keepdims=True)

    def first_update(n, s_buf):
        g, c, slope, _ = streams[n]
        z = jnp.where(key_idx <= qry_idx, s_buf[n], _NEG_INF)
        return _first_flash_step(z, -slope * qry_pos, vt_scr[g, i], acc_scr.at[g, c])

    def update(n, j, s_buf, tile_max, running_max):
        g, c, slope, _ = streams[n]
        rb = -slope * qry_pos - slope * ((i - j) * tq).astype(_F32)
        return _flash_step(s_buf.at[n], tile_max, rb, vt_scr[g, j], running_max,
                           acc_scr.at[g, c])

    _pipelined_flash(len(streams), i, scores, first_update, update, sa_scr, sb_scr)

    lam = (jnp.exp(jnp.sum(lq1_ref[...] * lk1_ref[...], axis=-1, keepdims=True))
           - jnp.exp(jnp.sum(lq2_ref[...] * lk2_ref[...], axis=-1, keepdims=True))
           + lam_init)
    for g in heads:
        o_t = _normalised(acc_scr[g, 0]) - lam * _normalised(acc_scr[g, 1])
        ms = jnp.mean(o_t * o_t, axis=0, keepdims=True)
        o = (o_t * lax.rsqrt(ms + EPS)).T * subg_ref[...] * (1.0 - lam_init)
        gate = gate_ref[:, _head_lanes(g)].astype(_F32)
        o_ref[:, _head_lanes(g)] = (o * _silu(gate)).astype(o_ref.dtype)


def diff_attention(proj, slopes, lq1, lk1, lq2, lk2, subg, *, lam_init, tq, heads_per_step):
    b, s, _ = proj.shape
    hps = heads_per_step
    assert s % tq == 0 and DIFF_HEADS % hps == 0
    width = hps * HEAD_DIM
    vec = lambda n: pl.BlockSpec((1, n), lambda bi, hi, i: (0, 0))
    spec = functools.partial(_head_group_spec, tq=tq, width=width, hps=hps, seq_buffers=2)
    scratch = [pltpu.VMEM((hps, s // tq, _VT_ROWS, tq), _BF16),
               pltpu.VMEM((hps, 2, s // tq, tq, HEAD_DIM), _BF16),
               pltpu.VMEM((hps, 2, _VT_ROWS, tq), _F32),
               pltpu.VMEM((2 * hps, tq, tq), _F32),
               pltpu.VMEM((2 * hps, tq, tq), _F32)]
    return pl.pallas_call(
        functools.partial(_diff_attention_kernel, lam_init=lam_init, tq=tq,
                          heads_per_step=hps),
        grid=(b, DIFF_HEADS // hps, s // tq),
        in_specs=[
            pl.BlockSpec(memory_space=pltpu.SMEM),
            vec(DIFF_HALF), vec(DIFF_HALF), vec(DIFF_HALF), vec(DIFF_HALF), vec(HEAD_DIM),
            spec(tq, first_blk=_DQ_BLK), spec(s, first_blk=_DK_BLK), spec(s, first_blk=_DV_BLK),
            spec(tq, first_blk=_GATE_BLK),
        ],
        out_specs=pl.BlockSpec((None, tq, width), lambda bi, hi, i: (bi, i, hi)),
        out_shape=jax.ShapeDtypeStruct((b, s, DIFF_WIDTH), _BF16),
        scratch_shapes=scratch,
        compiler_params=pltpu.CompilerParams(
            dimension_semantics=("parallel", "parallel", "arbitrary"),
            vmem_limit_bytes=_VMEM_LIMIT),
        name="diff_attention",
    )(slopes, lq1.reshape(1, -1), lk1.reshape(1, -1), lq2.reshape(1, -1), lk2.reshape(1, -1),
      subg.reshape(1, -1), proj, proj, proj, proj)


def _moba_attention_kernel(slopes_ref, q_ref, k_ref, v_ref, gate_ref, o_ref,
                           kmean_scr, kpieces_scr, vt_scr, sel_scr, acc_scr, sa_scr, sb_scr,
                           *, n_blocks, heads_per_step):
    i = pl.program_id(2)
    tq = MOBA_BLOCK
    heads = range(heads_per_step)

    lane = lax.broadcasted_iota(jnp.int32, (tq, HEAD_DIM), 1)

    @pl.when(i == 0)
    def _():
        for g in heads:
            _fill_values_transposed(v_ref, vt_scr, g, tq)
            for n in range(n_blocks):
                k = k_ref[n * MOBA_BLOCK:(n + 1) * MOBA_BLOCK, _head_lanes(g)]
                kmean_scr[g, n:n + 1, :] = (jnp.sum(k.astype(_F32), axis=0, keepdims=True)
                                            * (1.0 / MOBA_BLOCK))
            rest = kmean_scr[g]
            for t in range(_F32_AS_BF16_PIECES):
                piece = rest.astype(_BF16)
                kpieces_scr[g, t * n_blocks:(t + 1) * n_blocks, :] = piece
                rest = rest - piece.astype(_F32)

    key_idx = lax.broadcasted_iota(jnp.int32, (tq, tq), 0)
    qry_idx = lax.broadcasted_iota(jnp.int32, (tq, tq), 1)
    qry_pos = lax.broadcasted_iota(jnp.int32, (1, tq), 1).astype(_F32)
    blk = lax.broadcasted_iota(jnp.int32, (n_blocks, tq), 0)
    past = blk < i

    slopes = [slopes_ref[pl.program_id(1) * heads_per_step + g] * _LOG2E for g in heads]
    qs = [q_ref[:, _head_lanes(g)] for g in heads]
    qa_ts = [jnp.concatenate([qs[g].astype(_F32), _bias_query_lanes(slopes[g], lane, 0)],
                             axis=1).T.astype(_BF16) for g in heads]
    key_pos_lanes = _bias_key_lanes(lane, 0).astype(_BF16)

    def scores(g, j, s_buf):
        ka = jnp.concatenate([k_ref[pl.ds(j * tq, tq), _head_lanes(g)], key_pos_lanes], axis=1)
        s = _dot(ka, qa_ts[g])
        s_buf[g] = s
        return jnp.max(s, axis=0, keepdims=True)

    def select_blocks(g):
        parts = _dot_nt(kpieces_scr[g], qs[g])
        gate = sum(parts[t * n_blocks:(t + 1) * n_blocks] for t in range(_F32_AS_BF16_PIECES))
        gate = jnp.where(past, gate, _NEG_INF)
        for r0 in range(0, n_blocks, _SUBLANES):
            tile = gate[r0:r0 + _SUBLANES]
            row = lax.broadcasted_iota(jnp.int32, tile.shape, 0) + r0
            rank = jnp.zeros(tile.shape, _F32)
            for n in range(n_blocks):
                g_n = gate[n:n + 1, :]
                if n < r0:
                    beats = g_n >= tile
                elif n >= r0 + _SUBLANES:
                    beats = g_n > tile
                else:
                    beats = (g_n > tile) | ((g_n == tile) & (n < row))
                rank = rank + jnp.where(beats, 1.0, 0.0)
            sel_scr[g, r0:r0 + _SUBLANES, :] = jnp.where(
                (row < i) & (rank < MOBA_TOPK), 0.0, _NEG_INF)

    def first_update(g, s_buf):
        select_blocks(g)
        z = jnp.where(key_idx <= qry_idx, s_buf[g], _NEG_INF)
        return _first_flash_step(z, -slopes[g] * qry_pos, vt_scr[g, i], acc_scr.at[g])

    def update(g, j, s_buf, tile_max, running_max):
        rb = (-slopes[g] * qry_pos - slopes[g] * ((i - j) * tq).astype(_F32)
              + sel_scr[g, pl.ds(j, 1), :])
        return _flash_step(s_buf.at[g], tile_max, rb, vt_scr[g, j], running_max, acc_scr.at[g])

    _pipelined_flash(heads_per_step, i, scores, first_update, update, sa_scr, sb_scr)
    for g in heads:
        o = _normalised(acc_scr[g]).T
        gate = gate_ref[:, _head_lanes(g)].astype(_F32)
        o_ref[:, _head_lanes(g)] = (o * _silu(gate)).astype(o_ref.dtype)


def moba_attention(proj, slopes, *, heads_per_step):
    b, s, _ = proj.shape
    hps = heads_per_step
    assert s % MOBA_BLOCK == 0 and MOBA_HEADS % hps == 0
    n_blocks = s // MOBA_BLOCK
    assert n_blocks % _SUBLANES == 0
    tq = MOBA_BLOCK
    width = hps * HEAD_DIM
    spec = functools.partial(_head_group_spec, tq=tq, width=width, hps=hps, seq_buffers=1)
    return pl.pallas_call(
        functools.partial(_moba_attention_kernel, n_blocks=n_blocks, heads_per_step=hps),
        grid=(b, MOBA_HEADS // hps, n_blocks),
        in_specs=[
            pl.BlockSpec(memory_space=pltpu.SMEM),
            spec(tq, first_blk=_MQ_BLK), spec(s, first_blk=_MK_BLK), spec(s, first_blk=_MV_BLK),
            spec(tq, first_blk=_GATE_BLK + DIFF_HEADS),
        ],
        out_specs=pl.BlockSpec((None, tq, width), lambda bi, hi, i: (bi, i, hi)),
        out_shape=jax.ShapeDtypeStruct((b, s, MOBA_WIDTH), _BF16),
        scratch_shapes=[pltpu.VMEM((hps, n_blocks, HEAD_DIM), _F32),
                        pltpu.VMEM((hps, _F32_AS_BF16_PIECES * n_blocks, HEAD_DIM), _BF16),
                        pltpu.VMEM((hps, n_blocks, _VT_ROWS, tq), _BF16),
                        pltpu.VMEM((hps, n_blocks, tq), _F32),
                        pltpu.VMEM((hps, _VT_ROWS, tq), _F32),
                        pltpu.VMEM((hps, tq, tq), _F32),
                        pltpu.VMEM((hps, tq, tq), _F32)],
        compiler_params=pltpu.CompilerParams(
            dimension_semantics=("parallel", "parallel", "arbitrary"),
            vmem_limit_bytes=_VMEM_LIMIT),
        name="moba_attention",
    )(slopes, proj, proj, proj, proj)


def _layer_tail_kernel(yd_ref, ym_ref, x_ref, wa_ref, wb_ref, g_ref, wq_ref, kv_ref, wo_ref,
                       fg_ref, o_ref, q_scr, o_scr, *, apply_final, sub_rows):
    subs = [slice(r, r + sub_rows) for r in range(0, x_ref.shape[0], sub_rows)]
    xs = [x_ref[r, :] + _dot(yd_ref[r, :], wa_ref[...]) + _dot(ym_ref[r, :], wb_ref[...])
          for r in subs]
    for r, x in zip(subs, xs):
        ms = jnp.mean(x * x, axis=-1, keepdims=True)
        h = (x * lax.rsqrt(ms + EPS) * g_ref[...]).astype(_BF16)
        q_scr[r, :] = (_dot(h, wq_ref[...]) * (HEAD_DIM ** -0.5 * _LOG2E)).astype(_BF16)
    for r in subs:
        for hh in range(MEM_HEADS):
            lo, hi = hh * HEAD_DIM, (hh + 1) * HEAD_DIM
            s = _dot_nt(q_scr[r, lo:hi], kv_ref[:, lo:hi])
            p = jnp.exp2(s - jnp.max(s, axis=-1, keepdims=True))
            l = jnp.sum(p, axis=-1, keepdims=True)
            o_h = _dot(p.astype(_BF16), kv_ref[:, MEM_WIDTH + lo:MEM_WIDTH + hi]) / l
            o_scr[r, lo:hi] = o_h.astype(_BF16)
    for r, x in zip(subs, xs):
        x = x + _dot(o_scr[r, :], wo_ref[...])
        if apply_final:
            ms = jnp.mean(x * x, axis=-1, keepdims=True)
            x = x * lax.rsqrt(ms + EPS) * fg_ref[...]
        o_ref[r, :] = x


def layer_tail(yd, ym, x, w_out, g, wq, kv, wo, final_g, *, apply_final, tm, sub_rows):
    b, s, d = x.shape
    mlen = kv.shape[1]
    ka, kb = yd.shape[-1], ym.shape[-1]
    assert s % tm == 0 and ka == kb and w_out.shape[0] == ka + kb
    rows = lambda width: pl.BlockSpec((None, tm, width), lambda bi, i: (bi, i, 0))
    whole = lambda shape: pl.BlockSpec(shape, lambda bi, i: (0,) * len(shape))
    return pl.pallas_call(
        functools.partial(_layer_tail_kernel, apply_final=apply_final, sub_rows=sub_rows),
        grid=(b, s // tm),
        in_specs=[
            rows(ka), rows(kb), rows(d),
            pl.BlockSpec((ka, d), lambda bi, i: (0, 0)),
            pl.BlockSpec((kb, d), lambda bi, i: (1, 0)),
            whole((1, d)), whole((d, MEM_WIDTH)),
            pl.BlockSpec((None, mlen, 2 * MEM_WIDTH), lambda bi, i: (bi, 0, 0)),
            whole((MEM_WIDTH, d)), whole((1, d)),
        ],
        out_specs=rows(d),
        out_shape=jax.ShapeDtypeStruct((b, s, d), _F32),
        scratch_shapes=[pltpu.VMEM((tm, MEM_WIDTH), _BF16),
                        pltpu.VMEM((tm, MEM_WIDTH), _BF16)],
        compiler_params=pltpu.CompilerParams(
            dimension_semantics=("parallel", "parallel"),
            vmem_limit_bytes=_VMEM_LIMIT),
        name="layer_tail",
    )(yd, ym, x, w_out, w_out, g.reshape(1, d), wq, kv, wo, final_g.reshape(1, d))


def _in_proj_col_scale(n_cols):
    cs = np.ones((n_cols,), np.float32)
    cs[_DQ_BLK * HEAD_DIM:_DK_BLK * HEAD_DIM] = DIFF_HALF ** -0.5 * _LOG2E
    cs[_MQ_BLK * HEAD_DIM:_MK_BLK * HEAD_DIM] = HEAD_DIM ** -0.5 * _LOG2E
    return jnp.asarray(cs)


def _alibi_slopes(n):
    return jnp.asarray(2.0 ** (-8.0 * np.arange(1, n + 1) / n), dtype=_F32)


_IN_PROJ_TILE = (1024, 1024)
_ATTN_Q_TILE = 256
_DIFF_HEADS_PER_STEP = 4
_MOBA_HEADS_PER_STEP = 8
_TAIL_ROW_TILE = 512
_TAIL_SUB_ROWS = 256


def kernel(x, mem, norm_mix_g, w_in, lambda_q1, lambda_k1, lambda_q2, lambda_k2, subln_g,
           w_out, norm_mem_g, mem_norm_g, w_q_mem, w_kv_mem, w_o_mem, final_g):
    b, s, d = x.shape
    mlen = mem.shape[1]
    depth = w_in.shape[0]
    diff_slopes = _alibi_slopes(DIFF_HEADS)
    moba_slopes = _alibi_slopes(MOBA_HEADS)
    mem2 = mem.reshape(b * mlen, d)
    for l in range(depth):
        lam_init = 0.8 - 0.6 * math.exp(-0.3 * l)
        proj = rms_matmul(x.reshape(b * s, d), norm_mix_g[l], w_in[l],
                          _in_proj_col_scale(w_in.shape[-1]),
                          tm=_IN_PROJ_TILE[0], tn=_IN_PROJ_TILE[1]).reshape(b, s, -1)
        y_d = diff_attention(proj, diff_slopes, lambda_q1[l], lambda_k1[l], lambda_q2[l],
                             lambda_k2[l], subln_g[l], lam_init=lam_init, tq=_ATTN_Q_TILE,
                             heads_per_step=_DIFF_HEADS_PER_STEP)
        y_m = moba_attention(proj, moba_slopes, heads_per_step=_MOBA_HEADS_PER_STEP)
        kv = rms_matmul(mem2, mem_norm_g[l], w_kv_mem[l],
                        tm=b * mlen, tn=2 * MEM_WIDTH).reshape(b, mlen, 2 * MEM_WIDTH)
        x = layer_tail(y_d, y_m, x, w_out[l].astype(_BF16), norm_mem_g[l],
                       w_q_mem[l].astype(_BF16), kv, w_o_mem[l].astype(_BF16), final_g,
                       apply_final=(l == depth - 1), tm=_TAIL_ROW_TILE,
                       sub_rows=_TAIL_SUB_ROWS)
    return x
```

```python
import functools
import math

import jax
import jax.numpy as jnp
import numpy as np
from jax import lax
from jax.experimental import pallas as pl
from jax.experimental.pallas import tpu as pltpu

HEAD_DIM = 128
DIFF_HEADS = 8
DIFF_HALF = HEAD_DIM // 2
DIFF_WIDTH = DIFF_HEADS * HEAD_DIM
MOBA_HEADS = 8
MOBA_WIDTH = MOBA_HEADS * HEAD_DIM
MOBA_BLOCK = 256
MOBA_TOPK = 3
MEM_HEADS = 4
MEM_WIDTH = MEM_HEADS * HEAD_DIM
EPS = 1e-6

_DQ_BLK = 0
_DK_BLK = DIFF_HEADS
_DV_BLK = 2 * DIFF_HEADS
_MQ_BLK = 3 * DIFF_HEADS
_MK_BLK = 3 * DIFF_HEADS + MOBA_HEADS
_MV_BLK = 3 * DIFF_HEADS + 2 * MOBA_HEADS
_GATE_BLK = 3 * DIFF_HEADS + 3 * MOBA_HEADS

_V7X_VMEM_BYTES = 64 * 1024 * 1024
_VMEM_LIMIT = _V7X_VMEM_BYTES * 3 // 4

_BF16 = jnp.bfloat16
_F32 = jnp.float32
_NEG_INF = float("-inf")


def _dot_nt(a, b):
    return lax.dot_general(a, b, (((1,), (1,)), ((), ())), preferred_element_type=_F32)


def _dot(a, b):
    return jnp.dot(a, b, preferred_element_type=_F32)


def _silu(g):
    return g * jax.nn.sigmoid(g)


_NORM_CHUNK_ROWS = 256

def _rms_matmul_kernel(x_ref, g_ref, w_ref, *rest, has_col_scale, norm_rows):
    cs_ref, o_ref, h_scr = rest if has_col_scale else (None,) + rest

    def project(rows):
        out = _dot(h_scr[rows, :], w_ref[...].astype(_BF16))
        if has_col_scale:
            out = out * cs_ref[...]
        o_ref[rows, :] = out.astype(o_ref.dtype)

    @pl.when(pl.program_id(1) == 0)
    def _():
        for r0 in range(0, x_ref.shape[0], norm_rows):
            rows = slice(r0, r0 + norm_rows)
            x = x_ref[rows, :]
            ms = jnp.mean(x * x, axis=-1, keepdims=True)
            h_scr[rows, :] = (x * lax.rsqrt(ms + EPS) * g_ref[...]).astype(_BF16)
            project(rows)

    @pl.when(pl.program_id(1) != 0)
    def _():
        project(slice(None))


def rms_matmul(x, g, w, col_scale=None, *, tm, tn, out_dtype=_BF16):
    m, k = x.shape
    n = w.shape[1]
    assert m % tm == 0 and n % tn == 0
    has_cs = col_scale is not None
    cs_specs = [pl.BlockSpec((1, tn), lambda i, j: (0, j))] if has_cs else []
    cs_args = [col_scale.reshape(1, n)] if has_cs else []
    norm_rows = min(tm, _NORM_CHUNK_ROWS)
    assert tm % norm_rows == 0
    return pl.pallas_call(
        functools.partial(_rms_matmul_kernel, has_col_scale=has_cs, norm_rows=norm_rows),
        grid=(m // tm, n // tn),
        in_specs=[
            pl.BlockSpec((tm, k), lambda i, j: (i, 0)),
            pl.BlockSpec((1, k), lambda i, j: (0, 0)),
            pl.BlockSpec((k, tn), lambda i, j: (0, j)),
        ] + cs_specs,
        out_specs=pl.BlockSpec((tm, tn), lambda i, j: (i, j)),
        out_shape=jax.ShapeDtypeStruct((m, n), out_dtype),
        scratch_shapes=[pltpu.VMEM((tm, k), _BF16)],
        compiler_params=pltpu.CompilerParams(
            dimension_semantics=("parallel", "arbitrary"),
            vmem_limit_bytes=_VMEM_LIMIT),
        name="rms_matmul",
    )(x, g.reshape(1, k), w, *cs_args)


_LOG2E = math.log2(math.e)
_ONES_ROWS = 16
_VT_ROWS = HEAD_DIM + _ONES_ROWS


def _first_flash_step(z, rb, vt, acc_ref):
    m = jnp.max(z, axis=0, keepdims=True) + rb
    p = jnp.exp2(z - (m - rb))
    acc_ref[...] = _dot(vt, p.astype(_BF16))
    return m


def _flash_step(z_ref, z_max, rb, vt, m, acc_ref):
    m_new = jnp.maximum(m, z_max + rb)
    p = jnp.exp2(z_ref[...] - (m_new - rb))
    acc_ref[...] = jnp.exp2(m - m_new) * acc_ref[...] + _dot(vt, p.astype(_BF16))
    return m_new


_F32_AS_BF16_PIECES = 3
_BIAS_LANES = _F32_AS_BF16_PIECES
_SUBLANES = 8


def _bias_key_lanes(lane, lane0):
    pos = lax.broadcasted_iota(jnp.int32, lane.shape, 0).astype(_F32)
    return jnp.where((lane >= lane0) & (lane < lane0 + _BIAS_LANES), pos, 0.0)


def _bias_query_lanes(slope, lane, lane0):
    rest = jnp.full(lane.shape, slope, _F32)
    out = jnp.zeros(lane.shape, _F32)
    for t in range(_BIAS_LANES):
        piece = rest.astype(_BF16).astype(_F32)
        out = jnp.where(lane == lane0 + t, piece, out)
        rest = rest - piece
    return out


def _stage_values_transposed(v_ref, vt_scr, g, n, tk):
    vt_scr[g, n, :HEAD_DIM, :] = v_ref[pl.ds(n * tk, tk), _head_lanes(g)].T
    vt_scr[g, n, HEAD_DIM:, :] = jnp.ones((_ONES_ROWS, tk), _BF16)


def _normalised(acc):
    return acc[:HEAD_DIM] / acc[HEAD_DIM:HEAD_DIM + 1]


def _pipelined_flash(n_streams, n_past, scores, first_update, update, sa_scr, sb_scr):
    streams = range(n_streams)
    for n in streams:
        scores(n, n_past, sb_scr)
    max_a = tuple(scores(n, 0, sa_scr) for n in streams)
    running = tuple(first_update(n, sb_scr) for n in streams)

    def pair(t, carry):
        running, max_a = (list(c) for c in carry)
        j0 = 2 * t
        max_b = [None] * n_streams
        for n in streams:
            max_b[n] = scores(n, j0 + 1, sb_scr)
            running[n] = update(n, j0, sa_scr, max_a[n], running[n])
        for n in streams:
            max_a[n] = scores(n, j0 + 2, sa_scr)
            running[n] = update(n, j0 + 1, sb_scr, max_b[n], running[n])
        return tuple(running), tuple(max_a)

    def two_pairs(t, carry):
        return pair(2 * t + 1, pair(2 * t, carry))

    n_pairs = n_past // 2
    carry = lax.fori_loop(0, n_pairs // 2, two_pairs, (running, max_a))
    running, max_a = lax.fori_loop(n_pairs - n_pairs % 2, n_pairs, pair, carry)

    @pl.when(n_past % 2 == 1)
    def _():
        for n in streams:
            update(n, n_past - 1, sa_scr, max_a[n], running[n])


def _head_lanes(g):
    return slice(g * HEAD_DIM, (g + 1) * HEAD_DIM)


def _head_group_spec(rows, tq, width, first_blk, hps, seq_buffers):
    assert first_blk % hps == 0
    index_map = lambda bi, hi, i: (bi, i if rows == tq else 0, first_blk // hps + hi)
    if rows == tq:
        return pl.BlockSpec((None, rows, width), index_map)
    return pl.BlockSpec((None, rows, width), index_map,
                        pipeline_mode=pl.Buffered(seq_buffers))


def _diff_attention_kernel(slopes_ref, lq1_ref, lk1_ref, lq2_ref, lk2_ref, subg_ref,
                           q_ref, k_ref, v_ref, gate_ref, o_ref,
                           vt_scr, ka_scr, acc_scr, sa_scr, sb_scr,
                           *, lam_init, tq, heads_per_step):
    i = pl.program_id(2)
    heads = range(heads_per_step)
    lane = lax.broadcasted_iota(jnp.int32, (tq, HEAD_DIM), 1)
    own_half = (lane < DIFF_HALF, lane >= DIFF_HALF)
    bias_lane0 = (DIFF_HALF, 0)

    for g in heads:
        _stage_values_transposed(v_ref, vt_scr, g, i, tq)
        k = k_ref[pl.ds(i * tq, tq), _head_lanes(g)].astype(_F32)
        for c in range(2):
            ka_scr[g, c, i] = jnp.where(
                own_half[c], k, _bias_key_lanes(lane, bias_lane0[c])).astype(_BF16)

    key_idx = lax.broadcasted_iota(jnp.int32, (tq, tq), 0)
    qry_idx = lax.broadcasted_iota(jnp.int32, (tq, tq), 1)
    qry_pos = lax.broadcasted_iota(jnp.int32, (1, tq), 1).astype(_F32)

    streams = []
    for g in heads:
        slope = slopes_ref[pl.program_id(1) * heads_per_step + g] * _LOG2E
        q = q_ref[:, _head_lanes(g)].astype(_F32)
        for c in range(2):
            qc = jnp.where(own_half[c], q, _bias_query_lanes(slope, lane, bias_lane0[c]))
            streams.append((g, c, slope, qc.astype(_BF16)))

    def scores(n, j, s_buf):
        g, c, _, qc = streams[n]
        s = _dot_nt(ka_scr[g, c, j], qc)
        s_buf[n] = s
        return jnp.max(s, axis=0, keepdims=True)

    def first_update(n, s_buf):
        g, c, slope, _ = streams[n]
        z = jnp.where(key_idx <= qry_idx, s_buf[n], _NEG_INF)
        return _first_flash_step(z, -slope * qry_pos, vt_scr[g, i], acc_scr.at[g, c])

    def update(n, j, s_buf, tile_max, running_max):
        g, c, slope, _ = streams[n]
        rb = -slope * qry_pos - slope * ((i - j) * tq).astype(_F32)
        return _flash_step(s_buf.at[n], tile_max, rb, vt_scr[g, j], running_max,
                           acc_scr.at[g, c])

    _pipelined_flash(len(streams), i, scores, first_update, update, sa_scr, sb_scr)

    lam = (jnp.exp(jnp.sum(lq1_ref[...] * lk1_ref[...], axis=-1, keepdims=True))
           - jnp.exp(jnp.sum(lq2_ref[...] * lk2_ref[...], axis=-1, keepdims=True))
           + lam_init)
    for g in heads:
        o_t = _normalised(acc_scr[g, 0]) - lam * _normalised(acc_scr[g, 1])
        ms = jnp.mean(o_t * o_t, axis=0, keepdims=True)
        o = (o_t * lax.rsqrt(ms + EPS)).T * subg_ref[...] * (1.0 - lam_init)
        gate = gate_ref[:, _head_lanes(g)].astype(_F32)
        o_ref[:, _head_lanes(g)] = (o * _silu(gate)).astype(o_ref.dtype)


def diff_attention(proj, slopes, lq1, lk1, lq2, lk2, subg, *, lam_init, tq, heads_per_step):
    b, s, _ = proj.shape
    hps = heads_per_step
    assert s % tq == 0 and DIFF_HEADS % hps == 0
    width = hps * HEAD_DIM
    vec = lambda n: pl.BlockSpec((1, n), lambda bi, hi, i: (0, 0))
    spec = functools.partial(_head_group_spec, tq=tq, width=width, hps=hps, seq_buffers=2)
    scratch = [pltpu.VMEM((hps, s // tq, _VT_ROWS, tq), _BF16),
               pltpu.VMEM((hps, 2, s // tq, tq, HEAD_DIM), _BF16),
               pltpu.VMEM((hps, 2, _VT_ROWS, tq), _F32),
               pltpu.VMEM((2 * hps, tq, tq), _F32),
               pltpu.VMEM((2 * hps, tq, tq), _F32)]
    return pl.pallas_call(
        functools.partial(_diff_attention_kernel, lam_init=lam_init, tq=tq,
                          heads_per_step=hps),
        grid=(b, DIFF_HEADS // hps, s // tq),
        in_specs=[
            pl.BlockSpec(memory_space=pltpu.SMEM),
            vec(DIFF_HALF), vec(DIFF_HALF), vec(DIFF_HALF), vec(DIFF_HALF), vec(HEAD_DIM),
            spec(tq, first_blk=_DQ_BLK), spec(s, first_blk=_DK_BLK), spec(s, first_blk=_DV_BLK),
            spec(tq, first_blk=_GATE_BLK),
        ],
        out_specs=pl.BlockSpec((None, tq, width), lambda bi, hi, i: (bi, i, hi)),
        out_shape=jax.ShapeDtypeStruct((b, s, DIFF_WIDTH), _BF16),
        scratch_shapes=scratch,
        compiler_params=pltpu.CompilerParams(
            dimension_semantics=("parallel", "parallel", "arbitrary"),
            vmem_limit_bytes=_VMEM_LIMIT),
        name="diff_attention",
    )(slopes, lq1.reshape(1, -1), lk1.reshape(1, -1), lq2.reshape(1, -1), lk2.reshape(1, -1),
      subg.reshape(1, -1), proj, proj, proj, proj)


def _moba_attention_kernel(slopes_ref, q_ref, k_ref, v_ref, gate_ref, o_ref,
                           kmean_scr, vt_scr, sel_scr, acc_scr, sa_scr, sb_scr,
                           *, n_blocks, heads_per_step):
    i = pl.program_id(2)
    tq = MOBA_BLOCK
    heads = range(heads_per_step)

    lane = lax.broadcasted_iota(jnp.int32, (tq, HEAD_DIM), 1)

    @pl.when(i == 0)
    def _():
        kmean_scr[...] = jnp.zeros(kmean_scr.shape, _F32)

    for g in heads:
        _stage_values_transposed(v_ref, vt_scr, g, i, tq)
        k = k_ref[pl.ds(i * tq, tq), _head_lanes(g)]
        kmean_scr[g, pl.ds(i, 1), :] = (jnp.sum(k.astype(_F32), axis=0, keepdims=True)
                                        * (1.0 / MOBA_BLOCK))

    key_idx = lax.broadcasted_iota(jnp.int32, (tq, tq), 0)
    qry_idx = lax.broadcasted_iota(jnp.int32, (tq, tq), 1)
    qry_pos = lax.broadcasted_iota(jnp.int32, (1, tq), 1).astype(_F32)
    blk = lax.broadcasted_iota(jnp.int32, (n_blocks, tq), 0)
    past = blk < i

    slopes = [slopes_ref[pl.program_id(1) * heads_per_step + g] * _LOG2E for g in heads]
    qs = [q_ref[:, _head_lanes(g)] for g in heads]
    qa_ts = [jnp.concatenate([qs[g].astype(_F32), _bias_query_lanes(slopes[g], lane, 0)],
                             axis=1).T.astype(_BF16) for g in heads]
    key_pos_lanes = _bias_key_lanes(lane, 0).astype(_BF16)

    def scores(g, j, s_buf):
        ka = jnp.concatenate([k_ref[pl.ds(j * tq, tq), _head_lanes(g)], key_pos_lanes], axis=1)
        s = _dot(ka, qa_ts[g])
        s_buf[g] = s
        return jnp.max(s, axis=0, keepdims=True)

    def select_blocks(g):
        rest = kmean_scr[g]
        pieces = []
        for _ in range(_F32_AS_BF16_PIECES):
            pieces.append(rest.astype(_BF16))
            rest = rest - pieces[-1].astype(_F32)
        parts = _dot_nt(jnp.concatenate(pieces, axis=0), qs[g])
        gate = sum(parts[t * n_blocks:(t + 1) * n_blocks] for t in range(_F32_AS_BF16_PIECES))
        gate = jnp.where(past, gate, _NEG_INF)
        for r0 in range(0, n_blocks, _SUBLANES):
            tile = gate[r0:r0 + _SUBLANES]
            row = lax.broadcasted_iota(jnp.int32, tile.shape, 0) + r0
            rank = jnp.zeros(tile.shape, _F32)
            for n in range(n_blocks):
                g_n = gate[n:n + 1, :]
                if n < r0:
                    beats = g_n >= tile
                elif n >= r0 + _SUBLANES:
                    beats = g_n > tile
                else:
                    beats = (g_n > tile) | ((g_n == tile) & (n < row))
                rank = rank + jnp.where(beats, 1.0, 0.0)
            sel_scr[g, r0:r0 + _SUBLANES, :] = jnp.where(
                (row < i) & (rank < MOBA_TOPK), 0.0, _NEG_INF)

    def first_update(g, s_buf):
        select_blocks(g)
        z = jnp.where(key_idx <= qry_idx, s_buf[g], _NEG_INF)
        return _first_flash_step(z, -slopes[g] * qry_pos, vt_scr[g, i], acc_scr.at[g])

    def update(g, j, s_buf, tile_max, running_max):
        rb = (-slopes[g] * qry_pos - slopes[g] * ((i - j) * tq).astype(_F32)
              + sel_scr[g, pl.ds(j, 1), :])
        return _flash_step(s_buf.at[g], tile_max, rb, vt_scr[g, j], running_max, acc_scr.at[g])

    _pipelined_flash(heads_per_step, i, scores, first_update, update, sa_scr, sb_scr)
    for g in heads:
        o = _normalised(acc_scr[g]).T
        gate = gate_ref[:, _head_lanes(g)].astype(_F32)
        o_ref[:, _head_lanes(g)] = (o * _silu(gate)).astype(o_ref.dtype)


def moba_attention(proj, slopes, *, heads_per_step):
    b, s, _ = proj.shape
    hps = heads_per_step
    assert s % MOBA_BLOCK == 0 and MOBA_HEADS % hps == 0
    n_blocks = s // MOBA_BLOCK
    assert n_blocks % _SUBLANES == 0
    tq = MOBA_BLOCK
    width = hps * HEAD_DIM
    spec = functools.partial(_head_group_spec, tq=tq, width=width, hps=hps, seq_buffers=1)
    return pl.pallas_call(
        functools.partial(_moba_attention_kernel, n_blocks=n_blocks, heads_per_step=hps),
        grid=(b, MOBA_HEADS // hps, n_blocks),
        in_specs=[
            pl.BlockSpec(memory_space=pltpu.SMEM),
            spec(tq, first_blk=_MQ_BLK), spec(s, first_blk=_MK_BLK), spec(s, first_blk=_MV_BLK),
            spec(tq, first_blk=_GATE_BLK + DIFF_HEADS),
        ],
        out_specs=pl.BlockSpec((None, tq, width), lambda bi, hi, i: (bi, i, hi)),
        out_shape=jax.ShapeDtypeStruct((b, s, MOBA_WIDTH), _BF16),
        scratch_shapes=[pltpu.VMEM((hps, n_blocks, HEAD_DIM), _F32),
                        pltpu.VMEM((hps, n_blocks, _VT_ROWS, tq), _BF16),
                        pltpu.VMEM((hps, n_blocks, tq), _F32),
                        pltpu.VMEM((hps, _VT_ROWS, tq), _F32),
                        pltpu.VMEM((hps, tq, tq), _F32),
                        pltpu.VMEM((hps, tq, tq), _F32)],
        compiler_params=pltpu.CompilerParams(
            dimension_semantics=("parallel", "parallel", "arbitrary"),
            vmem_limit_bytes=_VMEM_LIMIT),
        name="moba_attention",
    )(slopes, proj, proj, proj, proj)


def _layer_tail_kernel(yd_ref, ym_ref, x_ref, wa_ref, wb_ref, g_ref, wq_ref, kv_ref, wo_ref,
                       fg_ref, o_ref, q_scr, o_scr, *, apply_final, sub_rows):
    subs = [slice(r, r + sub_rows) for r in range(0, x_ref.shape[0], sub_rows)]
    xs = [x_ref[r, :] + _dot(yd_ref[r, :], wa_ref[...]) + _dot(ym_ref[r, :], wb_ref[...])
          for r in subs]
    for r, x in zip(subs, xs):
        ms = jnp.mean(x * x, axis=-1, keepdims=True)
        h = (x * lax.rsqrt(ms + EPS) * g_ref[...]).astype(_BF16)
        q_scr[r, :] = (_dot(h, wq_ref[...]) * (HEAD_DIM ** -0.5 * _LOG2E)).astype(_BF16)
    for r in subs:
        for hh in range(MEM_HEADS):
            lo, hi = hh * HEAD_DIM, (hh + 1) * HEAD_DIM
            s = _dot_nt(q_scr[r, lo:hi], kv_ref[:, lo:hi])
            p = jnp.exp2(s - jnp.max(s, axis=-1, keepdims=True))
            l = jnp.sum(p, axis=-1, keepdims=True)
            o_h = _dot(p.astype(_BF16), kv_ref[:, MEM_WIDTH + lo:MEM_WIDTH + hi]) / l
            o_scr[r, lo:hi] = o_h.astype(_BF16)
    for r, x in zip(subs, xs):
        x = x + _dot(o_scr[r, :], wo_ref[...])
        if apply_final:
            ms = jnp.mean(x * x, axis=-1, keepdims=True)
            x = x * lax.rsqrt(ms + EPS) * fg_ref[...]
        o_ref[r, :] = x


def layer_tail(yd, ym, x, w_out, g, wq, kv, wo, final_g, *, apply_final, tm, sub_rows):
    b, s, d = x.shape
    mlen = kv.shape[1]
    ka, kb = yd.shape[-1], ym.shape[-1]
    assert s % tm == 0 and ka == kb and w_out.shape[0] == ka + kb
    rows = lambda width: pl.BlockSpec((None, tm, width), lambda bi, i: (bi, i, 0))
    whole = lambda shape: pl.BlockSpec(shape, lambda bi, i: (0,) * len(shape))
    return pl.pallas_call(
        functools.partial(_layer_tail_kernel, apply_final=apply_final, sub_rows=sub_rows),
        grid=(b, s // tm),
        in_specs=[
            rows(ka), rows(kb), rows(d),
            pl.BlockSpec((ka, d), lambda bi, i: (0, 0)),
            pl.BlockSpec((kb, d), lambda bi, i: (1, 0)),
            whole((1, d)), whole((d, MEM_WIDTH)),
            pl.BlockSpec((None, mlen, 2 * MEM_WIDTH), lambda bi, i: (bi, 0, 0)),
            whole((MEM_WIDTH, d)), whole((1, d)),
        ],
        out_specs=rows(d),
        out_shape=jax.ShapeDtypeStruct((b, s, d), _F32),
        scratch_shapes=[pltpu.VMEM((tm, MEM_WIDTH), _BF16),
                        pltpu.VMEM((tm, MEM_WIDTH), _BF16)],
        compiler_params=pltpu.CompilerParams(
            dimension_semantics=("parallel", "parallel"),
            vmem_limit_bytes=_VMEM_LIMIT),
        name="layer_tail",
    )(yd, ym, x, w_out, w_out, g.reshape(1, d), wq, kv, wo, final_g.reshape(1, d))


def _in_proj_col_scale(n_cols):
    cs = np.ones((n_cols,), np.float32)
    cs[_DQ_BLK * HEAD_DIM:_DK_BLK * HEAD_DIM] = DIFF_HALF ** -0.5 * _LOG2E
    cs[_MQ_BLK * HEAD_DIM:_MK_BLK * HEAD_DIM] = HEAD_DIM ** -0.5 * _LOG2E
    return jnp.asarray(cs)


def _alibi_slopes(n):
    return jnp.asarray(2.0 ** (-8.0 * np.arange(1, n + 1) / n), dtype=_F32)


_IN_PROJ_TILE = (1024, 1024)
_ATTN_Q_TILE = 256
_DIFF_HEADS_PER_STEP = 4
_MOBA_HEADS_PER_STEP = 8
_TAIL_ROW_TILE = 512
_TAIL_SUB_ROWS = 256


def kernel(x, mem, norm_mix_g, w_in, lambda_q1, lambda_k1, lambda_q2, lambda_k2, subln_g,
           w_out, norm_mem_g, mem_norm_g, w_q_mem, w_kv_mem, w_o_mem, final_g):
    b, s, d = x.shape
    mlen = mem.shape[1]
    depth = w_in.shape[0]
    diff_slopes = _alibi_slopes(DIFF_HEADS)
    moba_slopes = _alibi_slopes(MOBA_HEADS)
    mem2 = mem.reshape(b * mlen, d)
    for l in range(depth):
        lam_init = 0.8 - 0.6 * math.exp(-0.3 * l)
        proj = rms_matmul(x.reshape(b * s, d), norm_mix_g[l], w_in[l],
                          _in_proj_col_scale(w_in.shape[-1]),
                          tm=_IN_PROJ_TILE[0], tn=_IN_PROJ_TILE[1]).reshape(b, s, -1)
        y_d = diff_attention(proj, diff_slopes, lambda_q1[l], lambda_k1[l], lambda_q2[l],
                             lambda_k2[l], subln_g[l], lam_init=lam_init, tq=_ATTN_Q_TILE,
                             heads_per_step=_DIFF_HEADS_PER_STEP)
        y_m = moba_attention(proj, moba_slopes, heads_per_step=_MOBA_HEADS_PER_STEP)
        kv = rms_matmul(mem2, mem_norm_g[l], w_kv_mem[l],
                        tm=b * mlen, tn=2 * MEM_WIDTH).reshape(b, mlen, 2 * MEM_WIDTH)
        x = layer_tail(y_d, y_m, x, w_out[l].astype(_BF16), norm_mem_g[l],
                       w_q_mem[l].astype(_BF16), kv, w_o_mem[l].astype(_BF16), final_g,
                       apply_final=(l == depth - 1), tm=_TAIL_ROW_TILE,
                       sub_rows=_TAIL_SUB_ROWS)
    return x
```

```python
import functools
import math

import jax
import jax.numpy as jnp
import numpy as np
from jax import lax
from jax.experimental import pallas as pl
from jax.experimental.pallas import tpu as pltpu

HEAD_DIM = 128
DIFF_HEADS = 8
DIFF_HALF = HEAD_DIM // 2
DIFF_WIDTH = DIFF_HEADS * HEAD_DIM
MOBA_HEADS = 8
MOBA_WIDTH = MOBA_HEADS * HEAD_DIM
MOBA_BLOCK = 256
MOBA_TOPK = 3
MEM_HEADS = 4
MEM_WIDTH = MEM_HEADS * HEAD_DIM
EPS = 1e-6

_DQ_BLK = 0
_DK_BLK = DIFF_HEADS
_DV_BLK = 2 * DIFF_HEADS
_MQ_BLK = 3 * DIFF_HEADS
_MK_BLK = 3 * DIFF_HEADS + MOBA_HEADS
_MV_BLK = 3 * DIFF_HEADS + 2 * MOBA_HEADS
_GATE_BLK = 3 * DIFF_HEADS + 3 * MOBA_HEADS

_V7X_VMEM_BYTES = 64 * 1024 * 1024
_VMEM_LIMIT = _V7X_VMEM_BYTES * 3 // 4

_BF16 = jnp.bfloat16
_F32 = jnp.float32
_NEG_INF = float("-inf")


def _dot_nt(a, b):
    return lax.dot_general(a, b, (((1,), (1,)), ((), ())), preferred_element_type=_F32)


def _dot(a, b):
    return jnp.dot(a, b, preferred_element_type=_F32)


def _silu(g):
    return g * jax.nn.sigmoid(g)


_NORM_CHUNK_ROWS = 256

def _rms_matmul_kernel(x_ref, g_ref, w_ref, *rest, has_col_scale, norm_rows):
    cs_ref, o_ref, h_scr = rest if has_col_scale else (None,) + rest

    def project(rows):
        out = _dot(h_scr[rows, :], w_ref[...].astype(_BF16))
        if has_col_scale:
            out = out * cs_ref[...]
        o_ref[rows, :] = out.astype(o_ref.dtype)

    @pl.when(pl.program_id(1) == 0)
    def _():
        for r0 in range(0, x_ref.shape[0], norm_rows):
            rows = slice(r0, r0 + norm_rows)
            x = x_ref[rows, :]
            ms = jnp.mean(x * x, axis=-1, keepdims=True)
            h_scr[rows, :] = (x * lax.rsqrt(ms + EPS) * g_ref[...]).astype(_BF16)
            project(rows)

    @pl.when(pl.program_id(1) != 0)
    def _():
        project(slice(None))


def rms_matmul(x, g, w, col_scale=None, *, tm, tn, out_dtype=_BF16):
    m, k = x.shape
    n = w.shape[1]
    assert m % tm == 0 and n % tn == 0
    has_cs = col_scale is not None
    cs_specs = [pl.BlockSpec((1, tn), lambda i, j: (0, j))] if has_cs else []
    cs_args = [col_scale.reshape(1, n)] if has_cs else []
    norm_rows = min(tm, _NORM_CHUNK_ROWS)
    assert tm % norm_rows == 0
    return pl.pallas_call(
        functools.partial(_rms_matmul_kernel, has_col_scale=has_cs, norm_rows=norm_rows),
        grid=(m // tm, n // tn),
        in_specs=[
            pl.BlockSpec((tm, k), lambda i, j: (i, 0)),
            pl.BlockSpec((1, k), lambda i, j: (0, 0)),
            pl.BlockSpec((k, tn), lambda i, j: (0, j)),
        ] + cs_specs,
        out_specs=pl.BlockSpec((tm, tn), lambda i, j: (i, j)),
        out_shape=jax.ShapeDtypeStruct((m, n), out_dtype),
        scratch_shapes=[pltpu.VMEM((tm, k), _BF16)],
        compiler_params=pltpu.CompilerParams(
            dimension_semantics=("parallel", "arbitrary"),
            vmem_limit_bytes=_VMEM_LIMIT),
        name="rms_matmul",
    )(x, g.reshape(1, k), w, *cs_args)


_LOG2E = math.log2(math.e)
_ONES_ROWS = 16
_VT_ROWS = HEAD_DIM + _ONES_ROWS


def _first_flash_step(z, rb, vt, acc_ref):
    m = jnp.max(z, axis=0, keepdims=True) + rb
    p = jnp.exp2(z - (m - rb))
    acc_ref[...] = _dot(vt, p.astype(_BF16))
    return m


def _flash_step(z_ref, z_max, rb, vt, m, acc_ref):
    m_new = jnp.maximum(m, z_max + rb)
    p = jnp.exp2(z_ref[...] - (m_new - rb))
    acc_ref[...] = jnp.exp2(m - m_new) * acc_ref[...] + _dot(vt, p.astype(_BF16))
    return m_new


_F32_AS_BF16_PIECES = 3
_BIAS_LANES = _F32_AS_BF16_PIECES
_SUBLANES = 8


def _bias_key_lanes(lane, lane0):
    pos = lax.broadcasted_iota(jnp.int32, lane.shape, 0).astype(_F32)
    return jnp.where((lane >= lane0) & (lane < lane0 + _BIAS_LANES), pos, 0.0)


def _bias_query_lanes(slope, lane, lane0):
    rest = jnp.full(lane.shape, slope, _F32)
    out = jnp.zeros(lane.shape, _F32)
    for t in range(_BIAS_LANES):
        piece = rest.astype(_BF16).astype(_F32)
        out = jnp.where(lane == lane0 + t, piece, out)
        rest = rest - piece
    return out


def _stage_values_transposed(v_tile, vt_scr, g, n):
    vt_scr[g, n, :HEAD_DIM, :] = v_tile.T
    vt_scr[g, n, HEAD_DIM:, :] = jnp.ones((_ONES_ROWS, v_tile.shape[0]), _BF16)


def _normalised(acc):
    return acc[:HEAD_DIM] / acc[HEAD_DIM:HEAD_DIM + 1]


def _pipelined_flash(n_streams, n_past, scores, first_update, update, sa_scr, sb_scr):
    streams = range(n_streams)
    for n in streams:
        scores(n, n_past, sb_scr)
    max_a = tuple(scores(n, 0, sa_scr) for n in streams)
    running = tuple(first_update(n, sb_scr) for n in streams)

    def pair(t, carry):
        running, max_a = (list(c) for c in carry)
        j0 = 2 * t
        max_b = [None] * n_streams
        for n in streams:
            max_b[n] = scores(n, j0 + 1, sb_scr)
            running[n] = update(n, j0, sa_scr, max_a[n], running[n])
        for n in streams:
            max_a[n] = scores(n, j0 + 2, sa_scr)
            running[n] = update(n, j0 + 1, sb_scr, max_b[n], running[n])
        return tuple(running), tuple(max_a)

    def two_pairs(t, carry):
        return pair(2 * t + 1, pair(2 * t, carry))

    n_pairs = n_past // 2
    carry = lax.fori_loop(0, n_pairs // 2, two_pairs, (running, max_a))
    running, max_a = lax.fori_loop(n_pairs - n_pairs % 2, n_pairs, pair, carry)

    @pl.when(n_past % 2 == 1)
    def _():
        for n in streams:
            update(n, n_past - 1, sa_scr, max_a[n], running[n])


def _head_lanes(g):
    return slice(g * HEAD_DIM, (g + 1) * HEAD_DIM)


def _head_group_spec(rows, tq, width, first_blk, hps, seq_buffers):
    assert first_blk % hps == 0
    index_map = lambda bi, hi, i: (bi, i if rows == tq else 0, first_blk // hps + hi)
    if rows == tq:
        return pl.BlockSpec((None, rows, width), index_map)
    return pl.BlockSpec((None, rows, width), index_map,
                        pipeline_mode=pl.Buffered(seq_buffers))


def _diff_attention_kernel(slopes_ref, lq1_ref, lk1_ref, lq2_ref, lk2_ref, subg_ref,
                           q_ref, k_ref, v_ref, gate_ref, o_ref,
                           vt_scr, ka_scr, acc_scr, sa_scr, sb_scr,
                           *, lam_init, tq, heads_per_step):
    i = pl.program_id(2)
    heads = range(heads_per_step)
    lane = lax.broadcasted_iota(jnp.int32, (tq, HEAD_DIM), 1)
    own_half = (lane < DIFF_HALF, lane >= DIFF_HALF)
    bias_lane0 = (DIFF_HALF, 0)

    for g in heads:
        _stage_values_transposed(v_ref[:, _head_lanes(g)], vt_scr, g, i)
        k = k_ref[:, _head_lanes(g)].astype(_F32)
        for c in range(2):
            ka_scr[g, c, i] = jnp.where(
                own_half[c], k, _bias_key_lanes(lane, bias_lane0[c])).astype(_BF16)

    key_idx = lax.broadcasted_iota(jnp.int32, (tq, tq), 0)
    qry_idx = lax.broadcasted_iota(jnp.int32, (tq, tq), 1)
    qry_pos = lax.broadcasted_iota(jnp.int32, (1, tq), 1).astype(_F32)

    streams = []
    for g in heads:
        slope = slopes_ref[pl.program_id(1) * heads_per_step + g] * _LOG2E
        q = q_ref[:, _head_lanes(g)].astype(_F32)
        for c in range(2):
            qc = jnp.where(own_half[c], q, _bias_query_lanes(slope, lane, bias_lane0[c]))
            streams.append((g, c, slope, qc.astype(_BF16)))

    def scores(n, j, s_buf):
        g, c, _, qc = streams[n]
        s = _dot_nt(ka_scr[g, c, j], qc)
        s_buf[n] = s
        return jnp.max(s, axis=0, keepdims=True)

    def first_update(n, s_buf):
        g, c, slope, _ = streams[n]
        z = jnp.where(key_idx <= qry_idx, s_buf[n], _NEG_INF)
        return _first_flash_step(z, -slope * qry_pos, vt_scr[g, i], acc_scr.at[g, c])

    def update(n, j, s_buf, tile_max, running_max):
        g, c, slope, _ = streams[n]
        rb = -slope * qry_pos - slope * ((i - j) * tq).astype(_F32)
        return _flash_step(s_buf.at[n], tile_max, rb, vt_scr[g, j], running_max,
                           acc_scr.at[g, c])

    _pipelined_flash(len(streams), i, scores, first_update, update, sa_scr, sb_scr)

    lam = (jnp.exp(jnp.sum(lq1_ref[...] * lk1_ref[...], axis=-1, keepdims=True))
           - jnp.exp(jnp.sum(lq2_ref[...] * lk2_ref[...], axis=-1, keepdims=True))
           + lam_init)
    for g in heads:
        o_t = _normalised(acc_scr[g, 0]) - lam * _normalised(acc_scr[g, 1])
        ms = jnp.mean(o_t * o_t, axis=0, keepdims=True)
        o = (o_t * lax.rsqrt(ms + EPS)).T * subg_ref[...] * (1.0 - lam_init)
        gate = gate_ref[:, _head_lanes(g)].astype(_F32)
        o_ref[:, _head_lanes(g)] = (o * _silu(gate)).astype(o_ref.dtype)


def diff_attention(proj, slopes, lq1, lk1, lq2, lk2, subg, *, lam_init, tq, heads_per_step):
    b, s, _ = proj.shape
    hps = heads_per_step
    assert s % tq == 0 and DIFF_HEADS % hps == 0
    width = hps * HEAD_DIM
    vec = lambda n: pl.BlockSpec((1, n), lambda bi, hi, i: (0, 0))
    spec = functools.partial(_head_group_spec, tq=tq, width=width, hps=hps, seq_buffers=None)
    scratch = [pltpu.VMEM((hps, s // tq, _VT_ROWS, tq), _BF16),
               pltpu.VMEM((hps, 2, s // tq, tq, HEAD_DIM), _BF16),
               pltpu.VMEM((hps, 2, _VT_ROWS, tq), _F32),
               pltpu.VMEM((2 * hps, tq, tq), _F32),
               pltpu.VMEM((2 * hps, tq, tq), _F32)]
    return pl.pallas_call(
        functools.partial(_diff_attention_kernel, lam_init=lam_init, tq=tq,
                          heads_per_step=hps),
        grid=(b, DIFF_HEADS // hps, s // tq),
        in_specs=[
            pl.BlockSpec(memory_space=pltpu.SMEM),
            vec(DIFF_HALF), vec(DIFF_HALF), vec(DIFF_HALF), vec(DIFF_HALF), vec(HEAD_DIM),
            spec(tq, first_blk=_DQ_BLK), spec(tq, first_blk=_DK_BLK), spec(tq, first_blk=_DV_BLK),
            spec(tq, first_blk=_GATE_BLK),
        ],
        out_specs=pl.BlockSpec((None, tq, width), lambda bi, hi, i: (bi, i, hi)),
        out_shape=jax.ShapeDtypeStruct((b, s, DIFF_WIDTH), _BF16),
        scratch_shapes=scratch,
        compiler_params=pltpu.CompilerParams(
            dimension_semantics=("parallel", "parallel", "arbitrary"),
            vmem_limit_bytes=_VMEM_LIMIT),
        name="diff_attention",
    )(slopes, lq1.reshape(1, -1), lk1.reshape(1, -1), lq2.reshape(1, -1), lk2.reshape(1, -1),
      subg.reshape(1, -1), proj, proj, proj, proj)


def _moba_attention_kernel(slopes_ref, q_ref, k_ref, v_ref, gate_ref, o_ref,
                           kmean_scr, vt_scr, sel_scr, acc_scr, sa_scr, sb_scr,
                           *, n_blocks, heads_per_step):
    i = pl.program_id(2)
    tq = MOBA_BLOCK
    heads = range(heads_per_step)

    lane = lax.broadcasted_iota(jnp.int32, (tq, HEAD_DIM), 1)

    @pl.when(i == 0)
    def _():
        kmean_scr[...] = jnp.zeros(kmean_scr.shape, _F32)

    for g in heads:
        _stage_values_transposed(v_ref[pl.ds(i * tq, tq), _head_lanes(g)], vt_scr, g, i)
        k = k_ref[pl.ds(i * tq, tq), _head_lanes(g)]
        kmean_scr[g, pl.ds(i, 1), :] = (jnp.sum(k.astype(_F32), axis=0, keepdims=True)
                                        * (1.0 / MOBA_BLOCK))

    key_idx = lax.broadcasted_iota(jnp.int32, (tq, tq), 0)
    qry_idx = lax.broadcasted_iota(jnp.int32, (tq, tq), 1)
    qry_pos = lax.broadcasted_iota(jnp.int32, (1, tq), 1).astype(_F32)
    blk = lax.broadcasted_iota(jnp.int32, (n_blocks, tq), 0)
    past = blk < i

    slopes = [slopes_ref[pl.program_id(1) * heads_per_step + g] * _LOG2E for g in heads]
    qs = [q_ref[:, _head_lanes(g)] for g in heads]
    qa_ts = [jnp.concatenate([qs[g].astype(_F32), _bias_query_lanes(slopes[g], lane, 0)],
                             axis=1).T.astype(_BF16) for g in heads]
    key_pos_lanes = _bias_key_lanes(lane, 0).astype(_BF16)

    def scores(g, j, s_buf):
        ka = jnp.concatenate([k_ref[pl.ds(j * tq, tq), _head_lanes(g)], key_pos_lanes], axis=1)
        s = _dot(ka, qa_ts[g])
        s_buf[g] = s
        return jnp.max(s, axis=0, keepdims=True)

    def select_blocks(g):
        rest = kmean_scr[g]
        pieces = []
        for _ in range(_F32_AS_BF16_PIECES):
            pieces.append(rest.astype(_BF16))
            rest = rest - pieces[-1].astype(_F32)
        parts = _dot_nt(jnp.concatenate(pieces, axis=0), qs[g])
        gate = sum(parts[t * n_blocks:(t + 1) * n_blocks] for t in range(_F32_AS_BF16_PIECES))
        gate = jnp.where(past, gate, _NEG_INF)
        for r0 in range(0, n_blocks, _SUBLANES):
            tile = gate[r0:r0 + _SUBLANES]
            row = lax.broadcasted_iota(jnp.int32, tile.shape, 0) + r0
            rank = jnp.zeros(tile.shape, _F32)
            for n in range(n_blocks):
                g_n = gate[n:n + 1, :]
                if n < r0:
                    beats = g_n >= tile
                elif n >= r0 + _SUBLANES:
                    beats = g_n > tile
                else:
                    beats = (g_n > tile) | ((g_n == tile) & (n < row))
                rank = rank + jnp.where(beats, 1.0, 0.0)
            sel_scr[g, r0:r0 + _SUBLANES, :] = jnp.where(
                (row < i) & (rank < MOBA_TOPK), 0.0, _NEG_INF)

    def first_update(g, s_buf):
        select_blocks(g)
        z = jnp.where(key_idx <= qry_idx, s_buf[g], _NEG_INF)
        return _first_flash_step(z, -slopes[g] * qry_pos, vt_scr[g, i], acc_scr.at[g])

    def update(g, j, s_buf, tile_max, running_max):
        rb = (-slopes[g] * qry_pos - slopes[g] * ((i - j) * tq).astype(_F32)
              + sel_scr[g, pl.ds(j, 1), :])
        return _flash_step(s_buf.at[g], tile_max, rb, vt_scr[g, j], running_max, acc_scr.at[g])

    _pipelined_flash(heads_per_step, i, scores, first_update, update, sa_scr, sb_scr)
    for g in heads:
        o = _normalised(acc_scr[g]).T
        gate = gate_ref[:, _head_lanes(g)].astype(_F32)
        o_ref[:, _head_lanes(g)] = (o * _silu(gate)).astype(o_ref.dtype)


def moba_attention(proj, slopes, *, heads_per_step):
    b, s, _ = proj.shape
    hps = heads_per_step
    assert s % MOBA_BLOCK == 0 and MOBA_HEADS % hps == 0
    n_blocks = s // MOBA_BLOCK
    assert n_blocks % _SUBLANES == 0
    tq = MOBA_BLOCK
    width = hps * HEAD_DIM
    spec = functools.partial(_head_group_spec, tq=tq, width=width, hps=hps, seq_buffers=1)
    return pl.pallas_call(
        functools.partial(_moba_attention_kernel, n_blocks=n_blocks, heads_per_step=hps),
        grid=(b, MOBA_HEADS // hps, n_blocks),
        in_specs=[
            pl.BlockSpec(memory_space=pltpu.SMEM),
            spec(tq, first_blk=_MQ_BLK), spec(s, first_blk=_MK_BLK), spec(s, first_blk=_MV_BLK),
            spec(tq, first_blk=_GATE_BLK + DIFF_HEADS),
        ],
        out_specs=pl.BlockSpec((None, tq, width), lambda bi, hi, i: (bi, i, hi)),
        out_shape=jax.ShapeDtypeStruct((b, s, MOBA_WIDTH), _BF16),
        scratch_shapes=[pltpu.VMEM((hps, n_blocks, HEAD_DIM), _F32),
                        pltpu.VMEM((hps, n_blocks, _VT_ROWS, tq), _BF16),
                        pltpu.VMEM((hps, n_blocks, tq), _F32),
                        pltpu.VMEM((hps, _VT_ROWS, tq), _F32),
                        pltpu.VMEM((hps, tq, tq), _F32),
                        pltpu.VMEM((hps, tq, tq), _F32)],
        compiler_params=pltpu.CompilerParams(
            dimension_semantics=("parallel", "parallel", "arbitrary"),
            vmem_limit_bytes=_VMEM_LIMIT),
        name="moba_attention",
    )(slopes, proj, proj, proj, proj)


def _layer_tail_kernel(yd_ref, ym_ref, x_ref, wa_ref, wb_ref, g_ref, wq_ref, kv_ref, wo_ref,
                       fg_ref, o_ref, q_scr, o_scr, *, apply_final, sub_rows):
    subs = [slice(r, r + sub_rows) for r in range(0, x_ref.shape[0], sub_rows)]
    xs = [x_ref[r, :] + _dot(yd_ref[r, :], wa_ref[...]) + _dot(ym_ref[r, :], wb_ref[...])
          for r in subs]
    for r, x in zip(subs, xs):
        ms = jnp.mean(x * x, axis=-1, keepdims=True)
        h = (x * lax.rsqrt(ms + EPS) * g_ref[...]).astype(_BF16)
        q_scr[r, :] = (_dot(h, wq_ref[...]) * (HEAD_DIM ** -0.5 * _LOG2E)).astype(_BF16)
    for r in subs:
        for hh in range(MEM_HEADS):
            lo, hi = hh * HEAD_DIM, (hh + 1) * HEAD_DIM
            s = _dot_nt(q_scr[r, lo:hi], kv_ref[:, lo:hi])
            p = jnp.exp2(s - jnp.max(s, axis=-1, keepdims=True))
            l = jnp.sum(p, axis=-1, keepdims=True)
            o_h = _dot(p.astype(_BF16), kv_ref[:, MEM_WIDTH + lo:MEM_WIDTH + hi]) / l
            o_scr[r, lo:hi] = o_h.astype(_BF16)
    for r, x in zip(subs, xs):
        x = x + _dot(o_scr[r, :], wo_ref[...])
        if apply_final:
            ms = jnp.mean(x * x, axis=-1, keepdims=True)
            x = x * lax.rsqrt(ms + EPS) * fg_ref[...]
        o_ref[r, :] = x


def layer_tail(yd, ym, x, w_out, g, wq, kv, wo, final_g, *, apply_final, tm, sub_rows):
    b, s, d = x.shape
    mlen = kv.shape[1]
    ka, kb = yd.shape[-1], ym.shape[-1]
    assert s % tm == 0 and ka == kb and w_out.shape[0] == ka + kb
    rows = lambda width: pl.BlockSpec((None, tm, width), lambda bi, i: (bi, i, 0))
    whole = lambda shape: pl.BlockSpec(shape, lambda bi, i: (0,) * len(shape))
    return pl.pallas_call(
        functools.partial(_layer_tail_kernel, apply_final=apply_final, sub_rows=sub_rows),
        grid=(b, s // tm),
        in_specs=[
            rows(ka), rows(kb), rows(d),
            pl.BlockSpec((ka, d), lambda bi, i: (0, 0)),
            pl.BlockSpec((kb, d), lambda bi, i: (1, 0)),
            whole((1, d)), whole((d, MEM_WIDTH)),
            pl.BlockSpec((None, mlen, 2 * MEM_WIDTH), lambda bi, i: (bi, 0, 0)),
            whole((MEM_WIDTH, d)), whole((1, d)),
        ],
        out_specs=rows(d),
        out_shape=jax.ShapeDtypeStruct((b, s, d), _F32),
        scratch_shapes=[pltpu.VMEM((tm, MEM_WIDTH), _BF16),
                        pltpu.VMEM((tm, MEM_WIDTH), _BF16)],
        compiler_params=pltpu.CompilerParams(
            dimension_semantics=("parallel", "parallel"),
            vmem_limit_bytes=_VMEM_LIMIT),
        name="layer_tail",
    )(yd, ym, x, w_out, w_out, g.reshape(1, d), wq, kv, wo, final_g.reshape(1, d))


def _in_proj_col_scale(n_cols):
    cs = np.ones((n_cols,), np.float32)
    cs[_DQ_BLK * HEAD_DIM:_DK_BLK * HEAD_DIM] = DIFF_HALF ** -0.5 * _LOG2E
    cs[_MQ_BLK * HEAD_DIM:_MK_BLK * HEAD_DIM] = HEAD_DIM ** -0.5 * _LOG2E
    return jnp.asarray(cs)


def _alibi_slopes(n):
    return jnp.asarray(2.0 ** (-8.0 * np.arange(1, n + 1) / n), dtype=_F32)


_IN_PROJ_TILE = (1024, 1024)
_ATTN_Q_TILE = 256
_DIFF_HEADS_PER_STEP = 8
_MOBA_HEADS_PER_STEP = 8
_TAIL_ROW_TILE = 512
_TAIL_SUB_ROWS = 256


def kernel(x, mem, norm_mix_g, w_in, lambda_q1, lambda_k1, lambda_q2, lambda_k2, subln_g,
           w_out, norm_mem_g, mem_norm_g, w_q_mem, w_kv_mem, w_o_mem, final_g):
    b, s, d = x.shape
    mlen = mem.shape[1]
    depth = w_in.shape[0]
    diff_slopes = _alibi_slopes(DIFF_HEADS)
    moba_slopes = _alibi_slopes(MOBA_HEADS)
    mem2 = mem.reshape(b * mlen, d)
    for l in range(depth):
        lam_init = 0.8 - 0.6 * math.exp(-0.3 * l)
        proj = rms_matmul(x.reshape(b * s, d), norm_mix_g[l], w_in[l],
                          _in_proj_col_scale(w_in.shape[-1]),
                          tm=_IN_PROJ_TILE[0], tn=_IN_PROJ_TILE[1]).reshape(b, s, -1)
        y_d = diff_attention(proj, diff_slopes, lambda_q1[l], lambda_k1[l], lambda_q2[l],
                             lambda_k2[l], subln_g[l], lam_init=lam_init, tq=_ATTN_Q_TILE,
                             heads_per_step=_DIFF_HEADS_PER_STEP)
        y_m = moba_attention(proj, moba_slopes, heads_per_step=_MOBA_HEADS_PER_STEP)
        kv = rms_matmul(mem2, mem_norm_g[l], w_kv_mem[l],
                        tm=b * mlen, tn=2 * MEM_WIDTH).reshape(b, mlen, 2 * MEM_WIDTH)
        x = layer_tail(y_d, y_m, x, w_out[l].astype(_BF16), norm_mem_g[l],
                       w_q_mem[l].astype(_BF16), kv, w_o_mem[l].astype(_BF16), final_g,
                       apply_final=(l == depth - 1), tm=_TAIL_ROW_TILE,
                       sub_rows=_TAIL_SUB_ROWS)
    return x
```

```python
import functools
import math

import jax
import jax.numpy as jnp
import numpy as np
from jax import lax
from jax.experimental import pallas as pl
from jax.experimental.pallas import tpu as pltpu

HEAD_DIM = 128
DIFF_HEADS = 8
DIFF_HALF = HEAD_DIM // 2
DIFF_WIDTH = DIFF_HEADS * HEAD_DIM
MOBA_HEADS = 8
MOBA_WIDTH = MOBA_HEADS * HEAD_DIM
MOBA_BLOCK = 256
MOBA_TOPK = 3
MEM_HEADS = 4
MEM_WIDTH = MEM_HEADS * HEAD_DIM
EPS = 1e-6

_DQ_BLK = 0
_DK_BLK = DIFF_HEADS
_DV_BLK = 2 * DIFF_HEADS
_MQ_BLK = 3 * DIFF_HEADS
_MK_BLK = 3 * DIFF_HEADS + MOBA_HEADS
_MV_BLK = 3 * DIFF_HEADS + 2 * MOBA_HEADS
_GATE_BLK = 3 * DIFF_HEADS + 3 * MOBA_HEADS

_V7X_VMEM_BYTES = 64 * 1024 * 1024
_VMEM_LIMIT = _V7X_VMEM_BYTES * 3 // 4

_BF16 = jnp.bfloat16
_F32 = jnp.float32
_NEG_INF = float("-inf")


def _dot_nt(a, b):
    return lax.dot_general(a, b, (((1,), (1,)), ((), ())), preferred_element_type=_F32)


def _dot(a, b):
    return jnp.dot(a, b, preferred_element_type=_F32)


def _silu(g):
    return g * jax.nn.sigmoid(g)


_NORM_CHUNK_ROWS = 256

def _rms_matmul_kernel(x_ref, g_ref, w_ref, *rest, has_col_scale, norm_rows):
    cs_ref, o_ref, h_scr = rest if has_col_scale else (None,) + rest

    def project(rows):
        out = _dot(h_scr[rows, :], w_ref[...].astype(_BF16))
        if has_col_scale:
            out = out * cs_ref[...]
        o_ref[rows, :] = out.astype(o_ref.dtype)

    @pl.when(pl.program_id(1) == 0)
    def _():
        for r0 in range(0, x_ref.shape[0], norm_rows):
            rows = slice(r0, r0 + norm_rows)
            x = x_ref[rows, :]
            ms = jnp.mean(x * x, axis=-1, keepdims=True)
            h_scr[rows, :] = (x * lax.rsqrt(ms + EPS) * g_ref[...]).astype(_BF16)
            project(rows)

    @pl.when(pl.program_id(1) != 0)
    def _():
        project(slice(None))


def rms_matmul(x, g, w, col_scale=None, *, tm, tn, out_dtype=_BF16):
    m, k = x.shape
    n = w.shape[1]
    assert m % tm == 0 and n % tn == 0
    has_cs = col_scale is not None
    cs_specs = [pl.BlockSpec((1, tn), lambda i, j: (0, j))] if has_cs else []
    cs_args = [col_scale.reshape(1, n)] if has_cs else []
    norm_rows = min(tm, _NORM_CHUNK_ROWS)
    assert tm % norm_rows == 0
    return pl.pallas_call(
        functools.partial(_rms_matmul_kernel, has_col_scale=has_cs, norm_rows=norm_rows),
        grid=(m // tm, n // tn),
        in_specs=[
            pl.BlockSpec((tm, k), lambda i, j: (i, 0)),
            pl.BlockSpec((1, k), lambda i, j: (0, 0)),
            pl.BlockSpec((k, tn), lambda i, j: (0, j)),
        ] + cs_specs,
        out_specs=pl.BlockSpec((tm, tn), lambda i, j: (i, j)),
        out_shape=jax.ShapeDtypeStruct((m, n), out_dtype),
        scratch_shapes=[pltpu.VMEM((tm, k), _BF16)],
        compiler_params=pltpu.CompilerParams(
            dimension_semantics=("parallel", "arbitrary"),
            vmem_limit_bytes=_VMEM_LIMIT),
        name="rms_matmul",
    )(x, g.reshape(1, k), w, *cs_args)


_LOG2E = math.log2(math.e)
_ONES_ROWS = 16
_VT_ROWS = HEAD_DIM + _ONES_ROWS


def _first_flash_step(z, rb, vt, acc_ref):
    m = jnp.max(z, axis=0, keepdims=True) + rb
    p = jnp.exp2(z - (m - rb))
    acc_ref[...] = _dot(vt, p.astype(_BF16))
    return m


def _flash_step(z_ref, z_max, rb, vt, m, acc_ref):
    m_new = jnp.maximum(m, z_max + rb)
    p = jnp.exp2(z_ref[...] - (m_new - rb))
    acc_ref[...] = jnp.exp2(m - m_new) * acc_ref[...] + _dot(vt, p.astype(_BF16))
    return m_new


_F32_AS_BF16_PIECES = 3
_BIAS_LANES = _F32_AS_BF16_PIECES
_SUBLANES = 8


def _bias_key_lanes(lane, lane0):
    pos = lax.broadcasted_iota(jnp.int32, lane.shape, 0).astype(_F32)
    return jnp.where((lane >= lane0) & (lane < lane0 + _BIAS_LANES), pos, 0.0)


def _bias_query_lanes(slope, lane, lane0):
    rest = jnp.full(lane.shape, slope, _F32)
    out = jnp.zeros(lane.shape, _F32)
    for t in range(_BIAS_LANES):
        piece = rest.astype(_BF16).astype(_F32)
        out = jnp.where(lane == lane0 + t, piece, out)
        rest = rest - piece
    return out


def _stage_values_transposed(v_tile, vt_scr, g, n):
    vt_scr[g, n, :HEAD_DIM, :] = v_tile.T
    vt_scr[g, n, HEAD_DIM:, :] = jnp.ones((_ONES_ROWS, v_tile.shape[0]), _BF16)


def _normalised(acc):
    return acc[:HEAD_DIM] / acc[HEAD_DIM:HEAD_DIM + 1]


def _pipelined_flash(n_streams, n_past, scores, first_update, update, sa_scr, sb_scr):
    streams = range(n_streams)
    for n in streams:
        scores(n, n_past, sb_scr)
    max_a = tuple(scores(n, 0, sa_scr) for n in streams)
    running = tuple(first_update(n, sb_scr) for n in streams)

    def pair(t, carry):
        running, max_a = (list(c) for c in carry)
        j0 = 2 * t
        max_b = [None] * n_streams
        for n in streams:
            max_b[n] = scores(n, j0 + 1, sb_scr)
            running[n] = update(n, j0, sa_scr, max_a[n], running[n])
        for n in streams:
            max_a[n] = scores(n, j0 + 2, sa_scr)
            running[n] = update(n, j0 + 1, sb_scr, max_b[n], running[n])
        return tuple(running), tuple(max_a)

    def two_pairs(t, carry):
        return pair(2 * t + 1, pair(2 * t, carry))

    n_pairs = n_past // 2
    carry = lax.fori_loop(0, n_pairs // 2, two_pairs, (running, max_a))
    running, max_a = lax.fori_loop(n_pairs - n_pairs % 2, n_pairs, pair, carry)

    @pl.when(n_past % 2 == 1)
    def _():
        for n in streams:
            update(n, n_past - 1, sa_scr, max_a[n], running[n])


def _head_lanes(g):
    return slice(g * HEAD_DIM, (g + 1) * HEAD_DIM)


def _head_group_spec(tq, width, first_blk, hps):
    assert first_blk % hps == 0
    return pl.BlockSpec((None, tq, width),
                        lambda bi, hi, i: (bi, i, first_blk // hps + hi))


def _diff_attention_kernel(slopes_ref, lq1_ref, lk1_ref, lq2_ref, lk2_ref, subg_ref,
                           q_ref, k_ref, v_ref, gate_ref, o_ref,
                           vt_scr, ka_scr, acc_scr, sa_scr, sb_scr,
                           *, lam_init, tq, heads_per_step):
    i = pl.program_id(2)
    heads = range(heads_per_step)
    lane = lax.broadcasted_iota(jnp.int32, (tq, HEAD_DIM), 1)
    own_half = (lane < DIFF_HALF, lane >= DIFF_HALF)
    bias_lane0 = (DIFF_HALF, 0)

    for g in heads:
        _stage_values_transposed(v_ref[:, _head_lanes(g)], vt_scr, g, i)
        k = k_ref[:, _head_lanes(g)].astype(_F32)
        for c in range(2):
            ka_scr[g, c, i] = jnp.where(
                own_half[c], k, _bias_key_lanes(lane, bias_lane0[c])).astype(_BF16)

    key_idx = lax.broadcasted_iota(jnp.int32, (tq, tq), 0)
    qry_idx = lax.broadcasted_iota(jnp.int32, (tq, tq), 1)
    qry_pos = lax.broadcasted_iota(jnp.int32, (1, tq), 1).astype(_F32)

    streams = []
    for g in heads:
        slope = slopes_ref[pl.program_id(1) * heads_per_step + g] * _LOG2E
        q = q_ref[:, _head_lanes(g)].astype(_F32)
        for c in range(2):
            qc = jnp.where(own_half[c], q, _bias_query_lanes(slope, lane, bias_lane0[c]))
            streams.append((g, c, slope, qc.astype(_BF16)))

    def scores(n, j, s_buf):
        g, c, _, qc = streams[n]
        s = _dot_nt(ka_scr[g, c, j], qc)
        s_buf[n] = s
        return jnp.max(s, axis=0, keepdims=True)

    def first_update(n, s_buf):
        g, c, slope, _ = streams[n]
        z = jnp.where(key_idx <= qry_idx, s_buf[n], _NEG_INF)
        return _first_flash_step(z, -slope * qry_pos, vt_scr[g, i], acc_scr.at[g, c])

    def update(n, j, s_buf, tile_max, running_max):
        g, c, slope, _ = streams[n]
        rb = -slope * qry_pos - slope * ((i - j) * tq).astype(_F32)
        return _flash_step(s_buf.at[n], tile_max, rb, vt_scr[g, j], running_max,
                           acc_scr.at[g, c])

    _pipelined_flash(len(streams), i, scores, first_update, update, sa_scr, sb_scr)

    lam = (jnp.exp(jnp.sum(lq1_ref[...] * lk1_ref[...], axis=-1, keepdims=True))
           - jnp.exp(jnp.sum(lq2_ref[...] * lk2_ref[...], axis=-1, keepdims=True))
           + lam_init)
    for g in heads:
        o_t = _normalised(acc_scr[g, 0]) - lam * _normalised(acc_scr[g, 1])
        ms = jnp.mean(o_t * o_t, axis=0, keepdims=True)
        o = (o_t * lax.rsqrt(ms + EPS)).T * subg_ref[...] * (1.0 - lam_init)
        gate = gate_ref[:, _head_lanes(g)].astype(_F32)
        o_ref[:, _head_lanes(g)] = (o * _silu(gate)).astype(o_ref.dtype)


def diff_attention(proj, slopes, lq1, lk1, lq2, lk2, subg, *, lam_init, tq, heads_per_step):
    b, s, _ = proj.shape
    hps = heads_per_step
    assert s % tq == 0 and DIFF_HEADS % hps == 0
    width = hps * HEAD_DIM
    vec = lambda n: pl.BlockSpec((1, n), lambda bi, hi, i: (0, 0))
    spec = functools.partial(_head_group_spec, tq, width, hps=hps)
    scratch = [pltpu.VMEM((hps, s // tq, _VT_ROWS, tq), _BF16),
               pltpu.VMEM((hps, 2, s // tq, tq, HEAD_DIM), _BF16),
               pltpu.VMEM((hps, 2, _VT_ROWS, tq), _F32),
               pltpu.VMEM((2 * hps, tq, tq), _F32),
               pltpu.VMEM((2 * hps, tq, tq), _F32)]
    return pl.pallas_call(
        functools.partial(_diff_attention_kernel, lam_init=lam_init, tq=tq,
                          heads_per_step=hps),
        grid=(b, DIFF_HEADS // hps, s // tq),
        in_specs=[
            pl.BlockSpec(memory_space=pltpu.SMEM),
            vec(DIFF_HALF), vec(DIFF_HALF), vec(DIFF_HALF), vec(DIFF_HALF), vec(HEAD_DIM),
            spec(first_blk=_DQ_BLK), spec(first_blk=_DK_BLK), spec(first_blk=_DV_BLK),
            spec(first_blk=_GATE_BLK),
        ],
        out_specs=pl.BlockSpec((None, tq, width), lambda bi, hi, i: (bi, i, hi)),
        out_shape=jax.ShapeDtypeStruct((b, s, DIFF_WIDTH), _BF16),
        scratch_shapes=scratch,
        compiler_params=pltpu.CompilerParams(
            dimension_semantics=("parallel", "parallel", "arbitrary"),
            vmem_limit_bytes=_VMEM_LIMIT),
        name="diff_attention",
    )(slopes, lq1.reshape(1, -1), lk1.reshape(1, -1), lq2.reshape(1, -1), lk2.reshape(1, -1),
      subg.reshape(1, -1), proj, proj, proj, proj)


def _moba_attention_kernel(slopes_ref, q_ref, k_ref, v_ref, gate_ref, o_ref,
                           kmean_scr, k_scr, vt_scr, sel_scr, acc_scr, sa_scr, sb_scr,
                           *, n_blocks, heads_per_step):
    i = pl.program_id(2)
    tq = MOBA_BLOCK
    heads = range(heads_per_step)

    lane = lax.broadcasted_iota(jnp.int32, (tq, HEAD_DIM), 1)

    @pl.when(i == 0)
    def _():
        kmean_scr[...] = jnp.zeros(kmean_scr.shape, _F32)

    for g in heads:
        _stage_values_transposed(v_ref[:, _head_lanes(g)], vt_scr, g, i)
        k = k_ref[:, _head_lanes(g)]
        k_scr[g, i] = k
        kmean_scr[g, pl.ds(i, 1), :] = (jnp.sum(k.astype(_F32), axis=0, keepdims=True)
                                        * (1.0 / MOBA_BLOCK))

    key_idx = lax.broadcasted_iota(jnp.int32, (tq, tq), 0)
    qry_idx = lax.broadcasted_iota(jnp.int32, (tq, tq), 1)
    qry_pos = lax.broadcasted_iota(jnp.int32, (1, tq), 1).astype(_F32)
    blk = lax.broadcasted_iota(jnp.int32, (n_blocks, tq), 0)
    past = blk < i

    slopes = [slopes_ref[pl.program_id(1) * heads_per_step + g] * _LOG2E for g in heads]
    qs = [q_ref[:, _head_lanes(g)] for g in heads]
    qa_ts = [jnp.concatenate([qs[g].astype(_F32), _bias_query_lanes(slopes[g], lane, 0)],
                             axis=1).T.astype(_BF16) for g in heads]
    key_pos_lanes = _bias_key_lanes(lane, 0).astype(_BF16)

    def scores(g, j, s_buf):
        ka = jnp.concatenate([k_scr[g, j], key_pos_lanes], axis=1)
        s = _dot(ka, qa_ts[g])
        s_buf[g] = s
        return jnp.max(s, axis=0, keepdims=True)

    def select_blocks(g):
        rest = kmean_scr[g]
        pieces = []
        for _ in range(_F32_AS_BF16_PIECES):
            pieces.append(rest.astype(_BF16))
            rest = rest - pieces[-1].astype(_F32)
        parts = _dot_nt(jnp.concatenate(pieces, axis=0), qs[g])
        gate = sum(parts[t * n_blocks:(t + 1) * n_blocks] for t in range(_F32_AS_BF16_PIECES))
        gate = jnp.where(past, gate, _NEG_INF)
        for r0 in range(0, n_blocks, _SUBLANES):
            tile = gate[r0:r0 + _SUBLANES]
            row = lax.broadcasted_iota(jnp.int32, tile.shape, 0) + r0
            rank = jnp.zeros(tile.shape, _F32)
            for n in range(n_blocks):
                g_n = gate[n:n + 1, :]
                if n < r0:
                    beats = g_n >= tile
                elif n >= r0 + _SUBLANES:
                    beats = g_n > tile
                else:
                    beats = (g_n > tile) | ((g_n == tile) & (n < row))
                rank = rank + jnp.where(beats, 1.0, 0.0)
            sel_scr[g, r0:r0 + _SUBLANES, :] = jnp.where(
                (row < i) & (rank < MOBA_TOPK), 0.0, _NEG_INF)

    def first_update(g, s_buf):
        select_blocks(g)
        z = jnp.where(key_idx <= qry_idx, s_buf[g], _NEG_INF)
        return _first_flash_step(z, -slopes[g] * qry_pos, vt_scr[g, i], acc_scr.at[g])

    def update(g, j, s_buf, tile_max, running_max):
        rb = (-slopes[g] * qry_pos - slopes[g] * ((i - j) * tq).astype(_F32)
              + sel_scr[g, pl.ds(j, 1), :])
        return _flash_step(s_buf.at[g], tile_max, rb, vt_scr[g, j], running_max, acc_scr.at[g])

    _pipelined_flash(heads_per_step, i, scores, first_update, update, sa_scr, sb_scr)
    for g in heads:
        o = _normalised(acc_scr[g]).T
        gate = gate_ref[:, _head_lanes(g)].astype(_F32)
        o_ref[:, _head_lanes(g)] = (o * _silu(gate)).astype(o_ref.dtype)


def moba_attention(proj, slopes, *, heads_per_step):
    b, s, _ = proj.shape
    hps = heads_per_step
    assert s % MOBA_BLOCK == 0 and MOBA_HEADS % hps == 0
    n_blocks = s // MOBA_BLOCK
    assert n_blocks % _SUBLANES == 0
    tq = MOBA_BLOCK
    width = hps * HEAD_DIM
    spec = functools.partial(_head_group_spec, tq, width, hps=hps)
    return pl.pallas_call(
        functools.partial(_moba_attention_kernel, n_blocks=n_blocks, heads_per_step=hps),
        grid=(b, MOBA_HEADS // hps, n_blocks),
        in_specs=[
            pl.BlockSpec(memory_space=pltpu.SMEM),
            spec(first_blk=_MQ_BLK), spec(first_blk=_MK_BLK), spec(first_blk=_MV_BLK),
            spec(first_blk=_GATE_BLK + DIFF_HEADS),
        ],
        out_specs=pl.BlockSpec((None, tq, width), lambda bi, hi, i: (bi, i, hi)),
        out_shape=jax.ShapeDtypeStruct((b, s, MOBA_WIDTH), _BF16),
        scratch_shapes=[pltpu.VMEM((hps, n_blocks, HEAD_DIM), _F32),
                        pltpu.VMEM((hps, n_blocks, tq, HEAD_DIM), _BF16),
                        pltpu.VMEM((hps, n_blocks, _VT_ROWS, tq), _BF16),
                        pltpu.VMEM((hps, n_blocks, tq), _F32),
                        pltpu.VMEM((hps, _VT_ROWS, tq), _F32),
                        pltpu.VMEM((hps, tq, tq), _F32),
                        pltpu.VMEM((hps, tq, tq), _F32)],
        compiler_params=pltpu.CompilerParams(
            dimension_semantics=("parallel", "parallel", "arbitrary"),
            vmem_limit_bytes=_VMEM_LIMIT),
        name="moba_attention",
    )(slopes, proj, proj, proj, proj)


def _layer_tail_kernel(yd_ref, ym_ref, x_ref, wa_ref, wb_ref, g_ref, wq_ref, kv_ref, wo_ref,
                       fg_ref, o_ref, q_scr, o_scr, *, apply_final, sub_rows):
    subs = [slice(r, r + sub_rows) for r in range(0, x_ref.shape[0], sub_rows)]
    xs = [x_ref[r, :] + _dot(yd_ref[r, :], wa_ref[...]) + _dot(ym_ref[r, :], wb_ref[...])
          for r in subs]
    for r, x in zip(subs, xs):
        ms = jnp.mean(x * x, axis=-1, keepdims=True)
        h = (x * lax.rsqrt(ms + EPS) * g_ref[...]).astype(_BF16)
        q_scr[r, :] = (_dot(h, wq_ref[...]) * (HEAD_DIM ** -0.5 * _LOG2E)).astype(_BF16)
    for r in subs:
        for hh in range(MEM_HEADS):
            lo, hi = hh * HEAD_DIM, (hh + 1) * HEAD_DIM
            s = _dot_nt(q_scr[r, lo:hi], kv_ref[:, lo:hi])
            p = jnp.exp2(s - jnp.max(s, axis=-1, keepdims=True))
            l = jnp.sum(p, axis=-1, keepdims=True)
            o_h = _dot(p.astype(_BF16), kv_ref[:, MEM_WIDTH + lo:MEM_WIDTH + hi]) / l
            o_scr[r, lo:hi] = o_h.astype(_BF16)
    for r, x in zip(subs, xs):
        x = x + _dot(o_scr[r, :], wo_ref[...])
        if apply_final:
            ms = jnp.mean(x * x, axis=-1, keepdims=True)
            x = x * lax.rsqrt(ms + EPS) * fg_ref[...]
        o_ref[r, :] = x


def layer_tail(yd, ym, x, w_out, g, wq, kv, wo, final_g, *, apply_final, tm, sub_rows):
    b, s, d = x.shape
    mlen = kv.shape[1]
    ka, kb = yd.shape[-1], ym.shape[-1]
    assert s % tm == 0 and ka == kb and w_out.shape[0] == ka + kb
    rows = lambda width: pl.BlockSpec((None, tm, width), lambda bi, i: (bi, i, 0))
    whole = lambda shape: pl.BlockSpec(shape, lambda bi, i: (0,) * len(shape))
    return pl.pallas_call(
        functools.partial(_layer_tail_kernel, apply_final=apply_final, sub_rows=sub_rows),
        grid=(b, s // tm),
        in_specs=[
            rows(ka), rows(kb), rows(d),
            pl.BlockSpec((ka, d), lambda bi, i: (0, 0)),
            pl.BlockSpec((kb, d), lambda bi, i: (1, 0)),
            whole((1, d)), whole((d, MEM_WIDTH)),
            pl.BlockSpec((None, mlen, 2 * MEM_WIDTH), lambda bi, i: (bi, 0, 0)),
            whole((MEM_WIDTH, d)), whole((1, d)),
        ],
        out_specs=rows(d),
        out_shape=jax.ShapeDtypeStruct((b, s, d), _F32),
        scratch_shapes=[pltpu.VMEM((tm, MEM_WIDTH), _BF16),
                        pltpu.VMEM((tm, MEM_WIDTH), _BF16)],
        compiler_params=pltpu.CompilerParams(
            dimension_semantics=("parallel", "parallel"),
            vmem_limit_bytes=_VMEM_LIMIT),
        name="layer_tail",
    )(yd, ym, x, w_out, w_out, g.reshape(1, d), wq, kv, wo, final_g.reshape(1, d))


def _in_proj_col_scale(n_cols):
    cs = np.ones((n_cols,), np.float32)
    cs[_DQ_BLK * HEAD_DIM:_DK_BLK * HEAD_DIM] = DIFF_HALF ** -0.5 * _LOG2E
    cs[_MQ_BLK * HEAD_DIM:_MK_BLK * HEAD_DIM] = HEAD_DIM ** -0.5 * _LOG2E
    return jnp.asarray(cs)


def _alibi_slopes(n):
    return jnp.asarray(2.0 ** (-8.0 * np.arange(1, n + 1) / n), dtype=_F32)


_IN_PROJ_TILE = (1024, 1024)
_ATTN_Q_TILE = 256
_DIFF_HEADS_PER_STEP = 8
_MOBA_HEADS_PER_STEP = 8
_TAIL_ROW_TILE = 512
_TAIL_SUB_ROWS = 256


def kernel(x, mem, norm_mix_g, w_in, lambda_q1, lambda_k1, lambda_q2, lambda_k2, subln_g,
           w_out, norm_mem_g, mem_norm_g, w_q_mem, w_kv_mem, w_o_mem, final_g):
    b, s, d = x.shape
    mlen = mem.shape[1]
    depth = w_in.shape[0]
    diff_slopes = _alibi_slopes(DIFF_HEADS)
    moba_slopes = _alibi_slopes(MOBA_HEADS)
    mem2 = mem.reshape(b * mlen, d)
    for l in range(depth):
        lam_init = 0.8 - 0.6 * math.exp(-0.3 * l)
        proj = rms_matmul(x.reshape(b * s, d), norm_mix_g[l], w_in[l],
                          _in_proj_col_scale(w_in.shape[-1]),
                          tm=_IN_PROJ_TILE[0], tn=_IN_PROJ_TILE[1]).reshape(b, s, -1)
        y_d = diff_attention(proj, diff_slopes, lambda_q1[l], lambda_k1[l], lambda_q2[l],
                             lambda_k2[l], subln_g[l], lam_init=lam_init, tq=_ATTN_Q_TILE,
                             heads_per_step=_DIFF_HEADS_PER_STEP)
        y_m = moba_attention(proj, moba_slopes, heads_per_step=_MOBA_HEADS_PER_STEP)
        kv = rms_matmul(mem2, mem_norm_g[l], w_kv_mem[l],
                        tm=b * mlen, tn=2 * MEM_WIDTH).reshape(b, mlen, 2 * MEM_WIDTH)
        x = layer_tail(y_d, y_m, x, w_out[l].astype(_BF16), norm_mem_g[l],
                       w_q_mem[l].astype(_BF16), kv, w_o_mem[l].astype(_BF16), final_g,
                       apply_final=(l == depth - 1), tm=_TAIL_ROW_TILE,
                       sub_rows=_TAIL_SUB_ROWS)
    return x
```

```python
import functools
import math

import jax
import jax.numpy as jnp
import numpy as np
from jax import lax
from jax.experimental import pallas as pl
from jax.experimental.pallas import tpu as pltpu

HEAD_DIM = 128
DIFF_HEADS = 8
DIFF_HALF = HEAD_DIM // 2
DIFF_WIDTH = DIFF_HEADS * HEAD_DIM
MOBA_HEADS = 8
MOBA_WIDTH = MOBA_HEADS * HEAD_DIM
MOBA_BLOCK = 256
MOBA_TOPK = 3
MEM_HEADS = 4
MEM_WIDTH = MEM_HEADS * HEAD_DIM
EPS = 1e-6

_DQ_BLK = 0
_DK_BLK = DIFF_HEADS
_DV_BLK = 2 * DIFF_HEADS
_MQ_BLK = 3 * DIFF_HEADS
_MK_BLK = 3 * DIFF_HEADS + MOBA_HEADS
_MV_BLK = 3 * DIFF_HEADS + 2 * MOBA_HEADS
_GATE_BLK = 3 * DIFF_HEADS + 3 * MOBA_HEADS

_V7X_VMEM_BYTES = 64 * 1024 * 1024
_VMEM_LIMIT = _V7X_VMEM_BYTES * 3 // 4

_BF16 = jnp.bfloat16
_F32 = jnp.float32
_NEG_INF = float("-inf")


def _dot_nt(a, b):
    return lax.dot_general(a, b, (((1,), (1,)), ((), ())), preferred_element_type=_F32)


def _dot(a, b):
    return jnp.dot(a, b, preferred_element_type=_F32)


def _silu(g):
    return g * jax.nn.sigmoid(g)


_NORM_CHUNK_ROWS = 256

def _rms_matmul_kernel(x_ref, g_ref, w_ref, *rest, has_col_scale, norm_rows):
    cs_ref, o_ref, h_scr = rest if has_col_scale else (None,) + rest

    def project(rows):
        out = _dot(h_scr[rows, :], w_ref[...].astype(_BF16))
        if has_col_scale:
            out = out * cs_ref[...]
        o_ref[rows, :] = out.astype(o_ref.dtype)

    @pl.when(pl.program_id(1) == 0)
    def _():
        for r0 in range(0, x_ref.shape[0], norm_rows):
            rows = slice(r0, r0 + norm_rows)
            x = x_ref[rows, :]
            ms = jnp.mean(x * x, axis=-1, keepdims=True)
            h_scr[rows, :] = (x * lax.rsqrt(ms + EPS) * g_ref[...]).astype(_BF16)
            project(rows)

    @pl.when(pl.program_id(1) != 0)
    def _():
        project(slice(None))


def rms_matmul(x, g, w, col_scale=None, *, tm, tn, out_dtype=_BF16):
    m, k = x.shape
    n = w.shape[1]
    assert m % tm == 0 and n % tn == 0
    has_cs = col_scale is not None
    cs_specs = [pl.BlockSpec((1, tn), lambda i, j: (0, j))] if has_cs else []
    cs_args = [col_scale.reshape(1, n)] if has_cs else []
    norm_rows = min(tm, _NORM_CHUNK_ROWS)
    assert tm % norm_rows == 0
    return pl.pallas_call(
        functools.partial(_rms_matmul_kernel, has_col_scale=has_cs, norm_rows=norm_rows),
        grid=(m // tm, n // tn),
        in_specs=[
            pl.BlockSpec((tm, k), lambda i, j: (i, 0)),
            pl.BlockSpec((1, k), lambda i, j: (0, 0)),
            pl.BlockSpec((k, tn), lambda i, j: (0, j)),
        ] + cs_specs,
        out_specs=pl.BlockSpec((tm, tn), lambda i, j: (i, j)),
        out_shape=jax.ShapeDtypeStruct((m, n), out_dtype),
        scratch_shapes=[pltpu.VMEM((tm, k), _BF16)],
        compiler_params=pltpu.CompilerParams(
            dimension_semantics=("parallel", "arbitrary"),
            vmem_limit_bytes=_VMEM_LIMIT),
        name="rms_matmul",
    )(x, g.reshape(1, k), w, *cs_args)


_LOG2E = math.log2(math.e)
_ONES_ROWS = 16
_VT_ROWS = HEAD_DIM + _ONES_ROWS


def _first_flash_step(z, rb, vt, acc_ref):
    m = jnp.max(z, axis=0, keepdims=True) + rb
    p = jnp.exp2(z - (m - rb))
    acc_ref[...] = _dot(vt, p.astype(_BF16))
    return m


def _flash_step(z_ref, z_max, rb, vt, m, acc_ref):
    m_new = jnp.maximum(m, z_max + rb)
    p = jnp.exp2(z_ref[...] - (m_new - rb))
    acc_ref[...] = jnp.exp2(m - m_new) * acc_ref[...] + _dot(vt, p.astype(_BF16))
    return m_new


_F32_AS_BF16_PIECES = 3
_BIAS_LANES = _F32_AS_BF16_PIECES
_SUBLANES = 8


def _bias_key_lanes(lane, lane0):
    pos = lax.broadcasted_iota(jnp.int32, lane.shape, 0).astype(_F32)
    return jnp.where((lane >= lane0) & (lane < lane0 + _BIAS_LANES), pos, 0.0)


def _bias_query_lanes(slope, lane, lane0):
    rest = jnp.full(lane.shape, slope, _F32)
    out = jnp.zeros(lane.shape, _F32)
    for t in range(_BIAS_LANES):
        piece = rest.astype(_BF16).astype(_F32)
        out = jnp.where(lane == lane0 + t, piece, out)
        rest = rest - piece
    return out


def _stage_values_transposed(v_tile, vt_scr, g, n):
    vt_scr[g, n, :HEAD_DIM, :] = v_tile.T
    vt_scr[g, n, HEAD_DIM:, :] = jnp.ones((_ONES_ROWS, v_tile.shape[0]), _BF16)


def _normalised(acc):
    return acc[:HEAD_DIM] / acc[HEAD_DIM:HEAD_DIM + 1]


def _pipelined_flash(n_streams, n_past, scores, first_update, update, sa_scr, sb_scr):
    streams = range(n_streams)
    for n in streams:
        scores(n, n_past, sb_scr)
    max_a = tuple(scores(n, 0, sa_scr) for n in streams)
    running = tuple(first_update(n, sb_scr) for n in streams)

    def pair(t, carry):
        running, max_a = (list(c) for c in carry)
        j0 = 2 * t
        max_b = [None] * n_streams
        for n in streams:
            max_b[n] = scores(n, j0 + 1, sb_scr)
            running[n] = update(n, j0, sa_scr, max_a[n], running[n])
        for n in streams:
            max_a[n] = scores(n, j0 + 2, sa_scr)
            running[n] = update(n, j0 + 1, sb_scr, max_b[n], running[n])
        return tuple(running), tuple(max_a)

    def two_pairs(t, carry):
        return pair(2 * t + 1, pair(2 * t, carry))

    n_pairs = n_past // 2
    carry = lax.fori_loop(0, n_pairs // 2, two_pairs, (running, max_a))
    running, max_a = lax.fori_loop(n_pairs - n_pairs % 2, n_pairs, pair, carry)

    @pl.when(n_past % 2 == 1)
    def _():
        for n in streams:
            update(n, n_past - 1, sa_scr, max_a[n], running[n])


def _head_lanes(g):
    return slice(g * HEAD_DIM, (g + 1) * HEAD_DIM)


def _head_group_spec(rows, width, first_blk, hps):
    assert first_blk % hps == 0
    return pl.BlockSpec((None, rows, width),
                        lambda bi, hi, i: (bi, i, first_blk // hps + hi))


def _diff_attention_kernel(slopes_ref, lq1_ref, lk1_ref, lq2_ref, lk2_ref, subg_ref,
                           q_ref, k_ref, v_ref, gate_ref, o_ref,
                           vt_scr, ka_scr, acc_scr, sa_scr, sb_scr,
                           *, lam_init, tq, heads_per_step):
    i = pl.program_id(2)
    heads = range(heads_per_step)
    lane = lax.broadcasted_iota(jnp.int32, (tq, HEAD_DIM), 1)
    own_half = (lane < DIFF_HALF, lane >= DIFF_HALF)
    bias_lane0 = (DIFF_HALF, 0)

    for g in heads:
        _stage_values_transposed(v_ref[:, _head_lanes(g)], vt_scr, g, i)
        k = k_ref[:, _head_lanes(g)].astype(_F32)
        for c in range(2):
            ka_scr[g, c, i] = jnp.where(
                own_half[c], k, _bias_key_lanes(lane, bias_lane0[c])).astype(_BF16)

    key_idx = lax.broadcasted_iota(jnp.int32, (tq, tq), 0)
    qry_idx = lax.broadcasted_iota(jnp.int32, (tq, tq), 1)
    qry_pos = lax.broadcasted_iota(jnp.int32, (1, tq), 1).astype(_F32)

    streams = []
    for g in heads:
        slope = slopes_ref[pl.program_id(1) * heads_per_step + g] * _LOG2E
        q = q_ref[:, _head_lanes(g)].astype(_F32)
        for c in range(2):
            qc = jnp.where(own_half[c], q, _bias_query_lanes(slope, lane, bias_lane0[c]))
            streams.append((g, c, slope, qc.astype(_BF16)))

    def scores(n, j, s_buf):
        g, c, _, qc = streams[n]
        s = _dot_nt(ka_scr[g, c, j], qc)
        s_buf[n] = s
        return jnp.max(s, axis=0, keepdims=True)

    def first_update(n, s_buf):
        g, c, slope, _ = streams[n]
        z = jnp.where(key_idx <= qry_idx, s_buf[n], _NEG_INF)
        return _first_flash_step(z, -slope * qry_pos, vt_scr[g, i], acc_scr.at[g, c])

    def update(n, j, s_buf, tile_max, running_max):
        g, c, slope, _ = streams[n]
        rb = -slope * qry_pos - slope * ((i - j) * tq).astype(_F32)
        return _flash_step(s_buf.at[n], tile_max, rb, vt_scr[g, j], running_max,
                           acc_scr.at[g, c])

    _pipelined_flash(len(streams), i, scores, first_update, update, sa_scr, sb_scr)

    lam = (jnp.exp(jnp.sum(lq1_ref[...] * lk1_ref[...], axis=-1, keepdims=True))
           - jnp.exp(jnp.sum(lq2_ref[...] * lk2_ref[...], axis=-1, keepdims=True))
           + lam_init)
    for g in heads:
        o_t = _normalised(acc_scr[g, 0]) - lam * _normalised(acc_scr[g, 1])
        ms = jnp.mean(o_t * o_t, axis=0, keepdims=True)
        o = (o_t * lax.rsqrt(ms + EPS)).T * subg_ref[...] * (1.0 - lam_init)
        gate = gate_ref[:, _head_lanes(g)].astype(_F32)
        o_ref[:, _head_lanes(g)] = (o * _silu(gate)).astype(o_ref.dtype)


def diff_attention(proj, slopes, lq1, lk1, lq2, lk2, subg, *, lam_init, tq, heads_per_step):
    b, s, _ = proj.shape
    hps = heads_per_step
    assert s % tq == 0 and DIFF_HEADS % hps == 0
    width = hps * HEAD_DIM
    vec = lambda n: pl.BlockSpec((1, n), lambda bi, hi, i: (0, 0))
    spec = functools.partial(_head_group_spec, tq, width, hps=hps)
    scratch = [pltpu.VMEM((hps, s // tq, _VT_ROWS, tq), _BF16),
               pltpu.VMEM((hps, 2, s // tq, tq, HEAD_DIM), _BF16),
               pltpu.VMEM((hps, 2, _VT_ROWS, tq), _F32),
               pltpu.VMEM((2 * hps, tq, tq), _F32),
               pltpu.VMEM((2 * hps, tq, tq), _F32)]
    return pl.pallas_call(
        functools.partial(_diff_attention_kernel, lam_init=lam_init, tq=tq,
                          heads_per_step=hps),
        grid=(b, DIFF_HEADS // hps, s // tq),
        in_specs=[
            pl.BlockSpec(memory_space=pltpu.SMEM),
            vec(DIFF_HALF), vec(DIFF_HALF), vec(DIFF_HALF), vec(DIFF_HALF), vec(HEAD_DIM),
            spec(first_blk=_DQ_BLK), spec(first_blk=_DK_BLK), spec(first_blk=_DV_BLK),
            spec(first_blk=_GATE_BLK),
        ],
        out_specs=pl.BlockSpec((None, tq, width), lambda bi, hi, i: (bi, i, hi)),
        out_shape=jax.ShapeDtypeStruct((b, s, DIFF_WIDTH), _BF16),
        scratch_shapes=scratch,
        compiler_params=pltpu.CompilerParams(
            dimension_semantics=("parallel", "parallel", "arbitrary"),
            vmem_limit_bytes=_VMEM_LIMIT),
        name="diff_attention",
    )(slopes, lq1.reshape(1, -1), lk1.reshape(1, -1), lq2.reshape(1, -1), lk2.reshape(1, -1),
      subg.reshape(1, -1), proj, proj, proj, proj)


def _moba_attention_kernel(slopes_ref, q_ref, k_ref, v_ref, gate_ref, o_ref,
                           kmean_scr, k_scr, vt_scr, sel_scr, acc_scr, sa_scr, sb_scr,
                           *, n_blocks, heads_per_step, q_tiles_per_step):
    step = pl.program_id(2)

    @pl.when(step == 0)
    def _():
        kmean_scr[...] = jnp.zeros(kmean_scr.shape, _F32)

    for t in range(q_tiles_per_step):
        _moba_q_tile(step * q_tiles_per_step + t, slice(t * MOBA_BLOCK, (t + 1) * MOBA_BLOCK),
                     slopes_ref, q_ref, k_ref, v_ref, gate_ref, o_ref,
                     kmean_scr, k_scr, vt_scr, sel_scr, acc_scr.at[t], sa_scr, sb_scr,
                     n_blocks=n_blocks, heads_per_step=heads_per_step)


def _moba_q_tile(i, rows, slopes_ref, q_ref, k_ref, v_ref, gate_ref, o_ref,
                 kmean_scr, k_scr, vt_scr, sel_scr, acc_scr, sa_scr, sb_scr,
                 *, n_blocks, heads_per_step):
    tq = MOBA_BLOCK
    heads = range(heads_per_step)

    lane = lax.broadcasted_iota(jnp.int32, (tq, HEAD_DIM), 1)

    for g in heads:
        _stage_values_transposed(v_ref[rows, _head_lanes(g)], vt_scr, g, i)
        k = k_ref[rows, _head_lanes(g)]
        k_scr[g, i] = k
        kmean_scr[g, pl.ds(i, 1), :] = (jnp.sum(k.astype(_F32), axis=0, keepdims=True)
                                        * (1.0 / MOBA_BLOCK))

    key_idx = lax.broadcasted_iota(jnp.int32, (tq, tq), 0)
    qry_idx = lax.broadcasted_iota(jnp.int32, (tq, tq), 1)
    qry_pos = lax.broadcasted_iota(jnp.int32, (1, tq), 1).astype(_F32)
    blk = lax.broadcasted_iota(jnp.int32, (n_blocks, tq), 0)
    past = blk < i

    slopes = [slopes_ref[pl.program_id(1) * heads_per_step + g] * _LOG2E for g in heads]
    qs = [q_ref[rows, _head_lanes(g)] for g in heads]
    qa_ts = [jnp.concatenate([qs[g].astype(_F32), _bias_query_lanes(slopes[g], lane, 0)],
                             axis=1).T.astype(_BF16) for g in heads]
    key_pos_lanes = _bias_key_lanes(lane, 0).astype(_BF16)

    def scores(g, j, s_buf):
        ka = jnp.concatenate([k_scr[g, j], key_pos_lanes], axis=1)
        s = _dot(ka, qa_ts[g])
        s_buf[g] = s
        return jnp.max(s, axis=0, keepdims=True)

    def select_blocks(g):
        rest = kmean_scr[g]
        pieces = []
        for _ in range(_F32_AS_BF16_PIECES):
            pieces.append(rest.astype(_BF16))
            rest = rest - pieces[-1].astype(_F32)
        parts = _dot_nt(jnp.concatenate(pieces, axis=0), qs[g])
        gate = sum(parts[t * n_blocks:(t + 1) * n_blocks] for t in range(_F32_AS_BF16_PIECES))
        gate = jnp.where(past, gate, _NEG_INF)
        for r0 in range(0, n_blocks, _SUBLANES):
            tile = gate[r0:r0 + _SUBLANES]
            row = lax.broadcasted_iota(jnp.int32, tile.shape, 0) + r0
            rank = jnp.zeros(tile.shape, _F32)
            for n in range(n_blocks):
                g_n = gate[n:n + 1, :]
                if n < r0:
                    beats = g_n >= tile
                elif n >= r0 + _SUBLANES:
                    beats = g_n > tile
                else:
                    beats = (g_n > tile) | ((g_n == tile) & (n < row))
                rank = rank + jnp.where(beats, 1.0, 0.0)
            sel_scr[g, r0:r0 + _SUBLANES, :] = jnp.where(
                (row < i) & (rank < MOBA_TOPK), 0.0, _NEG_INF)

    def first_update(g, s_buf):
        select_blocks(g)
        z = jnp.where(key_idx <= qry_idx, s_buf[g], _NEG_INF)
        return _first_flash_step(z, -slopes[g] * qry_pos, vt_scr[g, i], acc_scr.at[g])

    def update(g, j, s_buf, tile_max, running_max):
        rb = (-slopes[g] * qry_pos - slopes[g] * ((i - j) * tq).astype(_F32)
              + sel_scr[g, pl.ds(j, 1), :])
        return _flash_step(s_buf.at[g], tile_max, rb, vt_scr[g, j], running_max, acc_scr.at[g])

    _pipelined_flash(heads_per_step, i, scores, first_update, update, sa_scr, sb_scr)
    for g in heads:
        o = _normalised(acc_scr[g]).T
        gate = gate_ref[rows, _head_lanes(g)].astype(_F32)
        o_ref[rows, _head_lanes(g)] = (o * _silu(gate)).astype(o_ref.dtype)


def moba_attention(proj, slopes, *, heads_per_step, q_tiles_per_step):
    b, s, _ = proj.shape
    hps = heads_per_step
    assert s % MOBA_BLOCK == 0 and MOBA_HEADS % hps == 0
    n_blocks = s // MOBA_BLOCK
    assert n_blocks % _SUBLANES == 0 and n_blocks % q_tiles_per_step == 0
    tq = MOBA_BLOCK
    width = hps * HEAD_DIM
    step_rows = q_tiles_per_step * tq
    spec = functools.partial(_head_group_spec, step_rows, width, hps=hps)
    return pl.pallas_call(
        functools.partial(_moba_attention_kernel, n_blocks=n_blocks, heads_per_step=hps,
                          q_tiles_per_step=q_tiles_per_step),
        grid=(b, MOBA_HEADS // hps, n_blocks // q_tiles_per_step),
        in_specs=[
            pl.BlockSpec(memory_space=pltpu.SMEM),
            spec(first_blk=_MQ_BLK), spec(first_blk=_MK_BLK), spec(first_blk=_MV_BLK),
            spec(first_blk=_GATE_BLK + DIFF_HEADS),
        ],
        out_specs=pl.BlockSpec((None, step_rows, width), lambda bi, hi, i: (bi, i, hi)),
        out_shape=jax.ShapeDtypeStruct((b, s, MOBA_WIDTH), _BF16),
        scratch_shapes=[pltpu.VMEM((hps, n_blocks, HEAD_DIM), _F32),
                        pltpu.VMEM((hps, n_blocks, tq, HEAD_DIM), _BF16),
                        pltpu.VMEM((hps, n_blocks, _VT_ROWS, tq), _BF16),
                        pltpu.VMEM((hps, n_blocks, tq), _F32),
                        pltpu.VMEM((q_tiles_per_step, hps, _VT_ROWS, tq), _F32),
                        pltpu.VMEM((hps, tq, tq), _F32),
                        pltpu.VMEM((hps, tq, tq), _F32)],
        compiler_params=pltpu.CompilerParams(
            dimension_semantics=("parallel", "parallel", "arbitrary"),
            vmem_limit_bytes=_VMEM_LIMIT),
        name="moba_attention",
    )(slopes, proj, proj, proj, proj)


def _layer_tail_kernel(yd_ref, ym_ref, x_ref, wa_ref, wb_ref, g_ref, wq_ref, kv_ref, wo_ref,
                       fg_ref, o_ref, q_scr, o_scr, *, apply_final, sub_rows):
    subs = [slice(r, r + sub_rows) for r in range(0, x_ref.shape[0], sub_rows)]
    xs = [x_ref[r, :] + _dot(yd_ref[r, :], wa_ref[...]) + _dot(ym_ref[r, :], wb_ref[...])
          for r in subs]
    for r, x in zip(subs, xs):
        ms = jnp.mean(x * x, axis=-1, keepdims=True)
        h = (x * lax.rsqrt(ms + EPS) * g_ref[...]).astype(_BF16)
        q_scr[r, :] = (_dot(h, wq_ref[...]) * (HEAD_DIM ** -0.5 * _LOG2E)).astype(_BF16)
    for r in subs:
        for hh in range(MEM_HEADS):
            lo, hi = hh * HEAD_DIM, (hh + 1) * HEAD_DIM
            s = _dot_nt(q_scr[r, lo:hi], kv_ref[:, lo:hi])
            p = jnp.exp2(s - jnp.max(s, axis=-1, keepdims=True))
            l = jnp.sum(p, axis=-1, keepdims=True)
            o_h = _dot(p.astype(_BF16), kv_ref[:, MEM_WIDTH + lo:MEM_WIDTH + hi]) / l
            o_scr[r, lo:hi] = o_h.astype(_BF16)
    for r, x in zip(subs, xs):
        x = x + _dot(o_scr[r, :], wo_ref[...])
        if apply_final:
            ms = jnp.mean(x * x, axis=-1, keepdims=True)
            x = x * lax.rsqrt(ms + EPS) * fg_ref[...]
        o_ref[r, :] = x


def layer_tail(yd, ym, x, w_out, g, wq, kv, wo, final_g, *, apply_final, tm, sub_rows):
    b, s, d = x.shape
    mlen = kv.shape[1]
    ka, kb = yd.shape[-1], ym.shape[-1]
    assert s % tm == 0 and ka == kb and w_out.shape[0] == ka + kb
    rows = lambda width: pl.BlockSpec((None, tm, width), lambda bi, i: (bi, i, 0))
    whole = lambda shape: pl.BlockSpec(shape, lambda bi, i: (0,) * len(shape))
    return pl.pallas_call(
        functools.partial(_layer_tail_kernel, apply_final=apply_final, sub_rows=sub_rows),
        grid=(b, s // tm),
        in_specs=[
            rows(ka), rows(kb), rows(d),
            pl.BlockSpec((ka, d), lambda bi, i: (0, 0)),
            pl.BlockSpec((kb, d), lambda bi, i: (1, 0)),
            whole((1, d)), whole((d, MEM_WIDTH)),
            pl.BlockSpec((None, mlen, 2 * MEM_WIDTH), lambda bi, i: (bi, 0, 0)),
            whole((MEM_WIDTH, d)), whole((1, d)),
        ],
        out_specs=rows(d),
        out_shape=jax.ShapeDtypeStruct((b, s, d), _F32),
        scratch_shapes=[pltpu.VMEM((tm, MEM_WIDTH), _BF16),
                        pltpu.VMEM((tm, MEM_WIDTH), _BF16)],
        compiler_params=pltpu.CompilerParams(
            dimension_semantics=("parallel", "parallel"),
            vmem_limit_bytes=_VMEM_LIMIT),
        name="layer_tail",
    )(yd, ym, x, w_out, w_out, g.reshape(1, d), wq, kv, wo, final_g.reshape(1, d))


def _in_proj_col_scale(n_cols):
    cs = np.ones((n_cols,), np.float32)
    cs[_DQ_BLK * HEAD_DIM:_DK_BLK * HEAD_DIM] = DIFF_HALF ** -0.5 * _LOG2E
    cs[_MQ_BLK * HEAD_DIM:_MK_BLK * HEAD_DIM] = HEAD_DIM ** -0.5 * _LOG2E
    return jnp.asarray(cs)


def _alibi_slopes(n):
    return jnp.asarray(2.0 ** (-8.0 * np.arange(1, n + 1) / n), dtype=_F32)


_IN_PROJ_TILE = (1024, 1024)
_ATTN_Q_TILE = 256
_DIFF_HEADS_PER_STEP = 8
_MOBA_HEADS_PER_STEP = 8
_MOBA_Q_TILES_PER_STEP = 2
_TAIL_ROW_TILE = 512
_TAIL_SUB_ROWS = 256


def kernel(x, mem, norm_mix_g, w_in, lambda_q1, lambda_k1, lambda_q2, lambda_k2, subln_g,
           w_out, norm_mem_g, mem_norm_g, w_q_mem, w_kv_mem, w_o_mem, final_g):
    b, s, d = x.shape
    mlen = mem.shape[1]
    depth = w_in.shape[0]
    diff_slopes = _alibi_slopes(DIFF_HEADS)
    moba_slopes = _alibi_slopes(MOBA_HEADS)
    mem2 = mem.reshape(b * mlen, d)
    for l in range(depth):
        lam_init = 0.8 - 0.6 * math.exp(-0.3 * l)
        proj = rms_matmul(x.reshape(b * s, d), norm_mix_g[l], w_in[l],
                          _in_proj_col_scale(w_in.shape[-1]),
                          tm=_IN_PROJ_TILE[0], tn=_IN_PROJ_TILE[1]).reshape(b, s, -1)
        y_d = diff_attention(proj, diff_slopes, lambda_q1[l], lambda_k1[l], lambda_q2[l],
                             lambda_k2[l], subln_g[l], lam_init=lam_init, tq=_ATTN_Q_TILE,
                             heads_per_step=_DIFF_HEADS_PER_STEP)
        y_m = moba_attention(proj, moba_slopes, heads_per_step=_MOBA_HEADS_PER_STEP,
                             q_tiles_per_step=_MOBA_Q_TILES_PER_STEP)
        kv = rms_matmul(mem2, mem_norm_g[l], w_kv_mem[l],
                        tm=b * mlen, tn=2 * MEM_WIDTH).reshape(b, mlen, 2 * MEM_WIDTH)
        x = layer_tail(y_d, y_m, x, w_out[l].astype(_BF16), norm_mem_g[l],
                       w_q_mem[l].astype(_BF16), kv, w_o_mem[l].astype(_BF16), final_g,
                       apply_final=(l == depth - 1), tm=_TAIL_ROW_TILE,
                       sub_rows=_TAIL_SUB_ROWS)
    return x
```

```python
import functools
import math

import jax
import jax.numpy as jnp
import numpy as np
from jax import lax
from jax.experimental import pallas as pl
from jax.experimental.pallas import tpu as pltpu

HEAD_DIM = 128
DIFF_HEADS = 8
DIFF_HALF = HEAD_DIM // 2
DIFF_WIDTH = DIFF_HEADS * HEAD_DIM
MOBA_HEADS = 8
MOBA_WIDTH = MOBA_HEADS * HEAD_DIM
MOBA_BLOCK = 256
MOBA_TOPK = 3
MEM_HEADS = 4
MEM_WIDTH = MEM_HEADS * HEAD_DIM
EPS = 1e-6

_DQ_BLK = 0
_DK_BLK = DIFF_HEADS
_DV_BLK = 2 * DIFF_HEADS
_MQ_BLK = 3 * DIFF_HEADS
_MK_BLK = 3 * DIFF_HEADS + MOBA_HEADS
_MV_BLK = 3 * DIFF_HEADS + 2 * MOBA_HEADS
_GATE_BLK = 3 * DIFF_HEADS + 3 * MOBA_HEADS

_V7X_VMEM_BYTES = 64 * 1024 * 1024
_VMEM_LIMIT = _V7X_VMEM_BYTES * 3 // 4
_VMEM_LIMIT_DIFF_ATTENTION = _V7X_VMEM_BYTES * 7 // 8

_BF16 = jnp.bfloat16
_F32 = jnp.float32
_NEG_INF = float("-inf")


def _dot_nt(a, b):
    return lax.dot_general(a, b, (((1,), (1,)), ((), ())), preferred_element_type=_F32)


def _dot(a, b):
    return jnp.dot(a, b, preferred_element_type=_F32)


def _silu(g):
    return g * jax.nn.sigmoid(g)


_NORM_CHUNK_ROWS = 256

def _rms_matmul_kernel(x_ref, g_ref, w_ref, *rest, has_col_scale, norm_rows):
    cs_ref, o_ref, h_scr = rest if has_col_scale else (None,) + rest

    def project(rows):
        out = _dot(h_scr[rows, :], w_ref[...].astype(_BF16))
        if has_col_scale:
            out = out * cs_ref[...]
        o_ref[rows, :] = out.astype(o_ref.dtype)

    @pl.when(pl.program_id(1) == 0)
    def _():
        for r0 in range(0, x_ref.shape[0], norm_rows):
            rows = slice(r0, r0 + norm_rows)
            x = x_ref[rows, :]
            ms = jnp.mean(x * x, axis=-1, keepdims=True)
            h_scr[rows, :] = (x * lax.rsqrt(ms + EPS) * g_ref[...]).astype(_BF16)
            project(rows)

    @pl.when(pl.program_id(1) != 0)
    def _():
        project(slice(None))


def rms_matmul(x, g, w, col_scale=None, *, tm, tn, out_dtype=_BF16):
    m, k = x.shape
    n = w.shape[1]
    assert m % tm == 0 and n % tn == 0
    has_cs = col_scale is not None
    cs_specs = [pl.BlockSpec((1, tn), lambda i, j: (0, j))] if has_cs else []
    cs_args = [col_scale.reshape(1, n)] if has_cs else []
    norm_rows = min(tm, _NORM_CHUNK_ROWS)
    assert tm % norm_rows == 0
    return pl.pallas_call(
        functools.partial(_rms_matmul_kernel, has_col_scale=has_cs, norm_rows=norm_rows),
        grid=(m // tm, n // tn),
        in_specs=[
            pl.BlockSpec((tm, k), lambda i, j: (i, 0)),
            pl.BlockSpec((1, k), lambda i, j: (0, 0)),
            pl.BlockSpec((k, tn), lambda i, j: (0, j)),
        ] + cs_specs,
        out_specs=pl.BlockSpec((tm, tn), lambda i, j: (i, j)),
        out_shape=jax.ShapeDtypeStruct((m, n), out_dtype),
        scratch_shapes=[pltpu.VMEM((tm, k), _BF16)],
        compiler_params=pltpu.CompilerParams(
            dimension_semantics=("parallel", "arbitrary"),
            vmem_limit_bytes=_VMEM_LIMIT),
        name="rms_matmul",
    )(x, g.reshape(1, k), w, *cs_args)


_LOG2E = math.log2(math.e)
_ONES_ROWS = 16
_VT_ROWS = HEAD_DIM + _ONES_ROWS


def _first_flash_step(z, rb, vt, acc_ref):
    m = jnp.max(z, axis=0, keepdims=True) + rb
    p = jnp.exp2(z - (m - rb))
    acc_ref[...] = _dot(vt, p.astype(_BF16))
    return m


def _flash_step(z_ref, z_max, rb, vt, m, acc_ref):
    m_new = jnp.maximum(m, z_max + rb)
    p = jnp.exp2(z_ref[...] - (m_new - rb))
    acc_ref[...] = jnp.exp2(m - m_new) * acc_ref[...] + _dot(vt, p.astype(_BF16))
    return m_new


_F32_AS_BF16_PIECES = 3
_BIAS_LANES = _F32_AS_BF16_PIECES
_SUBLANES = 8


def _bias_key_lanes(lane, lane0):
    pos = lax.broadcasted_iota(jnp.int32, lane.shape, 0).astype(_F32)
    return jnp.where((lane >= lane0) & (lane < lane0 + _BIAS_LANES), pos, 0.0)


def _bias_query_lanes(slope, lane, lane0):
    rest = jnp.full(lane.shape, slope, _F32)
    out = jnp.zeros(lane.shape, _F32)
    for t in range(_BIAS_LANES):
        piece = rest.astype(_BF16).astype(_F32)
        out = jnp.where(lane == lane0 + t, piece, out)
        rest = rest - piece
    return out


def _stage_values_transposed(v_tile, vt_scr, g, n):
    vt_scr[g, n, :HEAD_DIM, :] = v_tile.T
    vt_scr[g, n, HEAD_DIM:, :] = jnp.ones((_ONES_ROWS, v_tile.shape[0]), _BF16)


def _normalised(acc):
    return acc[:HEAD_DIM] / acc[HEAD_DIM:HEAD_DIM + 1]


def _pipelined_flash(n_streams, n_past, scores, first_update, update, sa_scr, sb_scr):
    streams = range(n_streams)
    for n in streams:
        scores(n, n_past, sb_scr)
    max_a = tuple(scores(n, 0, sa_scr) for n in streams)
    running = tuple(first_update(n, sb_scr) for n in streams)

    def pair(t, carry):
        running, max_a = (list(c) for c in carry)
        j0 = 2 * t
        max_b = [None] * n_streams
        for n in streams:
            max_b[n] = scores(n, j0 + 1, sb_scr)
            running[n] = update(n, j0, sa_scr, max_a[n], running[n])
        for n in streams:
            max_a[n] = scores(n, j0 + 2, sa_scr)
            running[n] = update(n, j0 + 1, sb_scr, max_b[n], running[n])
        return tuple(running), tuple(max_a)

    def two_pairs(t, carry):
        return pair(2 * t + 1, pair(2 * t, carry))

    n_pairs = n_past // 2
    carry = lax.fori_loop(0, n_pairs // 2, two_pairs, (running, max_a))
    running, max_a = lax.fori_loop(n_pairs - n_pairs % 2, n_pairs, pair, carry)

    @pl.when(n_past % 2 == 1)
    def _():
        for n in streams:
            update(n, n_past - 1, sa_scr, max_a[n], running[n])


def _head_lanes(g):
    return slice(g * HEAD_DIM, (g + 1) * HEAD_DIM)


def _head_group_spec(rows, width, first_blk, hps):
    assert first_blk % hps == 0
    return pl.BlockSpec((None, rows, width),
                        lambda bi, hi, i: (bi, i, first_blk // hps + hi))


def _diff_attention_kernel(slopes_ref, lq1_ref, lk1_ref, lq2_ref, lk2_ref, subg_ref,
                           q_ref, k_ref, v_ref, gate_ref, o_ref,
                           vt_scr, ka_scr, acc_scr, sa_scr, sb_scr,
                           *, lam_init, tq, heads_per_step, q_tiles_per_step):
    for t in range(q_tiles_per_step):
        _diff_q_tile(pl.program_id(2) * q_tiles_per_step + t, slice(t * tq, (t + 1) * tq),
                     slopes_ref, lq1_ref, lk1_ref, lq2_ref, lk2_ref, subg_ref,
                     q_ref, k_ref, v_ref, gate_ref, o_ref,
                     vt_scr, ka_scr, acc_scr.at[t], sa_scr, sb_scr,
                     lam_init=lam_init, tq=tq, heads_per_step=heads_per_step)


def _diff_q_tile(i, rows, slopes_ref, lq1_ref, lk1_ref, lq2_ref, lk2_ref, subg_ref,
                 q_ref, k_ref, v_ref, gate_ref, o_ref, vt_scr, ka_scr, acc_scr, sa_scr, sb_scr,
                 *, lam_init, tq, heads_per_step):
    heads = range(heads_per_step)
    lane = lax.broadcasted_iota(jnp.int32, (tq, HEAD_DIM), 1)
    own_half = (lane < DIFF_HALF, lane >= DIFF_HALF)
    bias_lane0 = (DIFF_HALF, 0)

    for g in heads:
        _stage_values_transposed(v_ref[rows, _head_lanes(g)], vt_scr, g, i)
        k = k_ref[rows, _head_lanes(g)].astype(_F32)
        for c in range(2):
            ka_scr[g, c, i] = jnp.where(
                own_half[c], k, _bias_key_lanes(lane, bias_lane0[c])).astype(_BF16)

    key_idx = lax.broadcasted_iota(jnp.int32, (tq, tq), 0)
    qry_idx = lax.broadcasted_iota(jnp.int32, (tq, tq), 1)
    qry_pos = lax.broadcasted_iota(jnp.int32, (1, tq), 1).astype(_F32)

    streams = []
    for g in heads:
        slope = slopes_ref[pl.program_id(1) * heads_per_step + g] * _LOG2E
        q = q_ref[rows, _head_lanes(g)].astype(_F32)
        for c in range(2):
            qc = jnp.where(own_half[c], q, _bias_query_lanes(slope, lane, bias_lane0[c]))
            streams.append((g, c, slope, qc.astype(_BF16)))

    def scores(n, j, s_buf):
        g, c, _, qc = streams[n]
        s = _dot_nt(ka_scr[g, c, j], qc)
        s_buf[n] = s
        return jnp.max(s, axis=0, keepdims=True)

    def first_update(n, s_buf):
        g, c, slope, _ = streams[n]
        z = jnp.where(key_idx <= qry_idx, s_buf[n], _NEG_INF)
        return _first_flash_step(z, -slope * qry_pos, vt_scr[g, i], acc_scr.at[g, c])

    def update(n, j, s_buf, tile_max, running_max):
        g, c, slope, _ = streams[n]
        rb = -slope * qry_pos - slope * ((i - j) * tq).astype(_F32)
        return _flash_step(s_buf.at[n], tile_max, rb, vt_scr[g, j], running_max,
                           acc_scr.at[g, c])

    _pipelined_flash(len(streams), i, scores, first_update, update, sa_scr, sb_scr)

    lam = (jnp.exp(jnp.sum(lq1_ref[...] * lk1_ref[...], axis=-1, keepdims=True))
           - jnp.exp(jnp.sum(lq2_ref[...] * lk2_ref[...], axis=-1, keepdims=True))
           + lam_init)
    for g in heads:
        o_t = _normalised(acc_scr[g, 0]) - lam * _normalised(acc_scr[g, 1])
        ms = jnp.mean(o_t * o_t, axis=0, keepdims=True)
        o = (o_t * lax.rsqrt(ms + EPS)).T * subg_ref[...] * (1.0 - lam_init)
        gate = gate_ref[rows, _head_lanes(g)].astype(_F32)
        o_ref[rows, _head_lanes(g)] = (o * _silu(gate)).astype(o_ref.dtype)


def diff_attention(proj, slopes, lq1, lk1, lq2, lk2, subg, *, lam_init, tq, heads_per_step,
                   q_tiles_per_step):
    b, s, _ = proj.shape
    hps = heads_per_step
    step_rows = q_tiles_per_step * tq
    assert s % step_rows == 0 and DIFF_HEADS % hps == 0
    width = hps * HEAD_DIM
    vec = lambda n: pl.BlockSpec((1, n), lambda bi, hi, i: (0, 0))
    spec = functools.partial(_head_group_spec, step_rows, width, hps=hps)
    scratch = [pltpu.VMEM((hps, s // tq, _VT_ROWS, tq), _BF16),
               pltpu.VMEM((hps, 2, s // tq, tq, HEAD_DIM), _BF16),
               pltpu.VMEM((q_tiles_per_step, hps, 2, _VT_ROWS, tq), _F32),
               pltpu.VMEM((2 * hps, tq, tq), _F32),
               pltpu.VMEM((2 * hps, tq, tq), _F32)]
    return pl.pallas_call(
        functools.partial(_diff_attention_kernel, lam_init=lam_init, tq=tq,
                          heads_per_step=hps, q_tiles_per_step=q_tiles_per_step),
        grid=(b, DIFF_HEADS // hps, s // step_rows),
        in_specs=[
            pl.BlockSpec(memory_space=pltpu.SMEM),
            vec(DIFF_HALF), vec(DIFF_HALF), vec(DIFF_HALF), vec(DIFF_HALF), vec(HEAD_DIM),
            spec(first_blk=_DQ_BLK), spec(first_blk=_DK_BLK), spec(first_blk=_DV_BLK),
            spec(first_blk=_GATE_BLK),
        ],
        out_specs=pl.BlockSpec((None, step_rows, width), lambda bi, hi, i: (bi, i, hi)),
        out_shape=jax.ShapeDtypeStruct((b, s, DIFF_WIDTH), _BF16),
        scratch_shapes=scratch,
        compiler_params=pltpu.CompilerParams(
            dimension_semantics=("parallel", "parallel", "arbitrary"),
            vmem_limit_bytes=_VMEM_LIMIT_DIFF_ATTENTION),
        name="diff_attention",
    )(slopes, lq1.reshape(1, -1), lk1.reshape(1, -1), lq2.reshape(1, -1), lk2.reshape(1, -1),
      subg.reshape(1, -1), proj, proj, proj, proj)


def _moba_attention_kernel(slopes_ref, q_ref, k_ref, v_ref, gate_ref, o_ref,
                           kmean_scr, k_scr, vt_scr, sel_scr, acc_scr, sa_scr, sb_scr,
                           *, n_blocks, heads_per_step, q_tiles_per_step):
    step = pl.program_id(2)

    @pl.when(step == 0)
    def _():
        kmean_scr[...] = jnp.zeros(kmean_scr.shape, _F32)

    for t in range(q_tiles_per_step):
        _moba_q_tile(step * q_tiles_per_step + t, slice(t * MOBA_BLOCK, (t + 1) * MOBA_BLOCK),
                     slopes_ref, q_ref, k_ref, v_ref, gate_ref, o_ref,
                     kmean_scr, k_scr, vt_scr, sel_scr, acc_scr.at[t], sa_scr, sb_scr,
                     n_blocks=n_blocks, heads_per_step=heads_per_step)


def _moba_q_tile(i, rows, slopes_ref, q_ref, k_ref, v_ref, gate_ref, o_ref,
                 kmean_scr, k_scr, vt_scr, sel_scr, acc_scr, sa_scr, sb_scr,
                 *, n_blocks, heads_per_step):
    tq = MOBA_BLOCK
    heads = range(heads_per_step)

    lane = lax.broadcasted_iota(jnp.int32, (tq, HEAD_DIM), 1)

    for g in heads:
        _stage_values_transposed(v_ref[rows, _head_lanes(g)], vt_scr, g, i)
        k = k_ref[rows, _head_lanes(g)]
        k_scr[g, i] = k
        kmean_scr[g, pl.ds(i, 1), :] = (jnp.sum(k.astype(_F32), axis=0, keepdims=True)
                                        * (1.0 / MOBA_BLOCK))

    key_idx = lax.broadcasted_iota(jnp.int32, (tq, tq), 0)
    qry_idx = lax.broadcasted_iota(jnp.int32, (tq, tq), 1)
    qry_pos = lax.broadcasted_iota(jnp.int32, (1, tq), 1).astype(_F32)
    blk = lax.broadcasted_iota(jnp.int32, (n_blocks, tq), 0)
    past = blk < i

    slopes = [slopes_ref[pl.program_id(1) * heads_per_step + g] * _LOG2E for g in heads]
    qs = [q_ref[rows, _head_lanes(g)] for g in heads]
    qa_ts = [jnp.concatenate([qs[g].astype(_F32), _bias_query_lanes(slopes[g], lane, 0)],
                             axis=1).T.astype(_BF16) for g in heads]
    key_pos_lanes = _bias_key_lanes(lane, 0).astype(_BF16)

    def scores(g, j, s_buf):
        ka = jnp.concatenate([k_scr[g, j], key_pos_lanes], axis=1)
        s = _dot(ka, qa_ts[g])
        s_buf[g] = s
        return jnp.max(s, axis=0, keepdims=True)

    def select_blocks(g):
        rest = kmean_scr[g]
        pieces = []
        for _ in range(_F32_AS_BF16_PIECES):
            pieces.append(rest.astype(_BF16))
            rest = rest - pieces[-1].astype(_F32)
        parts = _dot_nt(jnp.concatenate(pieces, axis=0), qs[g])
        gate = sum(parts[t * n_blocks:(t + 1) * n_blocks] for t in range(_F32_AS_BF16_PIECES))
        gate = jnp.where(past, gate, _NEG_INF)
        for r0 in range(0, n_blocks, _SUBLANES):
            tile = gate[r0:r0 + _SUBLANES]
            row = lax.broadcasted_iota(jnp.int32, tile.shape, 0) + r0
            rank = jnp.zeros(tile.shape, _F32)
            for n in range(n_blocks):
                g_n = gate[n:n + 1, :]
                if n < r0:
                    beats = g_n >= tile
                elif n >= r0 + _SUBLANES:
                    beats = g_n > tile
                else:
                    beats = (g_n > tile) | ((g_n == tile) & (n < row))
                rank = rank + jnp.where(beats, 1.0, 0.0)
            sel_scr[g, r0:r0 + _SUBLANES, :] = jnp.where(
                (row < i) & (rank < MOBA_TOPK), 0.0, _NEG_INF)

    def first_update(g, s_buf):
        select_blocks(g)
        z = jnp.where(key_idx <= qry_idx, s_buf[g], _NEG_INF)
        return _first_flash_step(z, -slopes[g] * qry_pos, vt_scr[g, i], acc_scr.at[g])

    def update(g, j, s_buf, tile_max, running_max):
        rb = (-slopes[g] * qry_pos - slopes[g] * ((i - j) * tq).astype(_F32)
              + sel_scr[g, pl.ds(j, 1), :])
        return _flash_step(s_buf.at[g], tile_max, rb, vt_scr[g, j], running_max, acc_scr.at[g])

    _pipelined_flash(heads_per_step, i, scores, first_update, update, sa_scr, sb_scr)
    for g in heads:
        o = _normalised(acc_scr[g]).T
        gate = gate_ref[rows, _head_lanes(g)].astype(_F32)
        o_ref[rows, _head_lanes(g)] = (o * _silu(gate)).astype(o_ref.dtype)


def moba_attention(proj, slopes, *, heads_per_step, q_tiles_per_step):
    b, s, _ = proj.shape
    hps = heads_per_step
    assert s % MOBA_BLOCK == 0 and MOBA_HEADS % hps == 0
    n_blocks = s // MOBA_BLOCK
    assert n_blocks % _SUBLANES == 0 and n_blocks % q_tiles_per_step == 0
    tq = MOBA_BLOCK
    width = hps * HEAD_DIM
    step_rows = q_tiles_per_step * tq
    spec = functools.partial(_head_group_spec, step_rows, width, hps=hps)
    return pl.pallas_call(
        functools.partial(_moba_attention_kernel, n_blocks=n_blocks, heads_per_step=hps,
                          q_tiles_per_step=q_tiles_per_step),
        grid=(b, MOBA_HEADS // hps, n_blocks // q_tiles_per_step),
        in_specs=[
            pl.BlockSpec(memory_space=pltpu.SMEM),
            spec(first_blk=_MQ_BLK), spec(first_blk=_MK_BLK), spec(first_blk=_MV_BLK),
            spec(first_blk=_GATE_BLK + DIFF_HEADS),
        ],
        out_specs=pl.BlockSpec((None, step_rows, width), lambda bi, hi, i: (bi, i, hi)),
        out_shape=jax.ShapeDtypeStruct((b, s, MOBA_WIDTH), _BF16),
        scratch_shapes=[pltpu.VMEM((hps, n_blocks, HEAD_DIM), _F32),
                        pltpu.VMEM((hps, n_blocks, tq, HEAD_DIM), _BF16),
                        pltpu.VMEM((hps, n_blocks, _VT_ROWS, tq), _BF16),
                        pltpu.VMEM((hps, n_blocks, tq), _F32),
                        pltpu.VMEM((q_tiles_per_step, hps, _VT_ROWS, tq), _F32),
                        pltpu.VMEM((hps, tq, tq), _F32),
                        pltpu.VMEM((hps, tq, tq), _F32)],
        compiler_params=pltpu.CompilerParams(
            dimension_semantics=("parallel", "parallel", "arbitrary"),
            vmem_limit_bytes=_VMEM_LIMIT),
        name="moba_attention",
    )(slopes, proj, proj, proj, proj)


def _layer_tail_kernel(yd_ref, ym_ref, x_ref, wa_ref, wb_ref, g_ref, wq_ref, kv_ref, wo_ref,
                       fg_ref, o_ref, q_scr, o_scr, *, apply_final, sub_rows):
    subs = [slice(r, r + sub_rows) for r in range(0, x_ref.shape[0], sub_rows)]
    xs = [x_ref[r, :] + _dot(yd_ref[r, :], wa_ref[...]) + _dot(ym_ref[r, :], wb_ref[...])
          for r in subs]
    for r, x in zip(subs, xs):
        ms = jnp.mean(x * x, axis=-1, keepdims=True)
        h = (x * lax.rsqrt(ms + EPS) * g_ref[...]).astype(_BF16)
        q_scr[r, :] = (_dot(h, wq_ref[...]) * (HEAD_DIM ** -0.5 * _LOG2E)).astype(_BF16)
    for r in subs:
        for hh in range(MEM_HEADS):
            lo, hi = hh * HEAD_DIM, (hh + 1) * HEAD_DIM
            s = _dot_nt(q_scr[r, lo:hi], kv_ref[:, lo:hi])
            p = jnp.exp2(s - jnp.max(s, axis=-1, keepdims=True))
            l = jnp.sum(p, axis=-1, keepdims=True)
            o_h = _dot(p.astype(_BF16), kv_ref[:, MEM_WIDTH + lo:MEM_WIDTH + hi]) / l
            o_scr[r, lo:hi] = o_h.astype(_BF16)
    for r, x in zip(subs, xs):
        x = x + _dot(o_scr[r, :], wo_ref[...])
        if apply_final:
            ms = jnp.mean(x * x, axis=-1, keepdims=True)
            x = x * lax.rsqrt(ms + EPS) * fg_ref[...]
        o_ref[r, :] = x


def layer_tail(yd, ym, x, w_out, g, wq, kv, wo, final_g, *, apply_final, tm, sub_rows):
    b, s, d = x.shape
    mlen = kv.shape[1]
    ka, kb = yd.shape[-1], ym.shape[-1]
    assert s % tm == 0 and ka == kb and w_out.shape[0] == ka + kb
    rows = lambda width: pl.BlockSpec((None, tm, width), lambda bi, i: (bi, i, 0))
    whole = lambda shape: pl.BlockSpec(shape, lambda bi, i: (0,) * len(shape))
    return pl.pallas_call(
        functools.partial(_layer_tail_kernel, apply_final=apply_final, sub_rows=sub_rows),
        grid=(b, s // tm),
        in_specs=[
            rows(ka), rows(kb), rows(d),
            pl.BlockSpec((ka, d), lambda bi, i: (0, 0)),
            pl.BlockSpec((kb, d), lambda bi, i: (1, 0)),
            whole((1, d)), whole((d, MEM_WIDTH)),
            pl.BlockSpec((None, mlen, 2 * MEM_WIDTH), lambda bi, i: (bi, 0, 0)),
            whole((MEM_WIDTH, d)), whole((1, d)),
        ],
        out_specs=rows(d),
        out_shape=jax.ShapeDtypeStruct((b, s, d), _F32),
        scratch_shapes=[pltpu.VMEM((tm, MEM_WIDTH), _BF16),
                        pltpu.VMEM((tm, MEM_WIDTH), _BF16)],
        compiler_params=pltpu.CompilerParams(
            dimension_semantics=("parallel", "parallel"),
            vmem_limit_bytes=_VMEM_LIMIT),
        name="layer_tail",
    )(yd, ym, x, w_out, w_out, g.reshape(1, d), wq, kv, wo, final_g.reshape(1, d))


def _in_proj_col_scale(n_cols):
    cs = np.ones((n_cols,), np.float32)
    cs[_DQ_BLK * HEAD_DIM:_DK_BLK * HEAD_DIM] = DIFF_HALF ** -0.5 * _LOG2E
    cs[_MQ_BLK * HEAD_DIM:_MK_BLK * HEAD_DIM] = HEAD_DIM ** -0.5 * _LOG2E
    return jnp.asarray(cs)


def _alibi_slopes(n):
    return jnp.asarray(2.0 ** (-8.0 * np.arange(1, n + 1) / n), dtype=_F32)


_IN_PROJ_TILE = (1024, 1024)
_ATTN_Q_TILE = 256
_DIFF_HEADS_PER_STEP = 8
_MOBA_HEADS_PER_STEP = 8
_ATTN_Q_TILES_PER_STEP = 2
_TAIL_ROW_TILE = 512
_TAIL_SUB_ROWS = 256


def kernel(x, mem, norm_mix_g, w_in, lambda_q1, lambda_k1, lambda_q2, lambda_k2, subln_g,
           w_out, norm_mem_g, mem_norm_g, w_q_mem, w_kv_mem, w_o_mem, final_g):
    b, s, d = x.shape
    mlen = mem.shape[1]
    depth = w_in.shape[0]
    diff_slopes = _alibi_slopes(DIFF_HEADS)
    moba_slopes = _alibi_slopes(MOBA_HEADS)
    mem2 = mem.reshape(b * mlen, d)
    for l in range(depth):
        lam_init = 0.8 - 0.6 * math.exp(-0.3 * l)
        proj = rms_matmul(x.reshape(b * s, d), norm_mix_g[l], w_in[l],
                          _in_proj_col_scale(w_in.shape[-1]),
                          tm=_IN_PROJ_TILE[0], tn=_IN_PROJ_TILE[1]).reshape(b, s, -1)
        y_d = diff_attention(proj, diff_slopes, lambda_q1[l], lambda_k1[l], lambda_q2[l],
                             lambda_k2[l], subln_g[l], lam_init=lam_init, tq=_ATTN_Q_TILE,
                             heads_per_step=_DIFF_HEADS_PER_STEP,
                             q_tiles_per_step=_ATTN_Q_TILES_PER_STEP)
        y_m = moba_attention(proj, moba_slopes, heads_per_step=_MOBA_HEADS_PER_STEP,
                             q_tiles_per_step=_ATTN_Q_TILES_PER_STEP)
        kv = rms_matmul(mem2, mem_norm_g[l], w_kv_mem[l],
                        tm=b * mlen, tn=2 * MEM_WIDTH).reshape(b, mlen, 2 * MEM_WIDTH)
        x = layer_tail(y_d, y_m, x, w_out[l].astype(_BF16), norm_mem_g[l],
                       w_q_mem[l].astype(_BF16), kv, w_o_mem[l].astype(_BF16), final_g,
                       apply_final=(l == depth - 1), tm=_TAIL_ROW_TILE,
                       sub_rows=_TAIL_SUB_ROWS)
    return x
```

```python
import functools
import math

import jax
import jax.numpy as jnp
import numpy as np
from jax import lax
from jax.experimental import pallas as pl
from jax.experimental.pallas import tpu as pltpu

HEAD_DIM = 128
DIFF_HEADS = 8
DIFF_HALF = HEAD_DIM // 2
DIFF_WIDTH = DIFF_HEADS * HEAD_DIM
MOBA_HEADS = 8
MOBA_WIDTH = MOBA_HEADS * HEAD_DIM
MOBA_BLOCK = 256
MOBA_TOPK = 3
MEM_HEADS = 4
MEM_WIDTH = MEM_HEADS * HEAD_DIM
EPS = 1e-6

_DQ_BLK = 0
_DK_BLK = DIFF_HEADS
_DV_BLK = 2 * DIFF_HEADS
_MQ_BLK = 3 * DIFF_HEADS
_MK_BLK = 3 * DIFF_HEADS + MOBA_HEADS
_MV_BLK = 3 * DIFF_HEADS + 2 * MOBA_HEADS
_GATE_BLK = 3 * DIFF_HEADS + 3 * MOBA_HEADS

_V7X_VMEM_BYTES = 64 * 1024 * 1024
_VMEM_LIMIT = _V7X_VMEM_BYTES * 3 // 4
_VMEM_LIMIT_DIFF_ATTENTION = _V7X_VMEM_BYTES * 7 // 8

_BF16 = jnp.bfloat16
_F32 = jnp.float32
_NEG_INF = float("-inf")


def _dot_nt(a, b):
    return lax.dot_general(a, b, (((1,), (1,)), ((), ())), preferred_element_type=_F32)


def _dot(a, b):
    return jnp.dot(a, b, preferred_element_type=_F32)


def _silu(g):
    return g * jax.nn.sigmoid(g)


_NORM_CHUNK_ROWS = 256

def _rms_matmul_kernel(x_ref, g_ref, w_ref, *rest, has_col_scale, norm_rows):
    cs_ref, o_ref, h_scr = rest if has_col_scale else (None,) + rest

    def project(rows):
        out = _dot(h_scr[rows, :], w_ref[...].astype(_BF16))
        if has_col_scale:
            out = out * cs_ref[...]
        o_ref[rows, :] = out.astype(o_ref.dtype)

    @pl.when(pl.program_id(1) == 0)
    def _():
        for r0 in range(0, x_ref.shape[0], norm_rows):
            rows = slice(r0, r0 + norm_rows)
            x = x_ref[rows, :]
            ms = jnp.mean(x * x, axis=-1, keepdims=True)
            h_scr[rows, :] = (x * lax.rsqrt(ms + EPS) * g_ref[...]).astype(_BF16)
            project(rows)

    @pl.when(pl.program_id(1) != 0)
    def _():
        project(slice(None))


def rms_matmul(x, g, w, col_scale=None, *, tm, tn, out_dtype=_BF16, vmem_limit=None):
    m, k = x.shape
    n = w.shape[1]
    assert m % tm == 0 and n % tn == 0
    has_cs = col_scale is not None
    cs_specs = [pl.BlockSpec((1, tn), lambda i, j: (0, j))] if has_cs else []
    cs_args = [col_scale.reshape(1, n)] if has_cs else []
    norm_rows = min(tm, _NORM_CHUNK_ROWS)
    assert tm % norm_rows == 0
    return pl.pallas_call(
        functools.partial(_rms_matmul_kernel, has_col_scale=has_cs, norm_rows=norm_rows),
        grid=(m // tm, n // tn),
        in_specs=[
            pl.BlockSpec((tm, k), lambda i, j: (i, 0)),
            pl.BlockSpec((1, k), lambda i, j: (0, 0)),
            pl.BlockSpec((k, tn), lambda i, j: (0, j)),
        ] + cs_specs,
        out_specs=pl.BlockSpec((tm, tn), lambda i, j: (i, j)),
        out_shape=jax.ShapeDtypeStruct((m, n), out_dtype),
        scratch_shapes=[pltpu.VMEM((tm, k), _BF16)],
        compiler_params=pltpu.CompilerParams(
            dimension_semantics=("parallel", "arbitrary"),
            vmem_limit_bytes=vmem_limit or _VMEM_LIMIT),
        name="rms_matmul",
    )(x, g.reshape(1, k), w, *cs_args)


_LOG2E = math.log2(math.e)
_ONES_ROWS = 16
_VT_ROWS = HEAD_DIM + _ONES_ROWS


def _first_flash_step(z, rb, vt, acc_ref):
    m = jnp.max(z, axis=0, keepdims=True) + rb
    p = jnp.exp2(z - (m - rb))
    acc_ref[...] = _dot(vt, p.astype(_BF16))
    return m


def _flash_step(z_ref, z_max, rb, vt, m, acc_ref):
    m_new = jnp.maximum(m, z_max + rb)
    p = jnp.exp2(z_ref[...] - (m_new - rb))
    acc_ref[...] = jnp.exp2(m - m_new) * acc_ref[...] + _dot(vt, p.astype(_BF16))
    return m_new


_F32_AS_BF16_PIECES = 3
_BIAS_LANES = _F32_AS_BF16_PIECES
_SUBLANES = 8


def _bias_key_lanes(lane, lane0):
    pos = lax.broadcasted_iota(jnp.int32, lane.shape, 0).astype(_F32)
    return jnp.where((lane >= lane0) & (lane < lane0 + _BIAS_LANES), pos, 0.0)


def _bias_query_lanes(slope, lane, lane0):
    rest = jnp.full(lane.shape, slope, _F32)
    out = jnp.zeros(lane.shape, _F32)
    for t in range(_BIAS_LANES):
        piece = rest.astype(_BF16).astype(_F32)
        out = jnp.where(lane == lane0 + t, piece, out)
        rest = rest - piece
    return out


def _stage_values_transposed(v_tile, vt_scr, g, n):
    vt_scr[g, n, :HEAD_DIM, :] = v_tile.T
    vt_scr[g, n, HEAD_DIM:, :] = jnp.ones((_ONES_ROWS, v_tile.shape[0]), _BF16)


def _normalised(acc):
    return acc[:HEAD_DIM] / acc[HEAD_DIM:HEAD_DIM + 1]


def _pipelined_flash(n_streams, n_past, scores, first_update, update, sa_scr, sb_scr):
    streams = range(n_streams)
    for n in streams:
        scores(n, n_past, sb_scr)
    max_a = tuple(scores(n, 0, sa_scr) for n in streams)
    running = tuple(first_update(n, sb_scr) for n in streams)

    def pair(t, carry):
        running, max_a = (list(c) for c in carry)
        j0 = 2 * t
        max_b = [None] * n_streams
        for n in streams:
            max_b[n] = scores(n, j0 + 1, sb_scr)
            running[n] = update(n, j0, sa_scr, max_a[n], running[n])
        for n in streams:
            max_a[n] = scores(n, j0 + 2, sa_scr)
            running[n] = update(n, j0 + 1, sb_scr, max_b[n], running[n])
        return tuple(running), tuple(max_a)

    def two_pairs(t, carry):
        return pair(2 * t + 1, pair(2 * t, carry))

    n_pairs = n_past // 2
    carry = lax.fori_loop(0, n_pairs // 2, two_pairs, (running, max_a))
    running, max_a = lax.fori_loop(n_pairs - n_pairs % 2, n_pairs, pair, carry)

    @pl.when(n_past % 2 == 1)
    def _():
        for n in streams:
            update(n, n_past - 1, sa_scr, max_a[n], running[n])


def _head_lanes(g):
    return slice(g * HEAD_DIM, (g + 1) * HEAD_DIM)


def _head_group_spec(rows, width, first_blk, hps):
    assert first_blk % hps == 0
    return pl.BlockSpec((None, rows, width),
                        lambda bi, hi, i: (bi, i, first_blk // hps + hi))


def _diff_attention_kernel(slopes_ref, lq1_ref, lk1_ref, lq2_ref, lk2_ref, subg_ref,
                           q_ref, k_ref, v_ref, gate_ref, o_ref,
                           vt_scr, ka_scr, acc_scr, sa_scr, sb_scr,
                           *, lam_init, tq, heads_per_step, q_tiles_per_step):
    for t in range(q_tiles_per_step):
        _diff_q_tile(pl.program_id(2) * q_tiles_per_step + t, slice(t * tq, (t + 1) * tq),
                     slopes_ref, lq1_ref, lk1_ref, lq2_ref, lk2_ref, subg_ref,
                     q_ref, k_ref, v_ref, gate_ref, o_ref,
                     vt_scr, ka_scr, acc_scr.at[t], sa_scr, sb_scr,
                     lam_init=lam_init, tq=tq, heads_per_step=heads_per_step)


def _diff_q_tile(i, rows, slopes_ref, lq1_ref, lk1_ref, lq2_ref, lk2_ref, subg_ref,
                 q_ref, k_ref, v_ref, gate_ref, o_ref, vt_scr, ka_scr, acc_scr, sa_scr, sb_scr,
                 *, lam_init, tq, heads_per_step):
    heads = range(heads_per_step)
    lane = lax.broadcasted_iota(jnp.int32, (tq, HEAD_DIM), 1)
    own_half = (lane < DIFF_HALF, lane >= DIFF_HALF)
    bias_lane0 = (DIFF_HALF, 0)

    for g in heads:
        _stage_values_transposed(v_ref[rows, _head_lanes(g)], vt_scr, g, i)
        k = k_ref[rows, _head_lanes(g)].astype(_F32)
        for c in range(2):
            ka_scr[g, c, i] = jnp.where(
                own_half[c], k, _bias_key_lanes(lane, bias_lane0[c])).astype(_BF16)

    key_idx = lax.broadcasted_iota(jnp.int32, (tq, tq), 0)
    qry_idx = lax.broadcasted_iota(jnp.int32, (tq, tq), 1)
    qry_pos = lax.broadcasted_iota(jnp.int32, (1, tq), 1).astype(_F32)

    streams = []
    for g in heads:
        slope = slopes_ref[pl.program_id(1) * heads_per_step + g] * _LOG2E
        q = q_ref[rows, _head_lanes(g)].astype(_F32)
        for c in range(2):
            qc = jnp.where(own_half[c], q, _bias_query_lanes(slope, lane, bias_lane0[c]))
            streams.append((g, c, slope, qc.astype(_BF16)))

    def scores(n, j, s_buf):
        g, c, _, qc = streams[n]
        s = _dot_nt(ka_scr[g, c, j], qc)
        s_buf[n] = s
        return jnp.max(s, axis=0, keepdims=True)

    def first_update(n, s_buf):
        g, c, slope, _ = streams[n]
        z = jnp.where(key_idx <= qry_idx, s_buf[n], _NEG_INF)
        return _first_flash_step(z, -slope * qry_pos, vt_scr[g, i], acc_scr.at[g, c])

    def update(n, j, s_buf, tile_max, running_max):
        g, c, slope, _ = streams[n]
        rb = -slope * qry_pos - slope * ((i - j) * tq).astype(_F32)
        return _flash_step(s_buf.at[n], tile_max, rb, vt_scr[g, j], running_max,
                           acc_scr.at[g, c])

    _pipelined_flash(len(streams), i, scores, first_update, update, sa_scr, sb_scr)

    lam = (jnp.exp(jnp.sum(lq1_ref[...] * lk1_ref[...], axis=-1, keepdims=True))
           - jnp.exp(jnp.sum(lq2_ref[...] * lk2_ref[...], axis=-1, keepdims=True))
           + lam_init)
    for g in heads:
        o_t = _normalised(acc_scr[g, 0]) - lam * _normalised(acc_scr[g, 1])
        ms = jnp.mean(o_t * o_t, axis=0, keepdims=True)
        o = (o_t * lax.rsqrt(ms + EPS)).T * subg_ref[...] * (1.0 - lam_init)
        gate = gate_ref[rows, _head_lanes(g)].astype(_F32)
        o_ref[rows, _head_lanes(g)] = (o * _silu(gate)).astype(o_ref.dtype)


def diff_attention(proj, slopes, lq1, lk1, lq2, lk2, subg, *, lam_init, tq, heads_per_step,
                   q_tiles_per_step):
    b, s, _ = proj.shape
    hps = heads_per_step
    step_rows = q_tiles_per_step * tq
    assert s % step_rows == 0 and DIFF_HEADS % hps == 0
    width = hps * HEAD_DIM
    vec = lambda n: pl.BlockSpec((1, n), lambda bi, hi, i: (0, 0))
    spec = functools.partial(_head_group_spec, step_rows, width, hps=hps)
    scratch = [pltpu.VMEM((hps, s // tq, _VT_ROWS, tq), _BF16),
               pltpu.VMEM((hps, 2, s // tq, tq, HEAD_DIM), _BF16),
               pltpu.VMEM((q_tiles_per_step, hps, 2, _VT_ROWS, tq), _F32),
               pltpu.VMEM((2 * hps, tq, tq), _F32),
               pltpu.VMEM((2 * hps, tq, tq), _F32)]
    return pl.pallas_call(
        functools.partial(_diff_attention_kernel, lam_init=lam_init, tq=tq,
                          heads_per_step=hps, q_tiles_per_step=q_tiles_per_step),
        grid=(b, DIFF_HEADS // hps, s // step_rows),
        in_specs=[
            pl.BlockSpec(memory_space=pltpu.SMEM),
            vec(DIFF_HALF), vec(DIFF_HALF), vec(DIFF_HALF), vec(DIFF_HALF), vec(HEAD_DIM),
            spec(first_blk=_DQ_BLK), spec(first_blk=_DK_BLK), spec(first_blk=_DV_BLK),
            spec(first_blk=_GATE_BLK),
        ],
        out_specs=pl.BlockSpec((None, step_rows, width), lambda bi, hi, i: (bi, i, hi)),
        out_shape=jax.ShapeDtypeStruct((b, s, DIFF_WIDTH), _BF16),
        scratch_shapes=scratch,
        compiler_params=pltpu.CompilerParams(
            dimension_semantics=("parallel", "parallel", "arbitrary"),
            vmem_limit_bytes=_VMEM_LIMIT_DIFF_ATTENTION),
        name="diff_attention",
    )(slopes, lq1.reshape(1, -1), lk1.reshape(1, -1), lq2.reshape(1, -1), lk2.reshape(1, -1),
      subg.reshape(1, -1), proj, proj, proj, proj)


def _moba_attention_kernel(slopes_ref, q_ref, k_ref, v_ref, gate_ref, o_ref,
                           kmean_scr, k_scr, vt_scr, sel_scr, acc_scr, sa_scr, sb_scr,
                           *, n_blocks, heads_per_step, q_tiles_per_step):
    step = pl.program_id(2)

    @pl.when(step == 0)
    def _():
        kmean_scr[...] = jnp.zeros(kmean_scr.shape, _F32)

    for t in range(q_tiles_per_step):
        _moba_q_tile(step * q_tiles_per_step + t, slice(t * MOBA_BLOCK, (t + 1) * MOBA_BLOCK),
                     slopes_ref, q_ref, k_ref, v_ref, gate_ref, o_ref,
                     kmean_scr, k_scr, vt_scr, sel_scr, acc_scr.at[t], sa_scr, sb_scr,
                     n_blocks=n_blocks, heads_per_step=heads_per_step)


def _moba_q_tile(i, rows, slopes_ref, q_ref, k_ref, v_ref, gate_ref, o_ref,
                 kmean_scr, k_scr, vt_scr, sel_scr, acc_scr, sa_scr, sb_scr,
                 *, n_blocks, heads_per_step):
    tq = MOBA_BLOCK
    heads = range(heads_per_step)

    lane = lax.broadcasted_iota(jnp.int32, (tq, HEAD_DIM), 1)

    for g in heads:
        _stage_values_transposed(v_ref[rows, _head_lanes(g)], vt_scr, g, i)
        k = k_ref[rows, _head_lanes(g)]
        k_scr[g, i] = k
        kmean_scr[g, pl.ds(i, 1), :] = (jnp.sum(k.astype(_F32), axis=0, keepdims=True)
                                        * (1.0 / MOBA_BLOCK))

    key_idx = lax.broadcasted_iota(jnp.int32, (tq, tq), 0)
    qry_idx = lax.broadcasted_iota(jnp.int32, (tq, tq), 1)
    qry_pos = lax.broadcasted_iota(jnp.int32, (1, tq), 1).astype(_F32)
    blk = lax.broadcasted_iota(jnp.int32, (n_blocks, tq), 0)
    past = blk < i

    slopes = [slopes_ref[pl.program_id(1) * heads_per_step + g] * _LOG2E for g in heads]
    qs = [q_ref[rows, _head_lanes(g)] for g in heads]
    qa_ts = [jnp.concatenate([qs[g].astype(_F32), _bias_query_lanes(slopes[g], lane, 0)],
                             axis=1).T.astype(_BF16) for g in heads]
    key_pos_lanes = _bias_key_lanes(lane, 0).astype(_BF16)

    def scores(g, j, s_buf):
        ka = jnp.concatenate([k_scr[g, j], key_pos_lanes], axis=1)
        s = _dot(ka, qa_ts[g])
        s_buf[g] = s
        return jnp.max(s, axis=0, keepdims=True)

    def select_blocks(g):
        rest = kmean_scr[g]
        pieces = []
        for _ in range(_F32_AS_BF16_PIECES):
            pieces.append(rest.astype(_BF16))
            rest = rest - pieces[-1].astype(_F32)
        parts = _dot_nt(jnp.concatenate(pieces, axis=0), qs[g])
        gate = sum(parts[t * n_blocks:(t + 1) * n_blocks] for t in range(_F32_AS_BF16_PIECES))
        gate = jnp.where(past, gate, _NEG_INF)
        for r0 in range(0, n_blocks, _SUBLANES):
            tile = gate[r0:r0 + _SUBLANES]
            row = lax.broadcasted_iota(jnp.int32, tile.shape, 0) + r0
            rank = jnp.zeros(tile.shape, _F32)
            for n in range(n_blocks):
                g_n = gate[n:n + 1, :]
                if n < r0:
                    beats = g_n >= tile
                elif n >= r0 + _SUBLANES:
                    beats = g_n > tile
                else:
                    beats = (g_n > tile) | ((g_n == tile) & (n < row))
                rank = rank + jnp.where(beats, 1.0, 0.0)
            sel_scr[g, r0:r0 + _SUBLANES, :] = jnp.where(
                (row < i) & (rank < MOBA_TOPK), 0.0, _NEG_INF)

    def first_update(g, s_buf):
        select_blocks(g)
        z = jnp.where(key_idx <= qry_idx, s_buf[g], _NEG_INF)
        return _first_flash_step(z, -slopes[g] * qry_pos, vt_scr[g, i], acc_scr.at[g])

    def update(g, j, s_buf, tile_max, running_max):
        rb = (-slopes[g] * qry_pos - slopes[g] * ((i - j) * tq).astype(_F32)
              + sel_scr[g, pl.ds(j, 1), :])
        return _flash_step(s_buf.at[g], tile_max, rb, vt_scr[g, j], running_max, acc_scr.at[g])

    _pipelined_flash(heads_per_step, i, scores, first_update, update, sa_scr, sb_scr)
    for g in heads:
        o = _normalised(acc_scr[g]).T
        gate = gate_ref[rows, _head_lanes(g)].astype(_F32)
        o_ref[rows, _head_lanes(g)] = (o * _silu(gate)).astype(o_ref.dtype)


def moba_attention(proj, slopes, *, heads_per_step, q_tiles_per_step):
    b, s, _ = proj.shape
    hps = heads_per_step
    assert s % MOBA_BLOCK == 0 and MOBA_HEADS % hps == 0
    n_blocks = s // MOBA_BLOCK
    assert n_blocks % _SUBLANES == 0 and n_blocks % q_tiles_per_step == 0
    tq = MOBA_BLOCK
    width = hps * HEAD_DIM
    step_rows = q_tiles_per_step * tq
    spec = functools.partial(_head_group_spec, step_rows, width, hps=hps)
    return pl.pallas_call(
        functools.partial(_moba_attention_kernel, n_blocks=n_blocks, heads_per_step=hps,
                          q_tiles_per_step=q_tiles_per_step),
        grid=(b, MOBA_HEADS // hps, n_blocks // q_tiles_per_step),
        in_specs=[
            pl.BlockSpec(memory_space=pltpu.SMEM),
            spec(first_blk=_MQ_BLK), spec(first_blk=_MK_BLK), spec(first_blk=_MV_BLK),
            spec(first_blk=_GATE_BLK + DIFF_HEADS),
        ],
        out_specs=pl.BlockSpec((None, step_rows, width), lambda bi, hi, i: (bi, i, hi)),
        out_shape=jax.ShapeDtypeStruct((b, s, MOBA_WIDTH), _BF16),
        scratch_shapes=[pltpu.VMEM((hps, n_blocks, HEAD_DIM), _F32),
                        pltpu.VMEM((hps, n_blocks, tq, HEAD_DIM), _BF16),
                        pltpu.VMEM((hps, n_blocks, _VT_ROWS, tq), _BF16),
                        pltpu.VMEM((hps, n_blocks, tq), _F32),
                        pltpu.VMEM((q_tiles_per_step, hps, _VT_ROWS, tq), _F32),
                        pltpu.VMEM((hps, tq, tq), _F32),
                        pltpu.VMEM((hps, tq, tq), _F32)],
        compiler_params=pltpu.CompilerParams(
            dimension_semantics=("parallel", "parallel", "arbitrary"),
            vmem_limit_bytes=_VMEM_LIMIT),
        name="moba_attention",
    )(slopes, proj, proj, proj, proj)


def _layer_tail_kernel(yd_ref, ym_ref, x_ref, wa_ref, wb_ref, g_ref, wq_ref, kv_ref, wo_ref,
                       fg_ref, o_ref, q_scr, o_scr, *, apply_final, sub_rows):
    subs = [slice(r, r + sub_rows) for r in range(0, x_ref.shape[0], sub_rows)]
    xs = [x_ref[r, :] + _dot(yd_ref[r, :], wa_ref[...]) + _dot(ym_ref[r, :], wb_ref[...])
          for r in subs]
    for r, x in zip(subs, xs):
        ms = jnp.mean(x * x, axis=-1, keepdims=True)
        h = (x * lax.rsqrt(ms + EPS) * g_ref[...]).astype(_BF16)
        q_scr[r, :] = (_dot(h, wq_ref[...]) * (HEAD_DIM ** -0.5 * _LOG2E)).astype(_BF16)
    for r in subs:
        for hh in range(MEM_HEADS):
            lo, hi = hh * HEAD_DIM, (hh + 1) * HEAD_DIM
            s = _dot_nt(q_scr[r, lo:hi], kv_ref[:, lo:hi])
            p = jnp.exp2(s - jnp.max(s, axis=-1, keepdims=True))
            l = jnp.sum(p, axis=-1, keepdims=True)
            o_h = _dot(p.astype(_BF16), kv_ref[:, MEM_WIDTH + lo:MEM_WIDTH + hi]) / l
            o_scr[r, lo:hi] = o_h.astype(_BF16)
    for r, x in zip(subs, xs):
        x = x + _dot(o_scr[r, :], wo_ref[...])
        if apply_final:
            ms = jnp.mean(x * x, axis=-1, keepdims=True)
            x = x * lax.rsqrt(ms + EPS) * fg_ref[...]
        o_ref[r, :] = x


def layer_tail(yd, ym, x, w_out, g, wq, kv, wo, final_g, *, apply_final, tm, sub_rows):
    b, s, d = x.shape
    mlen = kv.shape[1]
    ka, kb = yd.shape[-1], ym.shape[-1]
    assert s % tm == 0 and ka == kb and w_out.shape[0] == ka + kb
    rows = lambda width: pl.BlockSpec((None, tm, width), lambda bi, i: (bi, i, 0))
    whole = lambda shape: pl.BlockSpec(shape, lambda bi, i: (0,) * len(shape))
    return pl.pallas_call(
        functools.partial(_layer_tail_kernel, apply_final=apply_final, sub_rows=sub_rows),
        grid=(b, s // tm),
        in_specs=[
            rows(ka), rows(kb), rows(d),
            pl.BlockSpec((ka, d), lambda bi, i: (0, 0)),
            pl.BlockSpec((kb, d), lambda bi, i: (1, 0)),
            whole((1, d)), whole((d, MEM_WIDTH)),
            pl.BlockSpec((None, mlen, 2 * MEM_WIDTH), lambda bi, i: (bi, 0, 0)),
            whole((MEM_WIDTH, d)), whole((1, d)),
        ],
        out_specs=rows(d),
        out_shape=jax.ShapeDtypeStruct((b, s, d), _F32),
        scratch_shapes=[pltpu.VMEM((tm, MEM_WIDTH), _BF16),
                        pltpu.VMEM((tm, MEM_WIDTH), _BF16)],
        compiler_params=pltpu.CompilerParams(
            dimension_semantics=("parallel", "parallel"),
            vmem_limit_bytes=_VMEM_LIMIT),
        name="layer_tail",
    )(yd, ym, x, w_out, w_out, g.reshape(1, d), wq, kv, wo, final_g.reshape(1, d))


def _in_proj_col_scale(n_cols):
    cs = np.ones((n_cols,), np.float32)
    cs[_DQ_BLK * HEAD_DIM:_DK_BLK * HEAD_DIM] = DIFF_HALF ** -0.5 * _LOG2E
    cs[_MQ_BLK * HEAD_DIM:_MK_BLK * HEAD_DIM] = HEAD_DIM ** -0.5 * _LOG2E
    return jnp.asarray(cs)


def _alibi_slopes(n):
    return jnp.asarray(2.0 ** (-8.0 * np.arange(1, n + 1) / n), dtype=_F32)


_IN_PROJ_TILE = (2048, 512)
_ATTN_Q_TILE = 256
_DIFF_HEADS_PER_STEP = 8
_MOBA_HEADS_PER_STEP = 8
_ATTN_Q_TILES_PER_STEP = 2
_TAIL_ROW_TILE = 512
_TAIL_SUB_ROWS = 256


def kernel(x, mem, norm_mix_g, w_in, lambda_q1, lambda_k1, lambda_q2, lambda_k2, subln_g,
           w_out, norm_mem_g, mem_norm_g, w_q_mem, w_kv_mem, w_o_mem, final_g):
    b, s, d = x.shape
    mlen = mem.shape[1]
    depth = w_in.shape[0]
    diff_slopes = _alibi_slopes(DIFF_HEADS)
    moba_slopes = _alibi_slopes(MOBA_HEADS)
    mem2 = mem.reshape(b * mlen, d)
    for l in range(depth):
        lam_init = 0.8 - 0.6 * math.exp(-0.3 * l)
        proj = rms_matmul(x.reshape(b * s, d), norm_mix_g[l], w_in[l],
                          _in_proj_col_scale(w_in.shape[-1]),
                          tm=_IN_PROJ_TILE[0], tn=_IN_PROJ_TILE[1],
                          vmem_limit=_VMEM_LIMIT_DIFF_ATTENTION).reshape(b, s, -1)
        y_d = diff_attention(proj, diff_slopes, lambda_q1[l], lambda_k1[l], lambda_q2[l],
                             lambda_k2[l], subln_g[l], lam_init=lam_init, tq=_ATTN_Q_TILE,
                             heads_per_step=_DIFF_HEADS_PER_STEP,
                             q_tiles_per_step=_ATTN_Q_TILES_PER_STEP)
        y_m = moba_attention(proj, moba_slopes, heads_per_step=_MOBA_HEADS_PER_STEP,
                             q_tiles_per_step=_ATTN_Q_TILES_PER_STEP)
        kv = rms_matmul(mem2, mem_norm_g[l], w_kv_mem[l],
                        tm=b * mlen, tn=2 * MEM_WIDTH).reshape(b, mlen, 2 * MEM_WIDTH)
        x = layer_tail(y_d, y_m, x, w_out[l].astype(_BF16), norm_mem_g[l],
                       w_q_mem[l].astype(_BF16), kv, w_o_mem[l].astype(_BF16), final_g,
                       apply_final=(l == depth - 1), tm=_TAIL_ROW_TILE,
                       sub_rows=_TAIL_SUB_ROWS)
    return x
```

```python
import functools
import math

import jax
import jax.numpy as jnp
import numpy as np
from jax import lax
from jax.experimental import pallas as pl
from jax.experimental.pallas import tpu as pltpu

HEAD_DIM = 128
DIFF_HEADS = 8
DIFF_HALF = HEAD_DIM // 2
DIFF_WIDTH = DIFF_HEADS * HEAD_DIM
MOBA_HEADS = 8
MOBA_WIDTH = MOBA_HEADS * HEAD_DIM
MOBA_BLOCK = 256
MOBA_TOPK = 3
MEM_HEADS = 4
MEM_WIDTH = MEM_HEADS * HEAD_DIM
EPS = 1e-6

_DQ_BLK = 0
_DK_BLK = DIFF_HEADS
_DV_BLK = 2 * DIFF_HEADS
_MQ_BLK = 3 * DIFF_HEADS
_MK_BLK = 3 * DIFF_HEADS + MOBA_HEADS
_MV_BLK = 3 * DIFF_HEADS + 2 * MOBA_HEADS
_GATE_BLK = 3 * DIFF_HEADS + 3 * MOBA_HEADS

_V7X_VMEM_BYTES = 64 * 1024 * 1024
_VMEM_LIMIT = _V7X_VMEM_BYTES * 3 // 4
_VMEM_LIMIT_DIFF_ATTENTION = _V7X_VMEM_BYTES * 7 // 8

_BF16 = jnp.bfloat16
_F32 = jnp.float32
_NEG_INF = float("-inf")


def _dot_nt(a, b):
    return lax.dot_general(a, b, (((1,), (1,)), ((), ())), preferred_element_type=_F32)


def _dot(a, b):
    return jnp.dot(a, b, preferred_element_type=_F32)


def _silu(g):
    return g * jax.nn.sigmoid(g)


_NORM_CHUNK_ROWS = 256

def _rms_matmul_kernel(x_ref, g_ref, w_ref, *rest, has_col_scale, norm_rows):
    cs_ref, o_ref, h_scr = rest if has_col_scale else (None,) + rest

    def project(rows):
        out = _dot(h_scr[rows, :], w_ref[...].astype(_BF16))
        if has_col_scale:
            out = out * cs_ref[...]
        o_ref[rows, :] = out.astype(o_ref.dtype)

    @pl.when(pl.program_id(1) == 0)
    def _():
        for r0 in range(0, x_ref.shape[0], norm_rows):
            rows = slice(r0, r0 + norm_rows)
            x = x_ref[rows, :]
            ms = jnp.mean(x * x, axis=-1, keepdims=True)
            h_scr[rows, :] = (x * lax.rsqrt(ms + EPS) * g_ref[...]).astype(_BF16)
            project(rows)

    @pl.when(pl.program_id(1) != 0)
    def _():
        project(slice(None))


def rms_matmul(x, g, w, col_scale=None, *, tm, tn, out_dtype=_BF16):
    m, k = x.shape
    n = w.shape[1]
    assert m % tm == 0 and n % tn == 0
    has_cs = col_scale is not None
    cs_specs = [pl.BlockSpec((1, tn), lambda i, j: (0, j))] if has_cs else []
    cs_args = [col_scale.reshape(1, n)] if has_cs else []
    norm_rows = min(tm, _NORM_CHUNK_ROWS)
    assert tm % norm_rows == 0
    return pl.pallas_call(
        functools.partial(_rms_matmul_kernel, has_col_scale=has_cs, norm_rows=norm_rows),
        grid=(m // tm, n // tn),
        in_specs=[
            pl.BlockSpec((tm, k), lambda i, j: (i, 0)),
            pl.BlockSpec((1, k), lambda i, j: (0, 0)),
            pl.BlockSpec((k, tn), lambda i, j: (0, j)),
        ] + cs_specs,
        out_specs=pl.BlockSpec((tm, tn), lambda i, j: (i, j)),
        out_shape=jax.ShapeDtypeStruct((m, n), out_dtype),
        scratch_shapes=[pltpu.VMEM((tm, k), _BF16)],
        compiler_params=pltpu.CompilerParams(
            dimension_semantics=("parallel", "arbitrary"),
            vmem_limit_bytes=_VMEM_LIMIT),
        name="rms_matmul",
    )(x, g.reshape(1, k), w, *cs_args)


_LOG2E = math.log2(math.e)
_ONES_ROWS = 16
_VT_ROWS = HEAD_DIM + _ONES_ROWS


def _first_flash_step(z, rb, vt, acc_ref):
    m = jnp.max(z, axis=0, keepdims=True) + rb
    p = jnp.exp2(z - (m - rb))
    acc_ref[...] = _dot(vt, p.astype(_BF16))
    return m


def _flash_step(z_ref, z_max, rb, vt, m, acc_ref):
    m_new = jnp.maximum(m, z_max + rb)
    p = jnp.exp2(z_ref[...] - (m_new - rb))
    acc_ref[...] = jnp.exp2(m - m_new) * acc_ref[...] + _dot(vt, p.astype(_BF16))
    return m_new


_F32_AS_BF16_PIECES = 3
_BIAS_LANES = _F32_AS_BF16_PIECES
_SUBLANES = 8


def _bias_key_lanes(lane, lane0):
    pos = lax.broadcasted_iota(jnp.int32, lane.shape, 0).astype(_F32)
    return jnp.where((lane >= lane0) & (lane < lane0 + _BIAS_LANES), pos, 0.0)


def _bias_query_lanes(slope, lane, lane0):
    rest = jnp.full(lane.shape, slope, _F32)
    out = jnp.zeros(lane.shape, _F32)
    for t in range(_BIAS_LANES):
        piece = rest.astype(_BF16).astype(_F32)
        out = jnp.where(lane == lane0 + t, piece, out)
        rest = rest - piece
    return out


def _stage_values_transposed(v_tile, vt_scr, g, n):
    vt_scr[g, n, :HEAD_DIM, :] = v_tile.T
    vt_scr[g, n, HEAD_DIM:, :] = jnp.ones((_ONES_ROWS, v_tile.shape[0]), _BF16)


def _normalised(acc):
    return acc[:HEAD_DIM] / acc[HEAD_DIM:HEAD_DIM + 1]


def _pipelined_flash(n_streams, n_past, scores, first_update, update, sa_scr, sb_scr):
    streams = range(n_streams)
    for n in streams:
        scores(n, n_past, sb_scr)
    max_a = tuple(scores(n, 0, sa_scr) for n in streams)
    running = tuple(first_update(n, sb_scr) for n in streams)

    def pair(t, carry):
        running, max_a = (list(c) for c in carry)
        j0 = 2 * t
        max_b = [None] * n_streams
        for n in streams:
            max_b[n] = scores(n, j0 + 1, sb_scr)
            running[n] = update(n, j0, sa_scr, max_a[n], running[n])
        for n in streams:
            max_a[n] = scores(n, j0 + 2, sa_scr)
            running[n] = update(n, j0 + 1, sb_scr, max_b[n], running[n])
        return tuple(running), tuple(max_a)

    def two_pairs(t, carry):
        return pair(2 * t + 1, pair(2 * t, carry))

    n_pairs = n_past // 2
    carry = lax.fori_loop(0, n_pairs // 2, two_pairs, (running, max_a))
    running, max_a = lax.fori_loop(n_pairs - n_pairs % 2, n_pairs, pair, carry)

    @pl.when(n_past % 2 == 1)
    def _():
        for n in streams:
            update(n, n_past - 1, sa_scr, max_a[n], running[n])


def _head_lanes(g):
    return slice(g * HEAD_DIM, (g + 1) * HEAD_DIM)


def _head_group_spec(rows, width, first_blk, hps):
    assert first_blk % hps == 0
    return pl.BlockSpec((None, rows, width),
                        lambda bi, hi, i: (bi, i, first_blk // hps + hi))


def _diff_attention_kernel(slopes_ref, lq1_ref, lk1_ref, lq2_ref, lk2_ref, subg_ref,
                           q_ref, k_ref, v_ref, gate_ref, o_ref,
                           vt_scr, ka_scr, acc_scr, sa_scr, sb_scr,
                           *, lam_init, tq, heads_per_step, q_tiles_per_step):
    for t in range(q_tiles_per_step):
        _diff_q_tile(pl.program_id(2) * q_tiles_per_step + t, slice(t * tq, (t + 1) * tq),
                     slopes_ref, lq1_ref, lk1_ref, lq2_ref, lk2_ref, subg_ref,
                     q_ref, k_ref, v_ref, gate_ref, o_ref,
                     vt_scr, ka_scr, acc_scr.at[t], sa_scr, sb_scr,
                     lam_init=lam_init, tq=tq, heads_per_step=heads_per_step)


def _diff_q_tile(i, rows, slopes_ref, lq1_ref, lk1_ref, lq2_ref, lk2_ref, subg_ref,
                 q_ref, k_ref, v_ref, gate_ref, o_ref, vt_scr, ka_scr, acc_scr, sa_scr, sb_scr,
                 *, lam_init, tq, heads_per_step):
    heads = range(heads_per_step)
    lane = lax.broadcasted_iota(jnp.int32, (tq, HEAD_DIM), 1)
    own_half = (lane < DIFF_HALF, lane >= DIFF_HALF)
    bias_lane0 = (DIFF_HALF, 0)

    for g in heads:
        _stage_values_transposed(v_ref[rows, _head_lanes(g)], vt_scr, g, i)
        k = k_ref[rows, _head_lanes(g)].astype(_F32)
        for c in range(2):
            ka_scr[g, c, i] = jnp.where(
                own_half[c], k, _bias_key_lanes(lane, bias_lane0[c])).astype(_BF16)

    key_idx = lax.broadcasted_iota(jnp.int32, (tq, tq), 0)
    qry_idx = lax.broadcasted_iota(jnp.int32, (tq, tq), 1)
    qry_pos = lax.broadcasted_iota(jnp.int32, (1, tq), 1).astype(_F32)

    streams = []
    for g in heads:
        slope = slopes_ref[pl.program_id(1) * heads_per_step + g] * _LOG2E
        q = q_ref[rows, _head_lanes(g)].astype(_F32)
        for c in range(2):
            qc = jnp.where(own_half[c], q, _bias_query_lanes(slope, lane, bias_lane0[c]))
            streams.append((g, c, slope, qc.astype(_BF16)))

    def scores(n, j, s_buf):
        g, c, _, qc = streams[n]
        s = _dot_nt(ka_scr[g, c, j], qc)
        s_buf[n] = s
        return jnp.max(s, axis=0, keepdims=True)

    def first_update(n, s_buf):
        g, c, slope, _ = streams[n]
        z = jnp.where(key_idx <= qry_idx, s_buf[n], _NEG_INF)
        return _first_flash_step(z, -slope * qry_pos, vt_scr[g, i], acc_scr.at[g, c])

    def update(n, j, s_buf, tile_max, running_max):
        g, c, slope, _ = streams[n]
        rb = -slope * qry_pos - slope * ((i - j) * tq).astype(_F32)
        return _flash_step(s_buf.at[n], tile_max, rb, vt_scr[g, j], running_max,
                           acc_scr.at[g, c])

    _pipelined_flash(len(streams), i, scores, first_update, update, sa_scr, sb_scr)

    lam = (jnp.exp(jnp.sum(lq1_ref[...] * lk1_ref[...], axis=-1, keepdims=True))
           - jnp.exp(jnp.sum(lq2_ref[...] * lk2_ref[...], axis=-1, keepdims=True))
           + lam_init)
    for g in heads:
        o_t = _normalised(acc_scr[g, 0]) - lam * _normalised(acc_scr[g, 1])
        ms = jnp.mean(o_t * o_t, axis=0, keepdims=True)
        o = (o_t * lax.rsqrt(ms + EPS)).T * subg_ref[...] * (1.0 - lam_init)
        gate = gate_ref[rows, _head_lanes(g)].astype(_F32)
        o_ref[rows, _head_lanes(g)] = (o * _silu(gate)).astype(o_ref.dtype)


def diff_attention(proj, slopes, lq1, lk1, lq2, lk2, subg, *, lam_init, tq, heads_per_step,
                   q_tiles_per_step):
    b, s, _ = proj.shape
    hps = heads_per_step
    step_rows = q_tiles_per_step * tq
    assert s % step_rows == 0 and DIFF_HEADS % hps == 0
    width = hps * HEAD_DIM
    vec = lambda n: pl.BlockSpec((1, n), lambda bi, hi, i: (0, 0))
    spec = functools.partial(_head_group_spec, step_rows, width, hps=hps)
    scratch = [pltpu.VMEM((hps, s // tq, _VT_ROWS, tq), _BF16),
               pltpu.VMEM((hps, 2, s // tq, tq, HEAD_DIM), _BF16),
               pltpu.VMEM((q_tiles_per_step, hps, 2, _VT_ROWS, tq), _F32),
               pltpu.VMEM((2 * hps, tq, tq), _F32),
               pltpu.VMEM((2 * hps, tq, tq), _F32)]
    return pl.pallas_call(
        functools.partial(_diff_attention_kernel, lam_init=lam_init, tq=tq,
                          heads_per_step=hps, q_tiles_per_step=q_tiles_per_step),
        grid=(b, DIFF_HEADS // hps, s // step_rows),
        in_specs=[
            pl.BlockSpec(memory_space=pltpu.SMEM),
            vec(DIFF_HALF), vec(DIFF_HALF), vec(DIFF_HALF), vec(DIFF_HALF), vec(HEAD_DIM),
            spec(first_blk=_DQ_BLK), spec(first_blk=_DK_BLK), spec(first_blk=_DV_BLK),
            spec(first_blk=_GATE_BLK),
        ],
        out_specs=pl.BlockSpec((None, step_rows, width), lambda bi, hi, i: (bi, i, hi)),
        out_shape=jax.ShapeDtypeStruct((b, s, DIFF_WIDTH), _BF16),
        scratch_shapes=scratch,
        compiler_params=pltpu.CompilerParams(
            dimension_semantics=("parallel", "parallel", "arbitrary"),
            vmem_limit_bytes=_VMEM_LIMIT_DIFF_ATTENTION),
        name="diff_attention",
    )(slopes, lq1.reshape(1, -1), lk1.reshape(1, -1), lq2.reshape(1, -1), lk2.reshape(1, -1),
      subg.reshape(1, -1), proj, proj, proj, proj)


def _moba_attention_kernel(slopes_ref, q_ref, k_ref, v_ref, gate_ref, o_ref,
                           kmean_scr, k_scr, vt_scr, sel_scr, acc_scr, sa_scr, sb_scr,
                           *, n_blocks, heads_per_step, q_tiles_per_step):
    step = pl.program_id(2)

    @pl.when(step == 0)
    def _():
        kmean_scr[...] = jnp.zeros(kmean_scr.shape, _F32)

    for t in range(q_tiles_per_step):
        _moba_q_tile(step * q_tiles_per_step + t, slice(t * MOBA_BLOCK, (t + 1) * MOBA_BLOCK),
                     slopes_ref, q_ref, k_ref, v_ref, gate_ref, o_ref,
                     kmean_scr, k_scr, vt_scr, sel_scr, acc_scr.at[t], sa_scr, sb_scr,
                     n_blocks=n_blocks, heads_per_step=heads_per_step)


def _moba_q_tile(i, rows, slopes_ref, q_ref, k_ref, v_ref, gate_ref, o_ref,
                 kmean_scr, k_scr, vt_scr, sel_scr, acc_scr, sa_scr, sb_scr,
                 *, n_blocks, heads_per_step):
    tq = MOBA_BLOCK
    heads = range(heads_per_step)

    lane = lax.broadcasted_iota(jnp.int32, (tq, HEAD_DIM), 1)

    for g in heads:
        _stage_values_transposed(v_ref[rows, _head_lanes(g)], vt_scr, g, i)
        k = k_ref[rows, _head_lanes(g)]
        k_scr[g, i] = k
        kmean_scr[g, pl.ds(i, 1), :] = (jnp.sum(k.astype(_F32), axis=0, keepdims=True)
                                        * (1.0 / MOBA_BLOCK))

    key_idx = lax.broadcasted_iota(jnp.int32, (tq, tq), 0)
    qry_idx = lax.broadcasted_iota(jnp.int32, (tq, tq), 1)
    qry_pos = lax.broadcasted_iota(jnp.int32, (1, tq), 1).astype(_F32)
    blk = lax.broadcasted_iota(jnp.int32, (n_blocks, tq), 0)
    past = blk < i

    slopes = [slopes_ref[pl.program_id(1) * heads_per_step + g] * _LOG2E for g in heads]
    qs = [q_ref[rows, _head_lanes(g)] for g in heads]
    qa_ts = [jnp.concatenate([qs[g].astype(_F32), _bias_query_lanes(slopes[g], lane, 0)],
                             axis=1).T.astype(_BF16) for g in heads]
    key_pos_lanes = _bias_key_lanes(lane, 0).astype(_BF16)

    def scores(g, j, s_buf):
        ka = jnp.concatenate([k_scr[g, j], key_pos_lanes], axis=1)
        s = _dot(ka, qa_ts[g])
        s_buf[g] = s
        return jnp.max(s, axis=0, keepdims=True)

    def select_blocks(g):
        rest = kmean_scr[g]
        pieces = []
        for _ in range(_F32_AS_BF16_PIECES):
            pieces.append(rest.astype(_BF16))
            rest = rest - pieces[-1].astype(_F32)
        parts = _dot_nt(jnp.concatenate(pieces, axis=0), qs[g])
        gate = sum(parts[t * n_blocks:(t + 1) * n_blocks] for t in range(_F32_AS_BF16_PIECES))
        gate = jnp.where(past, gate, _NEG_INF)
        for r0 in range(0, n_blocks, _SUBLANES):
            tile = gate[r0:r0 + _SUBLANES]
            row = lax.broadcasted_iota(jnp.int32, tile.shape, 0) + r0
            rank = jnp.zeros(tile.shape, _F32)
            for n in range(n_blocks):
                g_n = gate[n:n + 1, :]
                if n < r0:
                    beats = g_n >= tile
                elif n >= r0 + _SUBLANES:
                    beats = g_n > tile
                else:
                    beats = (g_n > tile) | ((g_n == tile) & (n < row))
                rank = rank + jnp.where(beats, 1.0, 0.0)
            sel_scr[g, r0:r0 + _SUBLANES, :] = jnp.where(
                (row < i) & (rank < MOBA_TOPK), 0.0, _NEG_INF)

    def first_update(g, s_buf):
        select_blocks(g)
        z = jnp.where(key_idx <= qry_idx, s_buf[g], _NEG_INF)
        return _first_flash_step(z, -slopes[g] * qry_pos, vt_scr[g, i], acc_scr.at[g])

    def update(g, j, s_buf, tile_max, running_max):
        rb = (-slopes[g] * qry_pos - slopes[g] * ((i - j) * tq).astype(_F32)
              + sel_scr[g, pl.ds(j, 1), :])
        return _flash_step(s_buf.at[g], tile_max, rb, vt_scr[g, j], running_max, acc_scr.at[g])

    _pipelined_flash(heads_per_step, i, scores, first_update, update, sa_scr, sb_scr)
    for g in heads:
        o = _normalised(acc_scr[g]).T
        gate = gate_ref[rows, _head_lanes(g)].astype(_F32)
        o_ref[rows, _head_lanes(g)] = (o * _silu(gate)).astype(o_ref.dtype)


def moba_attention(proj, slopes, *, heads_per_step, q_tiles_per_step):
    b, s, _ = proj.shape
    hps = heads_per_step
    assert s % MOBA_BLOCK == 0 and MOBA_HEADS % hps == 0
    n_blocks = s // MOBA_BLOCK
    assert n_blocks % _SUBLANES == 0 and n_blocks % q_tiles_per_step == 0
    tq = MOBA_BLOCK
    width = hps * HEAD_DIM
    step_rows = q_tiles_per_step * tq
    spec = functools.partial(_head_group_spec, step_rows, width, hps=hps)
    return pl.pallas_call(
        functools.partial(_moba_attention_kernel, n_blocks=n_blocks, heads_per_step=hps,
                          q_tiles_per_step=q_tiles_per_step),
        grid=(b, MOBA_HEADS // hps, n_blocks // q_tiles_per_step),
        in_specs=[
            pl.BlockSpec(memory_space=pltpu.SMEM),
            spec(first_blk=_MQ_BLK), spec(first_blk=_MK_BLK), spec(first_blk=_MV_BLK),
            spec(first_blk=_GATE_BLK + DIFF_HEADS),
        ],
        out_specs=pl.BlockSpec((None, step_rows, width), lambda bi, hi, i: (bi, i, hi)),
        out_shape=jax.ShapeDtypeStruct((b, s, MOBA_WIDTH), _BF16),
        scratch_shapes=[pltpu.VMEM((hps, n_blocks, HEAD_DIM), _F32),
                        pltpu.VMEM((hps, n_blocks, tq, HEAD_DIM), _BF16),
                        pltpu.VMEM((hps, n_blocks, _VT_ROWS, tq), _BF16),
                        pltpu.VMEM((hps, n_blocks, tq), _F32),
                        pltpu.VMEM((q_tiles_per_step, hps, _VT_ROWS, tq), _F32),
                        pltpu.VMEM((hps, tq, tq), _F32),
                        pltpu.VMEM((hps, tq, tq), _F32)],
        compiler_params=pltpu.CompilerParams(
            dimension_semantics=("parallel", "parallel", "arbitrary"),
            vmem_limit_bytes=_VMEM_LIMIT),
        name="moba_attention",
    )(slopes, proj, proj, proj, proj)


def _layer_tail_kernel(yd_ref, ym_ref, x_ref, wa_ref, wb_ref, g_ref, wq_ref, kv_ref, wo_ref,
                       fg_ref, o_ref, q_scr, o_scr, *, apply_final, sub_rows):
    subs = [slice(r, r + sub_rows) for r in range(0, x_ref.shape[0], sub_rows)]
    xs = [x_ref[r, :] + _dot(yd_ref[r, :], wa_ref[...]) + _dot(ym_ref[r, :], wb_ref[...])
          for r in subs]
    for r, x in zip(subs, xs):
        ms = jnp.mean(x * x, axis=-1, keepdims=True)
        h = (x * lax.rsqrt(ms + EPS) * g_ref[...]).astype(_BF16)
        q_scr[r, :] = (_dot(h, wq_ref[...]) * (HEAD_DIM ** -0.5 * _LOG2E)).astype(_BF16)
    for r in subs:
        for hh in range(MEM_HEADS):
            lo, hi = hh * HEAD_DIM, (hh + 1) * HEAD_DIM
            s = _dot_nt(q_scr[r, lo:hi], kv_ref[:, lo:hi])
            p = jnp.exp2(s - jnp.max(s, axis=-1, keepdims=True))
            l = jnp.sum(p, axis=-1, keepdims=True)
            o_h = _dot(p.astype(_BF16), kv_ref[:, MEM_WIDTH + lo:MEM_WIDTH + hi]) / l
            o_scr[r, lo:hi] = o_h.astype(_BF16)
    for r, x in zip(subs, xs):
        x = x + _dot(o_scr[r, :], wo_ref[...])
        if apply_final:
            ms = jnp.mean(x * x, axis=-1, keepdims=True)
            x = x * lax.rsqrt(ms + EPS) * fg_ref[...]
        o_ref[r, :] = x


def layer_tail(yd, ym, x, w_out, g, wq, kv, wo, final_g, *, apply_final, tm, sub_rows):
    b, s, d = x.shape
    mlen = kv.shape[1]
    ka, kb = yd.shape[-1], ym.shape[-1]
    assert s % tm == 0 and ka == kb and w_out.shape[0] == ka + kb
    rows = lambda width: pl.BlockSpec((None, tm, width), lambda bi, i: (bi, i, 0))
    whole = lambda shape: pl.BlockSpec(shape, lambda bi, i: (0,) * len(shape))
    return pl.pallas_call(
        functools.partial(_layer_tail_kernel, apply_final=apply_final, sub_rows=sub_rows),
        grid=(b, s // tm),
        in_specs=[
            rows(ka), rows(kb), rows(d),
            pl.BlockSpec((ka, d), lambda bi, i: (0, 0)),
            pl.BlockSpec((kb, d), lambda bi, i: (1, 0)),
            whole((1, d)), whole((d, MEM_WIDTH)),
            pl.BlockSpec((None, mlen, 2 * MEM_WIDTH), lambda bi, i: (bi, 0, 0)),
            whole((MEM_WIDTH, d)), whole((1, d)),
        ],
        out_specs=rows(d),
        out_shape=jax.ShapeDtypeStruct((b, s, d), _F32),
        scratch_shapes=[pltpu.VMEM((tm, MEM_WIDTH), _BF16),
                        pltpu.VMEM((tm, MEM_WIDTH), _BF16)],
        compiler_params=pltpu.CompilerParams(
            dimension_semantics=("parallel", "parallel"),
            vmem_limit_bytes=_VMEM_LIMIT),
        name="layer_tail",
    )(yd, ym, x, w_out, w_out, g.reshape(1, d), wq, kv, wo, final_g.reshape(1, d))


def _in_proj_col_scale(n_cols):
    cs = np.ones((n_cols,), np.float32)
    cs[_DQ_BLK * HEAD_DIM:_DK_BLK * HEAD_DIM] = DIFF_HALF ** -0.5 * _LOG2E
    cs[_MQ_BLK * HEAD_DIM:_MK_BLK * HEAD_DIM] = HEAD_DIM ** -0.5 * _LOG2E
    return jnp.asarray(cs)


def _alibi_slopes(n):
    return jnp.asarray(2.0 ** (-8.0 * np.arange(1, n + 1) / n), dtype=_F32)


_IN_PROJ_TILE = (1024, 1024)
_ATTN_Q_TILE = 256
_DIFF_HEADS_PER_STEP = 8
_MOBA_HEADS_PER_STEP = 8
_ATTN_Q_TILES_PER_STEP = 2
_TAIL_ROW_TILE = 512
_TAIL_SUB_ROWS = 256


def kernel(x, mem, norm_mix_g, w_in, lambda_q1, lambda_k1, lambda_q2, lambda_k2, subln_g,
           w_out, norm_mem_g, mem_norm_g, w_q_mem, w_kv_mem, w_o_mem, final_g):
    b, s, d = x.shape
    mlen = mem.shape[1]
    depth = w_in.shape[0]
    diff_slopes = _alibi_slopes(DIFF_HEADS)
    moba_slopes = _alibi_slopes(MOBA_HEADS)
    mem2 = mem.reshape(b * mlen, d)
    for l in range(depth):
        lam_init = 0.8 - 0.6 * math.exp(-0.3 * l)
        proj = rms_matmul(x.reshape(b * s, d), norm_mix_g[l], w_in[l],
                          _in_proj_col_scale(w_in.shape[-1]),
                          tm=_IN_PROJ_TILE[0], tn=_IN_PROJ_TILE[1]).reshape(b, s, -1)
        y_d = diff_attention(proj, diff_slopes, lambda_q1[l], lambda_k1[l], lambda_q2[l],
                             lambda_k2[l], subln_g[l], lam_init=lam_init, tq=_ATTN_Q_TILE,
                             heads_per_step=_DIFF_HEADS_PER_STEP,
                             q_tiles_per_step=_ATTN_Q_TILES_PER_STEP)
        y_m = moba_attention(proj, moba_slopes, heads_per_step=_MOBA_HEADS_PER_STEP,
                             q_tiles_per_step=_ATTN_Q_TILES_PER_STEP)
        kv = rms_matmul(mem2, mem_norm_g[l], w_kv_mem[l],
                        tm=b * mlen, tn=2 * MEM_WIDTH).reshape(b, mlen, 2 * MEM_WIDTH)
        x = layer_tail(y_d, y_m, x, w_out[l].astype(_BF16), norm_mem_g[l],
                       w_q_mem[l].astype(_BF16), kv, w_o_mem[l].astype(_BF16), final_g,
                       apply_final=(l == depth - 1), tm=_TAIL_ROW_TILE,
                       sub_rows=_TAIL_SUB_ROWS)
    return x
```
